```python
import jax, jax.numpy as jnp
from jax import lax
import numpy as np

D_MODEL = 1024
BATCH = 8
SEQ = 2048
DEPTH = 1
DEC_BATCH = 128
DEC_SEQ = 1
PAST_LEN = 16384
PAGE_SIZE = 128

HGRN_HEADS = 8
HGRN_DK = 128
HGRN_F = HGRN_HEADS * HGRN_DK
HGRN_DV = D_MODEL // HGRN_HEADS
HGRN_V = HGRN_HEADS * HGRN_DV
HGRN_CHUNK = 64
LRU_WIDTH = D_MODEL
LRU_BLOCKS = 8
LRU_BW = LRU_WIDTH // LRU_BLOCKS
LRU_C = 8.0
CONV_W = 4
D_FF = 2816
EPS = 1e-6
IN_SPLITS = (HGRN_F, HGRN_F, HGRN_V, HGRN_V, LRU_WIDTH, LRU_WIDTH, D_MODEL, D_MODEL)
IN_WIDTH = HGRN_F * 2 + HGRN_V * 2 + LRU_WIDTH * 2 + D_MODEL * 2

kernel_name = "hgrn2_rglru_gated_macaron_step"


def rmsnorm(x, g):
    xf = x.astype(jnp.float32)
    y = xf * lax.rsqrt(jnp.mean(xf * xf, axis=-1, keepdims=True) + EPS)
    return (y * g.astype(jnp.float32)).astype(x.dtype)


def swiglu(x, w_gate, w_up, w_down):
    return (jax.nn.silu(x @ w_gate) * (x @ w_up)) @ w_down


def hgrn2_chunked(q, logf, k, v, s0):
    B, T, H, DK = q.shape
    DV = v.shape[-1]
    C = min(HGRN_CHUNK, T)
    pad = (-T) % C
    nc = (T + pad) // C

    def prep(a):
        a = jnp.pad(a, ((0, 0), (0, pad), (0, 0), (0, 0)))
        return a.reshape(B, nc, C, H, a.shape[-1]).transpose(1, 0, 3, 2, 4)

    qc, gc, kc, vc = prep(q), prep(logf), prep(k), prep(v)
    mask = jnp.tril(jnp.ones((C, C), dtype=bool))

    def step(S, inp):
        qi, gi, ki, vi = inp
        b = jnp.cumsum(gi, axis=2)
        bl = b[:, :, -1:, :]
        qe = qi * jnp.exp(b)
        ke = ki * jnp.exp(-b)
        inter = jnp.einsum('bhck,bhkv->bhcv', qe, S)
        A = jnp.where(mask, jnp.einsum('bhck,bhsk->bhcs', qe, ke), 0.0)
        intra = jnp.einsum('bhcs,bhsv->bhcv', A, vi)
        S_new = jnp.exp(bl[:, :, 0, :])[..., None] * S + jnp.einsum(
            'bhsk,bhsv->bhkv', ki * jnp.exp(bl - b), vi)
        return S_new, inter + intra

    s_fin, o = lax.scan(step, s0, (qc, gc, kc, vc))
    o = o.transpose(1, 0, 3, 2, 4).reshape(B, nc * C, H, DV)[:, :T]
    return o, s_fin


def causal_conv(x, buf, w, b):
    T = x.shape[1]
    xc = jnp.concatenate([buf.astype(x.dtype), x], axis=1)
    out = b
    for j in range(CONV_W):
        out = out + xc[:, j:j + T] * w[j]
    return out, xc[:, -(CONV_W - 1):]


def block_diag(x, w, b):
    B, T, D = x.shape
    xb = x.reshape(B, T, LRU_BLOCKS, LRU_BW)
    return jnp.einsum('btnc,ncd->btnd', xb, w).reshape(B, T, D) + b


def rglru(x, r, i, lam, h0):
    log_a = -LRU_C * r * jax.nn.softplus(-lam)
    a = jnp.exp(log_a)
    mult = jnp.sqrt(-jnp.expm1(2.0 * log_a))
    bterm = mult * (i * x)
    bterm = bterm.at[:, 0].add(a[:, 0] * h0)

    def combine(l, rr):
        a1, b1 = l
        a2, b2 = rr
        return a1 * a2, a2 * b1 + b2

    _, h = lax.associative_scan(combine, (a, bterm), axis=1)
    return h, h[:, -1]


def layer(x, s_hgrn, h_lru, c_conv, lb,
          n1, f1g, f1u, f1d, nmix, w_in, o_norm, conv_w, conv_b,
          w_a, b_a, w_x, b_x, lam, w_a_up, w_b_up, w_out, n3, f2g, f2u, f2d):
    f32 = jnp.float32
    B, T, _ = x.shape
    x = x + 0.5 * swiglu(rmsnorm(x, n1), f1g, f1u, f1d)
    h = rmsnorm(x, nmix)
    proj = h @ w_in
    idx = [int(v) for v in np.cumsum(IN_SPLITS)[:-1]]
    q, fpre, vin, gout, xr, yr, ga, gb = jnp.split(proj, idx, axis=-1)

    lbf = lb.astype(f32)
    f = lbf + (1.0 - lbf) * jax.nn.sigmoid(fpre.astype(f32))
    logf = jnp.log(f)
    kk = 1.0 - f
    qh = jax.nn.silu(q.astype(f32)).reshape(B, T, HGRN_HEADS, HGRN_DK)
    oA, s_new = hgrn2_chunked(qh, logf.reshape(B, T, HGRN_HEADS, HGRN_DK),
                              kk.reshape(B, T, HGRN_HEADS, HGRN_DK),
                              vin.astype(f32).reshape(B, T, HGRN_HEADS, HGRN_DV),
                              s_hgrn.astype(f32))
    oA = oA * lax.rsqrt(jnp.mean(oA * oA, axis=-1, keepdims=True) + EPS)
    oA = oA.reshape(B, T, HGRN_V) * o_norm.astype(f32) * jax.nn.silu(gout.astype(f32))
    oA = oA.astype(x.dtype)

    xc, c_new = causal_conv(xr, c_conv, conv_w, conv_b)
    r = jax.nn.sigmoid(block_diag(xc, w_a, b_a).astype(f32))
    ig = jax.nn.sigmoid(block_diag(xc, w_x, b_x).astype(f32))
    hs, h_new = rglru(xc.astype(f32), r, ig, lam.astype(f32), h_lru.astype(f32))
    oB = (hs.astype(x.dtype) * jax.nn.gelu(yr))

    m = jax.nn.sigmoid(ga) * (oA @ w_a_up) + jax.nn.sigmoid(gb) * (oB @ w_b_up)
    x = x + m @ w_out
    x = x + 0.5 * swiglu(rmsnorm(x, n3), f2g, f2u, f2d)
    return x, s_new, h_new, c_new


def setup_inputs(seed: int = 0) -> dict:
    key = jax.random.key(seed)
    ks = jax.random.split(key, 32)
    f32 = jnp.float32
    nrm = lambda k, shape, s: (jax.random.normal(k, shape, f32) * s)
    gain = lambda k, shape: 1.0 + 0.05 * jax.random.normal(k, shape, f32)
    u = jax.random.uniform(ks[20], (DEPTH, LRU_WIDTH), f32, 0.9, 0.999)
    sl = u ** (1.0 / LRU_C)
    lam = jnp.log(sl) - jnp.log1p(-sl)
    return {
        "x_prompt": nrm(ks[0], (BATCH, SEQ, D_MODEL), 1.0),
        "x_sample": nrm(ks[1], (DEC_BATCH, DEC_SEQ, D_MODEL), 1.0),
        "state_hgrn": nrm(ks[2], (DEPTH, DEC_BATCH, HGRN_HEADS, HGRN_DK, HGRN_DV), 0.5),
        "state_lru": nrm(ks[3], (DEPTH, DEC_BATCH, LRU_WIDTH), 0.5),
        "state_conv": nrm(ks[4], (DEPTH, DEC_BATCH, CONV_W - 1, LRU_WIDTH), 1.0),
        "ffn1_norm": gain(ks[5], (DEPTH, D_MODEL)),
        "ffn1_w_gate": nrm(ks[6], (DEPTH, D_MODEL, D_FF), D_MODEL ** -0.5),
        "ffn1_w_up": nrm(ks[7], (DEPTH, D_MODEL, D_FF), D_MODEL ** -0.5),
        "ffn1_w_down": nrm(ks[8], (DEPTH, D_FF, D_MODEL), D_FF ** -0.5),
        "mix_norm": gain(ks[9], (DEPTH, D_MODEL)),
        "w_in": nrm(ks[10], (DEPTH, D_MODEL, IN_WIDTH), D_MODEL ** -0.5),
        "hgrn_lower_bounds": nrm(ks[11], (DEPTH + 1, HGRN_F), 0.1),
        "hgrn_out_norm": gain(ks[12], (DEPTH, HGRN_V)),
        "conv_w": nrm(ks[13], (DEPTH, CONV_W, LRU_WIDTH), CONV_W ** -0.5),
        "conv_b": nrm(ks[14], (DEPTH, LRU_WIDTH), 0.01),
        "lru_w_a": nrm(ks[15], (DEPTH, LRU_BLOCKS, LRU_BW, LRU_BW), LRU_BW ** -0.5),
        "lru_b_a": nrm(ks[16], (DEPTH, LRU_WIDTH), 0.01),
        "lru_w_x": nrm(ks[17], (DEPTH, LRU_BLOCKS, LRU_BW, LRU_BW), LRU_BW ** -0.5),
        "lru_b_x": nrm(ks[18], (DEPTH, LRU_WIDTH), 0.01),
        "lru_lambda": lam,
        "w_a_up": nrm(ks[21], (DEPTH, HGRN_V, D_MODEL), HGRN_V ** -0.5),
        "w_b_up": nrm(ks[22], (DEPTH, LRU_WIDTH, D_MODEL), LRU_WIDTH ** -0.5),
        "w_out": nrm(ks[23], (DEPTH, D_MODEL, D_MODEL), D_MODEL ** -0.5),
        "ffn2_norm": gain(ks[24], (DEPTH, D_MODEL)),
        "ffn2_w_gate": nrm(ks[25], (DEPTH, D_MODEL, D_FF), D_MODEL ** -0.5),
        "ffn2_w_up": nrm(ks[26], (DEPTH, D_MODEL, D_FF), D_MODEL ** -0.5),
        "ffn2_w_down": nrm(ks[27], (DEPTH, D_FF, D_MODEL), D_FF ** -0.5),
        "final_norm": gain(ks[28], (D_MODEL,)),
    }


def reference(x_prompt, x_sample, state_hgrn, state_lru, state_conv,
              ffn1_norm, ffn1_w_gate, ffn1_w_up, ffn1_w_down, mix_norm, w_in,
              hgrn_lower_bounds, hgrn_out_norm, conv_w, conv_b,
              lru_w_a, lru_b_a, lru_w_x, lru_b_x, lru_lambda,
              w_a_up, w_b_up, w_out, ffn2_norm, ffn2_w_gate, ffn2_w_up, ffn2_w_down,
              final_norm):
    lb_all = jnp.cumsum(jax.nn.softmax(hgrn_lower_bounds.astype(jnp.float32), axis=0), axis=0)

    def run(x, s_in, h_in, c_in):
        s_out, h_out, c_out = [], [], []
        for l in range(DEPTH):
            x, s, h, c = layer(
                x, s_in[l], h_in[l], c_in[l], lb_all[l],
                ffn1_norm[l], ffn1_w_gate[l], ffn1_w_up[l], ffn1_w_down[l], mix_norm[l], w_in[l],
                hgrn_out_norm[l], conv_w[l], conv_b[l], lru_w_a[l], lru_b_a[l], lru_w_x[l], lru_b_x[l],
                lru_lambda[l], w_a_up[l], w_b_up[l], w_out[l],
                ffn2_norm[l], ffn2_w_gate[l], ffn2_w_up[l], ffn2_w_down[l])
            s_out.append(s)
            h_out.append(h)
            c_out.append(c)
        return rmsnorm(x, final_norm), jnp.stack(s_out), jnp.stack(h_out), jnp.stack(c_out)

    bp = x_prompt.shape[0]
    s0 = jnp.zeros((DEPTH, bp, HGRN_HEADS, HGRN_DK, HGRN_DV), jnp.float32)
    h0 = jnp.zeros((DEPTH, bp, LRU_WIDTH), jnp.float32)
    c0 = jnp.zeros((DEPTH, bp, CONV_W - 1, LRU_WIDTH), x_prompt.dtype)
    y_prompt, hgrn_p, lru_p, conv_p = run(x_prompt, s0, h0, c0)
    y_sample, hgrn_s, lru_s, conv_s = run(x_sample, state_hgrn, state_lru, state_conv)
    return (y_prompt, y_sample, hgrn_p, lru_p, conv_p, hgrn_s, lru_s, conv_s)
```

```python
import functools

import jax
import jax.numpy as jnp
from jax import lax
from jax.experimental import pallas as pl
from jax.experimental.pallas import tpu as pltpu

D_MODEL = 1024
HEADS = 8
DK = 128
DV = 128
CHUNK = 64
LRU_BLOCKS = 8
LRU_BW = 128
LRU_C = 8.0
CONV_W = 4
D_FF = 2816
EPS = 1e-6
N_SPLITS = 8

SUBLANES = 8
LANES = 128
VMEM_LIMIT_BYTES = 56 * 1024 * 1024

FFN_ROWS = 512
MIX_ROWS = 256
SAMPLE_STATE_BLOCK = 8

BF16 = jnp.bfloat16
F32 = jnp.float32


def _dot(a, b):
    return jnp.dot(a, b, preferred_element_type=F32)


def _dot_tn(a, b):
    return lax.dot_general(a, b, (((0,), (0,)), ((), ())), preferred_element_type=F32)


def _dot_nt(a, b):
    return lax.dot_general(a, b, (((1,), (1,)), ((), ())), preferred_element_type=F32)


def _sigmoid(x):
    return 1.0 / (1.0 + jnp.exp(-x))


def _silu(x):
    return x * _sigmoid(x)


def _gelu_tanh(x):
    c = 0.7978845608028654
    return 0.5 * x * (1.0 + jnp.tanh(c * (x + 0.044715 * (x * x * x))))


def _rms(x, g):
    return x * lax.rsqrt(jnp.mean(x * x, axis=-1, keepdims=True) + EPS) * g


def _lower_bound(raw, layer):
    m = jnp.max(raw, axis=0, keepdims=True)
    e = jnp.exp(raw - m)
    den = jnp.sum(e, axis=0, keepdims=True)
    num = jnp.sum(e[0:layer + 1], axis=0, keepdims=True)
    return num / den


def _softplus(x):
    return jnp.maximum(x, 0.0) + jnp.log1p(jnp.exp(-jnp.abs(x)))


def _lru_coeffs(r_pre, i_pre, xc, lam):
    r = _sigmoid(r_pre)
    ig = _sigmoid(i_pre)
    log_a = (-LRU_C) * r * _softplus(-lam)
    a = jnp.exp(log_a)
    th = jnp.tanh(log_a)
    mult = jnp.sqrt(-2.0 * th / (1.0 - th))
    return a, mult * (ig * xc)


def _split3(x):
    p1 = x.astype(BF16)
    r1 = x - p1.astype(F32)
    p2 = r1.astype(BF16)
    r2 = r1 - p2.astype(F32)
    return p1, p2, r2.astype(BF16)


def _ffn_kernel(x_ref, g_ref, wg_ref, wu_ref, wd_ref, gf_ref, o_ref, *, final_norm):
    x = x_ref[...]
    h = _rms(x, g_ref[...]).astype(BF16)
    gate = _dot(h, wg_ref[...])
    up = _dot(h, wu_ref[...])
    act = (_silu(gate) * up).astype(BF16)
    y = x + 0.5 * _dot(act, wd_ref[...])
    if final_norm:
        y = _rms(y, gf_ref[...])
    o_ref[...] = y


def _resident(shape):
    nd = len(shape)
    return pl.BlockSpec(shape, lambda *_: (0,) * nd, pipeline_mode=pl.Buffered(1))


def _ffn(x2d, g, wg, wu, wd, gf, *, final_norm):
    n = x2d.shape[0]
    rows = min(FFN_ROWS, n)
    assert n % rows == 0
    return pl.pallas_call(
        functools.partial(_ffn_kernel, final_norm=final_norm),
        grid=(n // rows,),
        in_specs=[
            pl.BlockSpec((rows, D_MODEL), lambda i: (i, 0)),
            _resident((1, D_MODEL)),
            _resident((D_MODEL, D_FF)),
            _resident((D_MODEL, D_FF)),
            _resident((D_FF, D_MODEL)),
            _resident((1, D_MODEL)),
        ],
        out_specs=pl.BlockSpec((rows, D_MODEL), lambda i: (i, 0)),
        out_shape=jax.ShapeDtypeStruct((n, D_MODEL), F32),
        name="ffn_final" if final_norm else "ffn",
        compiler_params=pltpu.CompilerParams(
            dimension_semantics=("arbitrary",), vmem_limit_bytes=VMEM_LIMIT_BYTES),
    )(x2d, g, wg, wu, wd, gf)


SCAN_PAD = MIX_ROWS // 2
CONV_PAD = SUBLANES


def _mix_prompt_kernel(x_ref, nmix_ref, win_ref, lbraw_ref, onorm_ref, convw_ref, convb_ref,
                       wax_ref, ba_ref, bx_ref, lam_ref, waup_ref, wbup_ref, wout_ref,
                       xo_ref, s_ref, hl_ref, cv_ref,
                       hb_s, q_s, g_s, k_s, v_s, oa_s, xr_s, a_s, b_s):
    rows = MIX_ROWS
    t = pl.program_id(1)

    @pl.when(t == 0)
    def _():
        s_ref[...] = jnp.zeros_like(s_ref)
        hl_ref[...] = jnp.zeros_like(hl_ref)
        xr_s[0:CONV_PAD, :] = jnp.zeros((CONV_PAD, D_MODEL), F32)
        a_s[0:SCAN_PAD, :] = jnp.ones((SCAN_PAD, D_MODEL), F32)
        b_s[0:SCAN_PAD, :] = jnp.zeros((SCAN_PAD, D_MODEL), F32)

    x = x_ref[0]
    hb_s[...] = _rms(x, nmix_ref[...]).astype(BF16)

    def proj(i):
        return _dot(hb_s[...], win_ref[:, i * D_MODEL:(i + 1) * D_MODEL])

    lb = _lower_bound(lbraw_ref[...], 0)
    q_s[...] = _silu(proj(0))
    f = lb + (1.0 - lb) * _sigmoid(proj(1))
    g_s[...] = jnp.log(f)
    k_s[...] = 1.0 - f
    v_s[...] = proj(2)

    ri = lax.broadcasted_iota(jnp.int32, (CHUNK, CHUNK), 0)
    ci = lax.broadcasted_iota(jnp.int32, (CHUNK, CHUNK), 1)
    causal = ri >= ci
    tril = jnp.where(causal, 1.0, 0.0).astype(BF16)
    ones_cv = jnp.ones((CHUNK, DV), BF16)

    def chunk_body(c, carry):
        r0 = pl.multiple_of(c * CHUNK, CHUNK)
        rs = pl.ds(r0, CHUNK)
        g1, g2, g3 = _split3(g_s[rs, :])
        b = _dot(tril, g1) + _dot(tril, g2) + _dot(tril, g3)
        bl = b[CHUNK - 1:CHUNK, :]
        kk = k_s[rs, :]
        qe = (q_s[rs, :] * jnp.exp(b)).astype(BF16)
        ke = (kk * jnp.exp(-b)).astype(BF16)
        kd = (kk * jnp.exp(bl - b)).astype(BF16)
        vb = v_s[rs, :].astype(BF16)
        for h in range(HEADS):
            hs = slice(h * DK, (h + 1) * DK)
            s_old = s_ref[0, h]
            inter = _dot(qe[:, hs], s_old.astype(BF16))
            att = jnp.where(causal, _dot_nt(qe[:, hs], ke[:, hs]), 0.0)
            o = inter + _dot(att.astype(BF16), vb[:, hs])
            oa_s[rs, hs] = o * lax.rsqrt(jnp.mean(o * o, axis=-1, keepdims=True) + EPS)
            lhs = jnp.concatenate([kd[:, hs], g1[:, hs], g2[:, hs], g3[:, hs]], axis=1)
            rhs = jnp.concatenate([vb[:, hs], ones_cv], axis=1)
            prod = _dot_tn(lhs, rhs)
            upd = prod[0:DK, 0:DV]
            bl_col = (prod[DK:2 * DK, DV:] + prod[2 * DK:3 * DK, DV:] + prod[3 * DK:4 * DK, DV:])
            s_ref[0, h] = jnp.exp(bl_col) * s_old + upd
        return carry

    lax.fori_loop(0, rows // CHUNK, chunk_body, 0)

    o_a = (oa_s[...] * onorm_ref[...] * _silu(proj(3))).astype(BF16)

    xr_s[CONV_PAD:CONV_PAD + rows, :] = proj(4)
    xc = convb_ref[...]
    for j in range(CONV_W):
        off = CONV_PAD - (CONV_W - 1) + j
        xc = xc + xr_s[off:off + rows, :] * convw_ref[j:j + 1, :]
    tail = xr_s[CONV_PAD + rows - (CONV_W - 1):CONV_PAD + rows, :]
    cv_ref[0] = tail
    xr_s[CONV_PAD - (CONV_W - 1):CONV_PAD, :] = tail

    xcb = xc.astype(BF16)
    h_prev = hl_ref[0]
    first_row = lax.broadcasted_iota(jnp.int32, (rows, LRU_BW), 0) == 0
    for n in range(LRU_BLOCKS):
        ns = slice(n * LRU_BW, (n + 1) * LRU_BW)
        pre = _dot(xcb[:, ns], wax_ref[n])
        a, bterm = _lru_coeffs(pre[:, :LRU_BW] + ba_ref[:, ns], pre[:, LRU_BW:] + bx_ref[:, ns],
                               xc[:, ns], lam_ref[:, ns])
        bterm = jnp.where(first_row, bterm + a * h_prev[:, ns], bterm)
        a_s[SCAN_PAD:SCAN_PAD + rows, ns] = a
        b_s[SCAN_PAD:SCAN_PAD + rows, ns] = bterm

    shift = 1
    while shift < rows:
        a_cur = a_s[SCAN_PAD:SCAN_PAD + rows, :]
        b_cur = b_s[SCAN_PAD:SCAN_PAD + rows, :]
        a_sh = a_s[SCAN_PAD - shift:SCAN_PAD - shift + rows, :]
        b_sh = b_s[SCAN_PAD - shift:SCAN_PAD - shift + rows, :]
        b_s[SCAN_PAD:SCAN_PAD + rows, :] = a_cur * b_sh + b_cur
        a_s[SCAN_PAD:SCAN_PAD + rows, :] = a_cur * a_sh
        shift *= 2
    hseq = b_s[SCAN_PAD:SCAN_PAD + rows, :]
    hl_ref[0] = hseq[rows - 1:rows, :]
    o_b = (hseq * _gelu_tanh(proj(5))).astype(BF16)

    m = (_sigmoid(proj(6)) * _dot(o_a, waup_ref[...])
         + _sigmoid(proj(7)) * _dot(o_b, wbup_ref[...]))
    xo_ref[0] = x + _dot(m.astype(BF16), wout_ref[...])


def _mix_prompt(x, p):
    batch, seq, _ = x.shape
    rows = MIX_ROWS
    assert seq % rows == 0 and rows % CHUNK == 0 and rows >= CONV_W - 1
    row_vec = _resident((1, D_MODEL))
    out_shape = (
        jax.ShapeDtypeStruct((batch, seq, D_MODEL), F32),
        jax.ShapeDtypeStruct((batch, HEADS, DK, DV), F32),
        jax.ShapeDtypeStruct((batch, 1, D_MODEL), F32),
        jax.ShapeDtypeStruct((batch, CONV_W - 1, D_MODEL), F32),
    )
    return pl.pallas_call(
        _mix_prompt_kernel,
        grid=(batch, seq // rows),
        in_specs=[
            pl.BlockSpec((1, rows, D_MODEL), lambda b, t: (b, t, 0)),
            row_vec,
            _resident((D_MODEL, N_SPLITS * D_MODEL)),
            _resident(p["lb_raw"].shape),
            row_vec,
            _resident((CONV_W, D_MODEL)),
            row_vec,
            _resident((LRU_BLOCKS, LRU_BW, 2 * LRU_BW)),
            row_vec, row_vec, row_vec,
            _resident((D_MODEL, D_MODEL)),
            _resident((D_MODEL, D_MODEL)),
            _resident((D_MODEL, D_MODEL)),
        ],
        out_specs=(
            pl.BlockSpec((1, rows, D_MODEL), lambda b, t: (b, t, 0)),
            pl.BlockSpec((1, HEADS, DK, DV), lambda b, t: (b, 0, 0, 0)),
            pl.BlockSpec((1, 1, D_MODEL), lambda b, t: (b, 0, 0)),
            pl.BlockSpec((1, CONV_W - 1, D_MODEL), lambda b, t: (b, 0, 0)),
        ),
        out_shape=out_shape,
        scratch_shapes=[
            pltpu.VMEM((rows, D_MODEL), BF16),
            pltpu.VMEM((rows, D_MODEL), F32),
            pltpu.VMEM((rows, D_MODEL), F32),
            pltpu.VMEM((rows, D_MODEL), F32),
            pltpu.VMEM((rows, D_MODEL), F32),
            pltpu.VMEM((rows, D_MODEL), F32),
            pltpu.VMEM((CONV_PAD + rows, D_MODEL), F32),
            pltpu.VMEM((SCAN_PAD + rows, D_MODEL), F32),
            pltpu.VMEM((SCAN_PAD + rows, D_MODEL), F32),
        ],
        name="mix_prompt",
        compiler_params=pltpu.CompilerParams(
            dimension_semantics=("arbitrary", "arbitrary"), vmem_limit_bytes=VMEM_LIMIT_BYTES),
    )(x, p["mix_norm"], p["w_in"], p["lb_raw"], p["o_norm"], p["conv_w"], p["conv_b"],
      p["w_ax"], p["b_a"], p["b_x"], p["lam"], p["w_a_up"], p["w_b_up"], p["w_out"])


def _mix_sample_kernel(x_ref, nmix_ref, win_ref, lbraw_ref, onorm_ref, convw_ref, convb_ref,
                       wax_ref, ba_ref, bx_ref, lam_ref, waup_ref, wbup_ref, wout_ref,
                       sin_ref, hlin_ref, cvin_ref,
                       xo_ref, sout_ref, hl_ref, cv_ref,
                       qt_s, ft_s, v_s, oa_s, gsil_s, ga_s, mb_s):
    nseq = x_ref.shape[0]
    blk = SAMPLE_STATE_BLOCK
    j = pl.program_id(0)

    @pl.when(j == 0)
    def _():
        x = x_ref[...]
        hb = _rms(x, nmix_ref[...]).astype(BF16)

        def proj(i):
            return _dot(hb, win_ref[:, i * D_MODEL:(i + 1) * D_MODEL])

        lb = _lower_bound(lbraw_ref[...], 0)
        qt_s[...] = _silu(proj(0)).T
        ft_s[...] = (lb + (1.0 - lb) * _sigmoid(proj(1))).T
        v_s[...] = proj(2)
        gsil_s[...] = onorm_ref[...] * _silu(proj(3))

        xr = proj(4)
        xc = convb_ref[...] + xr * convw_ref[CONV_W - 1:CONV_W, :]
        for i in range(CONV_W - 1):
            xc = xc + cvin_ref[:, i, :] * convw_ref[i:i + 1, :]
        for i in range(CONV_W - 2):
            cv_ref[:, i, :] = cvin_ref[:, i + 1, :]
        cv_ref[:, CONV_W - 2, :] = xr

        xcb = xc.astype(BF16)
        h_parts = []
        for n in range(LRU_BLOCKS):
            ns = slice(n * LRU_BW, (n + 1) * LRU_BW)
            pre = _dot(xcb[:, ns], wax_ref[n])
            a, bterm = _lru_coeffs(pre[:, :LRU_BW] + ba_ref[:, ns], pre[:, LRU_BW:] + bx_ref[:, ns],
                                   xc[:, ns], lam_ref[:, ns])
            h_parts.append(a * hlin_ref[:, ns] + bterm)
        h_new = jnp.concatenate(h_parts, axis=1)
        hl_ref[...] = h_new
        o_b = (h_new * _gelu_tanh(proj(5))).astype(BF16)
        ga_s[...] = _sigmoid(proj(6))
        mb_s[...] = _sigmoid(proj(7)) * _dot(o_b, wbup_ref[...])

    b0 = pl.multiple_of(j * blk, blk)
    lane_shift = lax.rem(nseq - b0, nseq)
    for h in range(HEADS):
        hrows = slice(h * DK, (h + 1) * DK)
        q_cols = pltpu.roll(qt_s[hrows, :], lane_shift, 1)
        f_cols = pltpu.roll(ft_s[hrows, :], lane_shift, 1)
        k_cols = 1.0 - f_cols
        v_rows = v_s[pl.ds(b0, blk), hrows]
        o_rows = []
        for i in range(blk):
            f_col = jnp.broadcast_to(f_cols[:, i:i + 1], (DK, DV))
            k_col = jnp.broadcast_to(k_cols[:, i:i + 1], (DK, DV))
            q_col = jnp.broadcast_to(q_cols[:, i:i + 1], (DK, DV))
            s_new = f_col * sin_ref[i, h] + k_col * v_rows[i:i + 1, :]
            sout_ref[i, h] = s_new
            o_rows.append(jnp.sum(q_col * s_new, axis=0, keepdims=True))
        o = jnp.concatenate(o_rows, axis=0)
        oa_s[pl.ds(b0, blk), hrows] = o * lax.rsqrt(
            jnp.mean(o * o, axis=-1, keepdims=True) + EPS)

    @pl.when(j == pl.num_programs(0) - 1)
    def _():
        o_a = (oa_s[...] * gsil_s[...]).astype(BF16)
        m = ga_s[...] * _dot(o_a, waup_ref[...]) + mb_s[...]
        xo_ref[...] = x_ref[...] + _dot(m.astype(BF16), wout_ref[...])


def _mix_sample(x2d, s_in, h_in, c_in, p):
    nseq = x2d.shape[0]
    blk = SAMPLE_STATE_BLOCK
    assert nseq == LANES and nseq % blk == 0
    row_vec = _resident((1, D_MODEL))
    tok = _resident((nseq, D_MODEL))
    conv_state = _resident((nseq, CONV_W - 1, D_MODEL))
    out_shape = (
        jax.ShapeDtypeStruct((nseq, D_MODEL), F32),
        jax.ShapeDtypeStruct((nseq, HEADS, DK, DV), F32),
        jax.ShapeDtypeStruct((nseq, D_MODEL), F32),
        jax.ShapeDtypeStruct((nseq, CONV_W - 1, D_MODEL), F32),
    )
    state_spec = pl.BlockSpec((blk, HEADS, DK, DV), lambda j: (j, 0, 0, 0))
    return pl.pallas_call(
        _mix_sample_kernel,
        grid=(nseq // blk,),
        in_specs=[
            tok, row_vec,
            _resident((D_MODEL, N_SPLITS * D_MODEL)),
            _resident(p["lb_raw"].shape),
            row_vec,
            _resident((CONV_W, D_MODEL)),
            row_vec,
            _resident((LRU_BLOCKS, LRU_BW, 2 * LRU_BW)),
            row_vec, row_vec, row_vec,
            _resident((D_MODEL, D_MODEL)),
            _resident((D_MODEL, D_MODEL)),
            _resident((D_MODEL, D_MODEL)),
            state_spec, tok, conv_state,
        ],
        out_specs=(
            pl.BlockSpec((nseq, D_MODEL), lambda j: (0, 0)),
            state_spec,
            pl.BlockSpec((nseq, D_MODEL), lambda j: (0, 0)),
            pl.BlockSpec((nseq, CONV_W - 1, D_MODEL), lambda j: (0, 0, 0)),
        ),
        out_shape=out_shape,
        scratch_shapes=[
            pltpu.VMEM((D_MODEL, nseq), F32),
            pltpu.VMEM((D_MODEL, nseq), F32),
            pltpu.VMEM((nseq, D_MODEL), F32),
            pltpu.VMEM((nseq, D_MODEL), F32),
            pltpu.VMEM((nseq, D_MODEL), F32),
            pltpu.VMEM((nseq, D_MODEL), F32),
            pltpu.VMEM((nseq, D_MODEL), F32),
        ],
        name="mix_sample",
        compiler_params=pltpu.CompilerParams(
            dimension_semantics=("arbitrary",), vmem_limit_bytes=VMEM_LIMIT_BYTES),
    )(x2d, p["mix_norm"], p["w_in"], p["lb_raw"], p["o_norm"], p["conv_w"], p["conv_b"],
      p["w_ax"], p["b_a"], p["b_x"], p["lam"], p["w_a_up"], p["w_b_up"], p["w_out"],
      s_in, h_in, c_in)


def kernel(x_prompt, x_sample, state_hgrn, state_lru, state_conv, ffn1_norm, ffn1_w_gate, ffn1_w_up, ffn1_w_down, mix_norm, w_in, hgrn_lower_bounds, hgrn_out_norm, conv_w, conv_b, lru_w_a, lru_b_a, lru_w_x, lru_b_x, lru_lambda, w_a_up, w_b_up, w_out, ffn2_norm, ffn2_w_gate, ffn2_w_up, ffn2_w_down, final_norm):
    depth = w_in.shape[0]
    assert depth == 1, "single-layer trunk"
    l = 0
    row = lambda v: v.reshape(1, -1).astype(F32)
    bf = lambda w: w.astype(BF16)
    p = {
        "mix_norm": row(mix_norm[l]),
        "w_in": bf(w_in[l]),
        "lb_raw": hgrn_lower_bounds.astype(F32),
        "o_norm": row(hgrn_out_norm[l]),
        "conv_w": conv_w[l].astype(F32),
        "conv_b": row(conv_b[l]),
        "w_ax": bf(jnp.concatenate([lru_w_a[l], lru_w_x[l]], axis=-1)),
        "b_a": row(lru_b_a[l]),
        "b_x": row(lru_b_x[l]),
        "lam": row(lru_lambda[l]),
        "w_a_up": bf(w_a_up[l]),
        "w_b_up": bf(w_b_up[l]),
        "w_out": bf(w_out[l]),
    }
    f1 = (row(ffn1_norm[l]), bf(ffn1_w_gate[l]), bf(ffn1_w_up[l]), bf(ffn1_w_down[l]))
    f2 = (row(ffn2_norm[l]), bf(ffn2_w_gate[l]), bf(ffn2_w_up[l]), bf(ffn2_w_down[l]))
    gfin = row(final_norm)

    bp, tp, _ = x_prompt.shape
    xp = _ffn(x_prompt.reshape(bp * tp, D_MODEL), *f1, gfin, final_norm=False)
    xp, s_p, h_p, c_p = _mix_prompt(xp.reshape(bp, tp, D_MODEL), p)
    y_p = _ffn(xp.reshape(bp * tp, D_MODEL), *f2, gfin, final_norm=True).reshape(bp, tp, D_MODEL)

    bs, ts, _ = x_sample.shape
    assert ts == 1
    xs = _ffn(x_sample.reshape(bs, D_MODEL), *f1, gfin, final_norm=False)
    xs, s_s, h_s, c_s = _mix_sample(xs, state_hgrn[l], state_lru[l], state_conv[l], p)
    y_s = _ffn(xs, *f2, gfin, final_norm=True).reshape(bs, ts, D_MODEL)

    return (y_p, y_s, s_p[None], h_p.reshape(bp, D_MODEL)[None], c_p[None],
            s_s[None], h_s[None], c_s[None])
```

```python
import functools

import jax
import jax.numpy as jnp
from jax import lax
from jax.experimental import pallas as pl
from jax.experimental.pallas import tpu as pltpu

D_MODEL = 1024
HEADS = 8
DK = 128
DV = 128
CHUNK = 64
LRU_BLOCKS = 8
LRU_BW = 128
LRU_C = 8.0
CONV_W = 4
D_FF = 2816
EPS = 1e-6
N_SPLITS = 8

SUBLANES = 8
LANES = 128
VMEM_LIMIT_BYTES = 56 * 1024 * 1024

FFN_ROWS = 512
MIX_ROWS = 256
SAMPLE_STATE_BLOCK = 8

BF16 = jnp.bfloat16
F32 = jnp.float32


def _dot(a, b):
    return jnp.dot(a, b, preferred_element_type=F32)


def _dot_tn(a, b):
    return lax.dot_general(a, b, (((0,), (0,)), ((), ())), preferred_element_type=F32)


def _dot_nt(a, b):
    return lax.dot_general(a, b, (((1,), (1,)), ((), ())), preferred_element_type=F32)


def _sigmoid(x):
    return 1.0 / (1.0 + jnp.exp(-x))


def _silu(x):
    return x * _sigmoid(x)


def _gelu_tanh(x):
    c = 0.7978845608028654
    return 0.5 * x * (1.0 + jnp.tanh(c * (x + 0.044715 * (x * x * x))))


def _rms(x, g):
    return x * lax.rsqrt(jnp.mean(x * x, axis=-1, keepdims=True) + EPS) * g


def _lower_bound(raw, layer):
    m = jnp.max(raw, axis=0, keepdims=True)
    e = jnp.exp(raw - m)
    den = jnp.sum(e, axis=0, keepdims=True)
    num = jnp.sum(e[0:layer + 1], axis=0, keepdims=True)
    return num / den


def _softplus(x):
    return jnp.maximum(x, 0.0) + jnp.log1p(jnp.exp(-jnp.abs(x)))


def _lru_coeffs(r_pre, i_pre, xc, lam):
    r = _sigmoid(r_pre)
    ig = _sigmoid(i_pre)
    log_a = (-LRU_C) * r * _softplus(-lam)
    a = jnp.exp(log_a)
    th = jnp.tanh(log_a)
    mult = jnp.sqrt(-2.0 * th / (1.0 - th))
    return a, mult * (ig * xc)


def _split3(x):
    p1 = x.astype(BF16)
    r1 = x - p1.astype(F32)
    p2 = r1.astype(BF16)
    r2 = r1 - p2.astype(F32)
    return p1, p2, r2.astype(BF16)


def _ffn_kernel(x_ref, g_ref, wg_ref, wu_ref, wd_ref, gf_ref, o_ref, *, final_norm):
    x = x_ref[...]
    h = _rms(x, g_ref[...]).astype(BF16)
    gate = _dot(h, wg_ref[...])
    up = _dot(h, wu_ref[...])
    act = (_silu(gate) * up).astype(BF16)
    y = x + 0.5 * _dot(act, wd_ref[...])
    if final_norm:
        y = _rms(y, gf_ref[...])
    o_ref[...] = y


def _resident(shape):
    nd = len(shape)
    return pl.BlockSpec(shape, lambda *_: (0,) * nd, pipeline_mode=pl.Buffered(1))


def _ffn(x2d, g, wg, wu, wd, gf, *, final_norm):
    n = x2d.shape[0]
    rows = min(FFN_ROWS, n)
    assert n % rows == 0
    return pl.pallas_call(
        functools.partial(_ffn_kernel, final_norm=final_norm),
        grid=(n // rows,),
        in_specs=[
            pl.BlockSpec((rows, D_MODEL), lambda i: (i, 0)),
            _resident((1, D_MODEL)),
            _resident((D_MODEL, D_FF)),
            _resident((D_MODEL, D_FF)),
            _resident((D_FF, D_MODEL)),
            _resident((1, D_MODEL)),
        ],
        out_specs=pl.BlockSpec((rows, D_MODEL), lambda i: (i, 0)),
        out_shape=jax.ShapeDtypeStruct((n, D_MODEL), F32),
        name="ffn_final" if final_norm else "ffn",
        compiler_params=pltpu.CompilerParams(
            dimension_semantics=("arbitrary",), vmem_limit_bytes=VMEM_LIMIT_BYTES),
    )(x2d, g, wg, wu, wd, gf)


SCAN_PAD = MIX_ROWS // 2
CONV_PAD = SUBLANES
PIECE_ROWS = 2 * SUBLANES
QA_W = 2 * LANES


def _mix_prompt_kernel(x_ref, nmix_ref, win_ref, lbraw_ref, onorm_ref, convw_ref, convb_ref,
                       wax_ref, ba_ref, bx_ref, lam_ref, waup_ref, wbup_ref, wout_ref,
                       xo_ref, s_ref, hl_ref, cv_ref,
                       hb_s, qa_s, ud_s, vb_s, oa_s, xr_s, a_s, b_s):
    rows = MIX_ROWS
    t = pl.program_id(1)

    @pl.when(t == 0)
    def _():
        s_ref[...] = jnp.zeros_like(s_ref)
        hl_ref[...] = jnp.zeros_like(hl_ref)
        xr_s[0:CONV_PAD, :] = jnp.zeros((CONV_PAD, D_MODEL), F32)
        a_s[0:SCAN_PAD, :] = jnp.ones((SCAN_PAD, D_MODEL), F32)
        b_s[0:SCAN_PAD, :] = jnp.zeros((SCAN_PAD, D_MODEL), F32)

    x = x_ref[0]
    hb_s[...] = _rms(x, nmix_ref[...]).astype(BF16)

    def proj(i):
        return _dot(hb_s[...], win_ref[:, i * D_MODEL:(i + 1) * D_MODEL])

    lb = _lower_bound(lbraw_ref[...], 0)
    q_act = _silu(proj(0))
    f = lb + (1.0 - lb) * _sigmoid(proj(1))
    kk = 1.0 - f
    vb_s[...] = proj(2).astype(BF16)

    ri = lax.broadcasted_iota(jnp.int32, (rows, rows), 0)
    ci = lax.broadcasted_iota(jnp.int32, (rows, rows), 1)
    same_chunk = (ri // CHUNK) == (ci // CHUNK)
    tril_blk = jnp.where((ri >= ci) & same_chunk, 1.0, 0.0).astype(BF16)
    g1, g2, g3 = _split3(jnp.log(f))
    b_all = _dot(tril_blk, g1) + _dot(tril_blk, g2) + _dot(tril_blk, g3)

    causal = (lax.broadcasted_iota(jnp.int32, (CHUNK, CHUNK), 0)
              >= lax.broadcasted_iota(jnp.int32, (CHUNK, CHUNK), 1))
    piece_row = lax.broadcasted_iota(jnp.int32, (PIECE_ROWS, D_MODEL), 0)
    ones_pv = jnp.ones((3 * PIECE_ROWS, DV), BF16)
    zeros_pv = jnp.zeros((3 * PIECE_ROWS, DV), BF16)
    zeros_cv = jnp.zeros((CHUNK, DV), BF16)

    nchunk = rows // CHUNK
    last = piece_row == PIECE_ROWS - 1
    zero_p = jnp.zeros((PIECE_ROWS, D_MODEL), BF16)

    for c in range(nchunk):
        rs = slice(c * CHUNK, (c + 1) * CHUNK)
        b = b_all[rs, :]
        bl = b[CHUNK - 1:CHUNK, :]
        qe = (q_act[rs, :] * jnp.exp(b)).astype(BF16)
        ke = (kk[rs, :] * jnp.exp(-b)).astype(BF16)
        kd = (kk[rs, :] * jnp.exp(bl - b)).astype(BF16)
        pieces = jnp.concatenate(
            [jnp.where(last, e, zero_p) for e in _split3(jnp.exp(b[CHUNK - PIECE_ROWS:, :]))], axis=0)
        for h in range(HEADS):
            hs = slice(h * DK, (h + 1) * DK)
            qa_s[rs, h * QA_W:h * QA_W + DK] = qe[:, hs]
            att = jnp.where(causal, _dot_nt(qe[:, hs], ke[:, hs]), 0.0)
            qa_s[rs, h * QA_W + DK:h * QA_W + DK + CHUNK] = att.astype(BF16)
            lhs = jnp.concatenate([kd[:, hs], pieces[:, hs]], axis=0)
            rhs = jnp.concatenate([jnp.concatenate([vb_s[rs, hs], zeros_cv], axis=1),
                                   jnp.concatenate([zeros_pv, ones_pv], axis=1)], axis=0)
            ud_s[c * HEADS + h] = _dot_tn(lhs, rhs)

    for c in range(nchunk):
        rs = slice(c * CHUNK, (c + 1) * CHUNK)
        for h in range(HEADS):
            hs = slice(h * DK, (h + 1) * DK)
            s_old = s_ref[0, h]
            rhs = jnp.concatenate([s_old.astype(BF16), vb_s[rs, hs]], axis=0)
            o = _dot(qa_s[rs, h * QA_W:h * QA_W + DK + CHUNK], rhs)
            oa_s[rs, hs] = o * lax.rsqrt(jnp.mean(o * o, axis=-1, keepdims=True) + EPS)
            ud = ud_s[c * HEADS + h]
            s_ref[0, h] = ud[:, DV:] * s_old + ud[:, 0:DV]

    o_a = (oa_s[...] * onorm_ref[...] * _silu(proj(3))).astype(BF16)

    xr_s[CONV_PAD:CONV_PAD + rows, :] = proj(4)
    xc = convb_ref[...]
    for j in range(CONV_W):
        off = CONV_PAD - (CONV_W - 1) + j
        xc = xc + xr_s[off:off + rows, :] * convw_ref[j:j + 1, :]
    tail = xr_s[CONV_PAD + rows - (CONV_W - 1):CONV_PAD + rows, :]
    cv_ref[0] = tail
    xr_s[CONV_PAD - (CONV_W - 1):CONV_PAD, :] = tail

    xcb = xc.astype(BF16)
    h_prev = hl_ref[0]
    first_row = lax.broadcasted_iota(jnp.int32, (rows, LRU_BW), 0) == 0
    for n in range(LRU_BLOCKS):
        ns = slice(n * LRU_BW, (n + 1) * LRU_BW)
        pre = _dot(xcb[:, ns], wax_ref[n])
        a, bterm = _lru_coeffs(pre[:, :LRU_BW] + ba_ref[:, ns], pre[:, LRU_BW:] + bx_ref[:, ns],
                               xc[:, ns], lam_ref[:, ns])
        bterm = jnp.where(first_row, bterm + a * h_prev[:, ns], bterm)
        a_s[SCAN_PAD:SCAN_PAD + rows, ns] = a
        b_s[SCAN_PAD:SCAN_PAD + rows, ns] = bterm

    shift = 1
    while shift < rows:
        a_cur = a_s[SCAN_PAD:SCAN_PAD + rows, :]
        b_cur = b_s[SCAN_PAD:SCAN_PAD + rows, :]
        a_sh = a_s[SCAN_PAD - shift:SCAN_PAD - shift + rows, :]
        b_sh = b_s[SCAN_PAD - shift:SCAN_PAD - shift + rows, :]
        b_s[SCAN_PAD:SCAN_PAD + rows, :] = a_cur * b_sh + b_cur
        a_s[SCAN_PAD:SCAN_PAD + rows, :] = a_cur * a_sh
        shift *= 2
    hseq = b_s[SCAN_PAD:SCAN_PAD + rows, :]
    hl_ref[0] = hseq[rows - 1:rows, :]
    o_b = (hseq * _gelu_tanh(proj(5))).astype(BF16)

    m = (_sigmoid(proj(6)) * _dot(o_a, waup_ref[...])
         + _sigmoid(proj(7)) * _dot(o_b, wbup_ref[...]))
    xo_ref[0] = x + _dot(m.astype(BF16), wout_ref[...])


def _mix_prompt(x, p):
    batch, seq, _ = x.shape
    rows = MIX_ROWS
    assert seq % rows == 0 and rows % CHUNK == 0 and rows >= CONV_W - 1
    row_vec = _resident((1, D_MODEL))
    out_shape = (
        jax.ShapeDtypeStruct((batch, seq, D_MODEL), F32),
        jax.ShapeDtypeStruct((batch, HEADS, DK, DV), F32),
        jax.ShapeDtypeStruct((batch, 1, D_MODEL), F32),
        jax.ShapeDtypeStruct((batch, CONV_W - 1, D_MODEL), F32),
    )
    return pl.pallas_call(
        _mix_prompt_kernel,
        grid=(batch, seq // rows),
        in_specs=[
            pl.BlockSpec((1, rows, D_MODEL), lambda b, t: (b, t, 0)),
            row_vec,
            _resident((D_MODEL, N_SPLITS * D_MODEL)),
            _resident(p["lb_raw"].shape),
            row_vec,
            _resident((CONV_W, D_MODEL)),
            row_vec,
            _resident((LRU_BLOCKS, LRU_BW, 2 * LRU_BW)),
            row_vec, row_vec, row_vec,
            _resident((D_MODEL, D_MODEL)),
            _resident((D_MODEL, D_MODEL)),
            _resident((D_MODEL, D_MODEL)),
        ],
        out_specs=(
            pl.BlockSpec((1, rows, D_MODEL), lambda b, t: (b, t, 0)),
            pl.BlockSpec((1, HEADS, DK, DV), lambda b, t: (b, 0, 0, 0)),
            pl.BlockSpec((1, 1, D_MODEL), lambda b, t: (b, 0, 0)),
            pl.BlockSpec((1, CONV_W - 1, D_MODEL), lambda b, t: (b, 0, 0)),
        ),
        out_shape=out_shape,
        scratch_shapes=[
            pltpu.VMEM((rows, D_MODEL), BF16),
            pltpu.VMEM((rows, HEADS * QA_W), BF16),
            pltpu.VMEM((rows // CHUNK * HEADS, DK, 2 * DV), F32),
            pltpu.VMEM((rows, D_MODEL), BF16),
            pltpu.VMEM((rows, D_MODEL), F32),
            pltpu.VMEM((CONV_PAD + rows, D_MODEL), F32),
            pltpu.VMEM((SCAN_PAD + rows, D_MODEL), F32),
            pltpu.VMEM((SCAN_PAD + rows, D_MODEL), F32),
        ],
        name="mix_prompt",
        compiler_params=pltpu.CompilerParams(
            dimension_semantics=("arbitrary", "arbitrary"), vmem_limit_bytes=VMEM_LIMIT_BYTES),
    )(x, p["mix_norm"], p["w_in"], p["lb_raw"], p["o_norm"], p["conv_w"], p["conv_b"],
      p["w_ax"], p["b_a"], p["b_x"], p["lam"], p["w_a_up"], p["w_b_up"], p["w_out"])


def _mix_sample_kernel(x_ref, nmix_ref, win_ref, lbraw_ref, onorm_ref, convw_ref, convb_ref,
                       wax_ref, ba_ref, bx_ref, lam_ref, waup_ref, wbup_ref, wout_ref,
                       sin_ref, hlin_ref, cvin_ref,
                       xo_ref, sout_ref, hl_ref, cv_ref,
                       qt_s, ft_s, v_s, oa_s, gsil_s, ga_s, mb_s):
    nseq = x_ref.shape[0]
    blk = SAMPLE_STATE_BLOCK
    j = pl.program_id(0)

    @pl.when(j == 0)
    def _():
        x = x_ref[...]
        hb = _rms(x, nmix_ref[...]).astype(BF16)

        def proj(i):
            return _dot(hb, win_ref[:, i * D_MODEL:(i + 1) * D_MODEL])

        lb = _lower_bound(lbraw_ref[...], 0)
        qt_s[...] = _silu(proj(0)).T
        ft_s[...] = (lb + (1.0 - lb) * _sigmoid(proj(1))).T
        v_s[...] = proj(2)
        gsil_s[...] = onorm_ref[...] * _silu(proj(3))

        xr = proj(4)
        xc = convb_ref[...] + xr * convw_ref[CONV_W - 1:CONV_W, :]
        for i in range(CONV_W - 1):
            xc = xc + cvin_ref[:, i, :] * convw_ref[i:i + 1, :]
        for i in range(CONV_W - 2):
            cv_ref[:, i, :] = cvin_ref[:, i + 1, :]
        cv_ref[:, CONV_W - 2, :] = xr

        xcb = xc.astype(BF16)
        h_parts = []
        for n in range(LRU_BLOCKS):
            ns = slice(n * LRU_BW, (n + 1) * LRU_BW)
            pre = _dot(xcb[:, ns], wax_ref[n])
            a, bterm = _lru_coeffs(pre[:, :LRU_BW] + ba_ref[:, ns], pre[:, LRU_BW:] + bx_ref[:, ns],
                                   xc[:, ns], lam_ref[:, ns])
            h_parts.append(a * hlin_ref[:, ns] + bterm)
        h_new = jnp.concatenate(h_parts, axis=1)
        hl_ref[...] = h_new
        o_b = (h_new * _gelu_tanh(proj(5))).astype(BF16)
        ga_s[...] = _sigmoid(proj(6))
        mb_s[...] = _sigmoid(proj(7)) * _dot(o_b, wbup_ref[...])

    b0 = pl.multiple_of(j * blk, blk)
    lane_shift = lax.rem(nseq - b0, nseq)
    for h in range(HEADS):
        hrows = slice(h * DK, (h + 1) * DK)
        q_cols = pltpu.roll(qt_s[hrows, :], lane_shift, 1)
        f_cols = pltpu.roll(ft_s[hrows, :], lane_shift, 1)
        k_cols = 1.0 - f_cols
        v_rows = v_s[pl.ds(b0, blk), hrows]
        o_rows = []
        for i in range(blk):
            f_col = jnp.broadcast_to(f_cols[:, i:i + 1], (DK, DV))
            k_col = jnp.broadcast_to(k_cols[:, i:i + 1], (DK, DV))
            q_col = jnp.broadcast_to(q_cols[:, i:i + 1], (DK, DV))
            s_new = f_col * sin_ref[i, h] + k_col * v_rows[i:i + 1, :]
            sout_ref[i, h] = s_new
            o_rows.append(jnp.sum(q_col * s_new, axis=0, keepdims=True))
        o = jnp.concatenate(o_rows, axis=0)
        oa_s[pl.ds(b0, blk), hrows] = o * lax.rsqrt(
            jnp.mean(o * o, axis=-1, keepdims=True) + EPS)

    @pl.when(j == pl.num_programs(0) - 1)
    def _():
        o_a = (oa_s[...] * gsil_s[...]).astype(BF16)
        m = ga_s[...] * _dot(o_a, waup_ref[...]) + mb_s[...]
        xo_ref[...] = x_ref[...] + _dot(m.astype(BF16), wout_ref[...])


def _mix_sample(x2d, s_in, h_in, c_in, p):
    nseq = x2d.shape[0]
    blk = SAMPLE_STATE_BLOCK
    assert nseq == LANES and nseq % blk == 0
    row_vec = _resident((1, D_MODEL))
    tok = _resident((nseq, D_MODEL))
    conv_state = _resident((nseq, CONV_W - 1, D_MODEL))
    out_shape = (
        jax.ShapeDtypeStruct((nseq, D_MODEL), F32),
        jax.ShapeDtypeStruct((nseq, HEADS, DK, DV), F32),
        jax.ShapeDtypeStruct((nseq, D_MODEL), F32),
        jax.ShapeDtypeStruct((nseq, CONV_W - 1, D_MODEL), F32),
    )
    state_spec = pl.BlockSpec((blk, HEADS, DK, DV), lambda j: (j, 0, 0, 0))
    return pl.pallas_call(
        _mix_sample_kernel,
        grid=(nseq // blk,),
        in_specs=[
            tok, row_vec,
            _resident((D_MODEL, N_SPLITS * D_MODEL)),
            _resident(p["lb_raw"].shape),
            row_vec,
            _resident((CONV_W, D_MODEL)),
            row_vec,
            _resident((LRU_BLOCKS, LRU_BW, 2 * LRU_BW)),
            row_vec, row_vec, row_vec,
            _resident((D_MODEL, D_MODEL)),
            _resident((D_MODEL, D_MODEL)),
            _resident((D_MODEL, D_MODEL)),
            state_spec, tok, conv_state,
        ],
        out_specs=(
            pl.BlockSpec((nseq, D_MODEL), lambda j: (0, 0)),
            state_spec,
            pl.BlockSpec((nseq, D_MODEL), lambda j: (0, 0)),
            pl.BlockSpec((nseq, CONV_W - 1, D_MODEL), lambda j: (0, 0, 0)),
        ),
        out_shape=out_shape,
        scratch_shapes=[
            pltpu.VMEM((D_MODEL, nseq), F32),
            pltpu.VMEM((D_MODEL, nseq), F32),
            pltpu.VMEM((nseq, D_MODEL), F32),
            pltpu.VMEM((nseq, D_MODEL), F32),
            pltpu.VMEM((nseq, D_MODEL), F32),
            pltpu.VMEM((nseq, D_MODEL), F32),
            pltpu.VMEM((nseq, D_MODEL), F32),
        ],
        name="mix_sample",
        compiler_params=pltpu.CompilerParams(
            dimension_semantics=("arbitrary",), vmem_limit_bytes=VMEM_LIMIT_BYTES),
    )(x2d, p["mix_norm"], p["w_in"], p["lb_raw"], p["o_norm"], p["conv_w"], p["conv_b"],
      p["w_ax"], p["b_a"], p["b_x"], p["lam"], p["w_a_up"], p["w_b_up"], p["w_out"],
      s_in, h_in, c_in)


def kernel(x_prompt, x_sample, state_hgrn, state_lru, state_conv, ffn1_norm, ffn1_w_gate, ffn1_w_up, ffn1_w_down, mix_norm, w_in, hgrn_lower_bounds, hgrn_out_norm, conv_w, conv_b, lru_w_a, lru_b_a, lru_w_x, lru_b_x, lru_lambda, w_a_up, w_b_up, w_out, ffn2_norm, ffn2_w_gate, ffn2_w_up, ffn2_w_down, final_norm):
    depth = w_in.shape[0]
    assert depth == 1, "single-layer trunk"
    l = 0
    row = lambda v: v.reshape(1, -1).astype(F32)
    bf = lambda w: w.astype(BF16)
    p = {
        "mix_norm": row(mix_norm[l]),
        "w_in": bf(w_in[l]),
        "lb_raw": hgrn_lower_bounds.astype(F32),
        "o_norm": row(hgrn_out_norm[l]),
        "conv_w": conv_w[l].astype(F32),
        "conv_b": row(conv_b[l]),
        "w_ax": bf(jnp.concatenate([lru_w_a[l], lru_w_x[l]], axis=-1)),
        "b_a": row(lru_b_a[l]),
        "b_x": row(lru_b_x[l]),
        "lam": row(lru_lambda[l]),
        "w_a_up": bf(w_a_up[l]),
        "w_b_up": bf(w_b_up[l]),
        "w_out": bf(w_out[l]),
    }
    f1 = (row(ffn1_norm[l]), bf(ffn1_w_gate[l]), bf(ffn1_w_up[l]), bf(ffn1_w_down[l]))
    f2 = (row(ffn2_norm[l]), bf(ffn2_w_gate[l]), bf(ffn2_w_up[l]), bf(ffn2_w_down[l]))
    gfin = row(final_norm)

    bp, tp, _ = x_prompt.shape
    xp = _ffn(x_prompt.reshape(bp * tp, D_MODEL), *f1, gfin, final_norm=False)
    xp, s_p, h_p, c_p = _mix_prompt(xp.reshape(bp, tp, D_MODEL), p)
    y_p = _ffn(xp.reshape(bp * tp, D_MODEL), *f2, gfin, final_norm=True).reshape(bp, tp, D_MODEL)

    bs, ts, _ = x_sample.shape
    assert ts == 1
    xs = _ffn(x_sample.reshape(bs, D_MODEL), *f1, gfin, final_norm=False)
    xs, s_s, h_s, c_s = _mix_sample(xs, state_hgrn[l], state_lru[l], state_conv[l], p)
    y_s = _ffn(xs, *f2, gfin, final_norm=True).reshape(bs, ts, D_MODEL)

    return (y_p, y_s, s_p[None], h_p.reshape(bp, D_MODEL)[None], c_p[None],
            s_s[None], h_s[None], c_s[None])
```

```python
import functools

import jax
import jax.numpy as jnp
from jax import lax
from jax.experimental import pallas as pl
from jax.experimental.pallas import tpu as pltpu

D_MODEL = 1024
HEADS = 8
DK = 128
DV = 128
CHUNK = 64
LRU_BLOCKS = 8
LRU_BW = 128
LRU_C = 8.0
CONV_W = 4
D_FF = 2816
EPS = 1e-6
N_SPLITS = 8

SUBLANES = 8
LANES = 128
MXU_COLS = 256
VMEM_LIMIT_BYTES = 56 * 1024 * 1024

FFN_ROWS = 512
MIX_ROWS = 256
SAMPLE_STATE_BLOCK = 8

BF16 = jnp.bfloat16
F32 = jnp.float32


def _dot(a, b):
    return jnp.dot(a, b, preferred_element_type=F32)


def _dot_tn(a, b):
    return lax.dot_general(a, b, (((0,), (0,)), ((), ())), preferred_element_type=F32)


def _dot_nt(a, b):
    return lax.dot_general(a, b, (((1,), (1,)), ((), ())), preferred_element_type=F32)


def _sigmoid(x):
    return 1.0 / (1.0 + jnp.exp(-x))


def _silu(x):
    return x * _sigmoid(x)


def _gelu_tanh(x):
    c = 0.7978845608028654
    return 0.5 * x * (1.0 + jnp.tanh(c * (x + 0.044715 * (x * x * x))))


def _rms(x, g):
    return x * lax.rsqrt(jnp.mean(x * x, axis=-1, keepdims=True) + EPS) * g


def _lower_bound(raw, layer):
    m = jnp.max(raw, axis=0, keepdims=True)
    e = jnp.exp(raw - m)
    den = jnp.sum(e, axis=0, keepdims=True)
    num = jnp.sum(e[0:layer + 1], axis=0, keepdims=True)
    return num / den


def _softplus(x):
    return jnp.maximum(x, 0.0) + jnp.log1p(jnp.exp(-jnp.abs(x)))


def _lru_coeffs(r_pre, i_pre, xc, lam):
    r = _sigmoid(r_pre)
    ig = _sigmoid(i_pre)
    log_a = (-LRU_C) * r * _softplus(-lam)
    a = jnp.exp(log_a)
    th = jnp.tanh(log_a)
    mult = jnp.sqrt(-2.0 * th / (1.0 - th))
    return a, mult * (ig * xc)


def _split3(x):
    p1 = x.astype(BF16)
    r1 = x - p1.astype(F32)
    p2 = r1.astype(BF16)
    r2 = r1 - p2.astype(F32)
    return p1, p2, r2.astype(BF16)


def _ffn_kernel(x_ref, g_ref, wg_ref, wu_ref, wd_ref, gf_ref, o_ref, *, final_norm):
    x = x_ref[...]
    h = _rms(x, g_ref[...]).astype(BF16)
    gate = _dot(h, wg_ref[...])
    up = _dot(h, wu_ref[...])
    act = (_silu(gate) * up).astype(BF16)
    y = x + 0.5 * _dot(act, wd_ref[...])
    if final_norm:
        y = _rms(y, gf_ref[...])
    o_ref[...] = y


def _resident(shape):
    nd = len(shape)
    return pl.BlockSpec(shape, lambda *_: (0,) * nd, pipeline_mode=pl.Buffered(1))


def _ffn(x2d, g, wg, wu, wd, gf, *, final_norm):
    n = x2d.shape[0]
    rows = min(FFN_ROWS, n)
    assert n % rows == 0
    return pl.pallas_call(
        functools.partial(_ffn_kernel, final_norm=final_norm),
        grid=(n // rows,),
        in_specs=[
            pl.BlockSpec((rows, D_MODEL), lambda i: (i, 0)),
            _resident((1, D_MODEL)),
            _resident((D_MODEL, D_FF)),
            _resident((D_MODEL, D_FF)),
            _resident((D_FF, D_MODEL)),
            _resident((1, D_MODEL)),
        ],
        out_specs=pl.BlockSpec((rows, D_MODEL), lambda i: (i, 0)),
        out_shape=jax.ShapeDtypeStruct((n, D_MODEL), F32),
        name="ffn_final" if final_norm else "ffn",
        compiler_params=pltpu.CompilerParams(
            dimension_semantics=("arbitrary",), vmem_limit_bytes=VMEM_LIMIT_BYTES),
    )(x2d, g, wg, wu, wd, gf)


SCAN_SEG = 4
SCAN_GROUP = SCAN_SEG * SUBLANES
CONV_PAD = SUBLANES
PIECE_ROWS = 2 * SUBLANES
QA_W = 2 * LANES


def _mix_prompt_kernel(x_ref, nmix_ref, win_ref, lbraw_ref, onorm_ref, convw_ref, convb_ref,
                       wax_ref, ba_ref, bx_ref, lam_ref, waup_ref, wbup_ref, wout_ref,
                       xo_ref, s_ref, hl_ref, cv_ref,
                       qa_s, ud_s, vb_s, oa_s, xr_s, a_s, b_s, h_s):
    rows = MIX_ROWS
    t = pl.program_id(1)

    @pl.when(t == 0)
    def _():
        s_ref[...] = jnp.zeros_like(s_ref)
        hl_ref[...] = jnp.zeros_like(hl_ref)
        xr_s[0:CONV_PAD, :] = jnp.zeros((CONV_PAD, D_MODEL), F32)

    x = x_ref[0]
    hb = _rms(x, nmix_ref[...]).astype(BF16)

    def proj(i):
        return _dot(hb, win_ref[:, i * D_MODEL:(i + 1) * D_MODEL])

    early = {}

    ncb = D_MODEL // MXU_COLS

    def issue_proj(i, cb):
        if cb < ncb and (i, cb) not in early:
            c0 = i * D_MODEL + cb * MXU_COLS
            early[(i, cb)] = _dot(hb, win_ref[:, c0:c0 + MXU_COLS])

    def gather(i):
        for cb in range(ncb):
            issue_proj(i, cb)
        return jnp.concatenate([early[(i, cb)] for cb in range(ncb)], axis=1)

    lb = _lower_bound(lbraw_ref[...], 0)
    q_act = _silu(proj(0))
    f = lb + (1.0 - lb) * _sigmoid(proj(1))
    kk = 1.0 - f
    vb_s[...] = proj(2).astype(BF16)

    ri = lax.broadcasted_iota(jnp.int32, (rows, rows), 0)
    ci = lax.broadcasted_iota(jnp.int32, (rows, rows), 1)
    same_chunk = (ri // CHUNK) == (ci // CHUNK)
    tril_blk = jnp.where((ri >= ci) & same_chunk, 1.0, 0.0).astype(BF16)
    g1, g2, g3 = _split3(jnp.log(f))
    b_all = _dot(tril_blk, g1) + _dot(tril_blk, g2) + _dot(tril_blk, g3)

    causal = (lax.broadcasted_iota(jnp.int32, (CHUNK, CHUNK), 0)
              >= lax.broadcasted_iota(jnp.int32, (CHUNK, CHUNK), 1))
    piece_row = lax.broadcasted_iota(jnp.int32, (PIECE_ROWS, D_MODEL), 0)
    ones_pv = jnp.ones((3 * PIECE_ROWS, DV), BF16)
    zeros_pv = jnp.zeros((3 * PIECE_ROWS, DV), BF16)
    zeros_cv = jnp.zeros((CHUNK, DV), BF16)

    nchunk = rows // CHUNK
    last = piece_row == PIECE_ROWS - 1
    zero_p = jnp.zeros((PIECE_ROWS, D_MODEL), BF16)

    for c in range(nchunk):
        issue_proj(3, c)
        rs = slice(c * CHUNK, (c + 1) * CHUNK)
        b = b_all[rs, :]
        bl = b[CHUNK - 1:CHUNK, :]
        qe = (q_act[rs, :] * jnp.exp(b)).astype(BF16)
        ke = (kk[rs, :] * jnp.exp(-b)).astype(BF16)
        kd = (kk[rs, :] * jnp.exp(bl - b)).astype(BF16)
        pieces = jnp.concatenate(
            [jnp.where(last, e, zero_p) for e in _split3(jnp.exp(b[CHUNK - PIECE_ROWS:, :]))], axis=0)
        for h in range(HEADS):
            hs = slice(h * DK, (h + 1) * DK)
            qa_s[rs, h * QA_W:h * QA_W + DK] = qe[:, hs]
            att = jnp.where(causal, _dot_nt(qe[:, hs], ke[:, hs]), 0.0)
            qa_s[rs, h * QA_W + DK:h * QA_W + DK + CHUNK] = att.astype(BF16)
            lhs = jnp.concatenate([kd[:, hs], pieces[:, hs]], axis=0)
            rhs = jnp.concatenate([jnp.concatenate([vb_s[rs, hs], zeros_cv], axis=1),
                                   jnp.concatenate([zeros_pv, ones_pv], axis=1)], axis=0)
            ud_s[c * HEADS + h] = _dot_tn(lhs, rhs)

    for c in range(nchunk):
        rs = slice(c * CHUNK, (c + 1) * CHUNK)
        for h in range(HEADS):
            hs = slice(h * DK, (h + 1) * DK)
            s_old = s_ref[0, h]
            rhs = jnp.concatenate([s_old.astype(BF16), vb_s[rs, hs]], axis=0)
            o = _dot(qa_s[rs, h * QA_W:h * QA_W + DK + CHUNK], rhs)
            oa_s[rs, hs] = o * lax.rsqrt(jnp.mean(o * o, axis=-1, keepdims=True) + EPS)
            ud = ud_s[c * HEADS + h]
            s_ref[0, h] = ud[:, DV:] * s_old + ud[:, 0:DV]

    o_a = (oa_s[...] * onorm_ref[...] * _silu(gather(3))).astype(BF16)
    up_a = {}

    def issue_up_a(cb):
        if cb < ncb and cb not in up_a:
            up_a[cb] = _dot(o_a, waup_ref[:, cb * MXU_COLS:(cb + 1) * MXU_COLS])

    xr_s[CONV_PAD:CONV_PAD + rows, :] = proj(4)
    xc = convb_ref[...]
    for j in range(CONV_W):
        off = CONV_PAD - (CONV_W - 1) + j
        xc = xc + xr_s[off:off + rows, :] * convw_ref[j:j + 1, :]
    tail = xr_s[CONV_PAD + rows - (CONV_W - 1):CONV_PAD + rows, :]
    cv_ref[0] = tail
    xr_s[CONV_PAD - (CONV_W - 1):CONV_PAD, :] = tail

    xcb = xc.astype(BF16)
    h_prev = hl_ref[0]
    for n in range(LRU_BLOCKS):
        ns = slice(n * LRU_BW, (n + 1) * LRU_BW)
        pre = _dot(xcb[:, ns], wax_ref[n])
        a, bterm = _lru_coeffs(pre[:, :LRU_BW] + ba_ref[:, ns], pre[:, LRU_BW:] + bx_ref[:, ns],
                               xc[:, ns], lam_ref[:, ns])
        a_s[n] = a
        b_s[n] = bterm
        issue_proj(5 + n % 2, n // 2)

    sub = lax.broadcasted_iota(jnp.int32, (SUBLANES, LANES), 0)
    h_last = []
    for n in range(LRU_BLOCKS):
        carry = h_prev[:, n * LRU_BW:(n + 1) * LRU_BW]
        for grp in range(rows // SCAN_GROUP):
            def strided(j, grp=grp):
                return pl.ds(grp * SCAN_GROUP + j, SUBLANES, stride=SCAN_SEG)
            pa = [a_s[n, strided(0), :]]
            hh = [b_s[n, strided(0), :]]
            for j in range(1, SCAN_SEG):
                aj = a_s[n, strided(j), :]
                hh.append(aj * hh[-1] + b_s[n, strided(j), :])
                pa.append(aj * pa[-1])
            pi, hi = pa[-1], hh[-1]
            for d in (1, 2, 4):
                keep = sub >= d
                hi = pi * jnp.where(keep, pltpu.roll(hi, d, 0), 0.0) + hi
                pi = pi * jnp.where(keep, pltpu.roll(pi, d, 0), 1.0)
            first = sub >= 1
            seg_in = (jnp.where(first, pltpu.roll(pi, 1, 0), 1.0) * carry
                      + jnp.where(first, pltpu.roll(hi, 1, 0), 0.0))
            carry = pi[SUBLANES - 1:, :] * carry + hi[SUBLANES - 1:, :]
            for j in range(SCAN_SEG):
                h_s[n, strided(j), :] = pa[j] * seg_in + hh[j]
        h_last.append(carry)
        if n % 2 == 0:
            issue_proj(7, n // 2)
        else:
            issue_up_a(n // 2)
    hl_ref[0] = jnp.concatenate(h_last, axis=1)
    gel = _gelu_tanh(gather(5))
    o_b = jnp.concatenate(
        [h_s[n] * gel[:, n * LRU_BW:(n + 1) * LRU_BW] for n in range(LRU_BLOCKS)], axis=1).astype(BF16)

    for cb in range(ncb):
        issue_up_a(cb)
    m = (_sigmoid(gather(6)) * jnp.concatenate([up_a[cb] for cb in range(ncb)], axis=1)
         + _sigmoid(gather(7)) * _dot(o_b, wbup_ref[...]))
    xo_ref[0] = x + _dot(m.astype(BF16), wout_ref[...])


def _mix_prompt(x, p):
    batch, seq, _ = x.shape
    rows = MIX_ROWS
    assert seq % rows == 0 and rows % CHUNK == 0 and rows >= CONV_W - 1
    row_vec = _resident((1, D_MODEL))
    out_shape = (
        jax.ShapeDtypeStruct((batch, seq, D_MODEL), F32),
        jax.ShapeDtypeStruct((batch, HEADS, DK, DV), F32),
        jax.ShapeDtypeStruct((batch, 1, D_MODEL), F32),
        jax.ShapeDtypeStruct((batch, CONV_W - 1, D_MODEL), F32),
    )
    return pl.pallas_call(
        _mix_prompt_kernel,
        grid=(batch, seq // rows),
        in_specs=[
            pl.BlockSpec((1, rows, D_MODEL), lambda b, t: (b, t, 0)),
            row_vec,
            _resident((D_MODEL, N_SPLITS * D_MODEL)),
            _resident(p["lb_raw"].shape),
            row_vec,
            _resident((CONV_W, D_MODEL)),
            row_vec,
            _resident((LRU_BLOCKS, LRU_BW, 2 * LRU_BW)),
            row_vec, row_vec, row_vec,
            _resident((D_MODEL, D_MODEL)),
            _resident((D_MODEL, D_MODEL)),
            _resident((D_MODEL, D_MODEL)),
        ],
        out_specs=(
            pl.BlockSpec((1, rows, D_MODEL), lambda b, t: (b, t, 0)),
            pl.BlockSpec((1, HEADS, DK, DV), lambda b, t: (b, 0, 0, 0)),
            pl.BlockSpec((1, 1, D_MODEL), lambda b, t: (b, 0, 0)),
            pl.BlockSpec((1, CONV_W - 1, D_MODEL), lambda b, t: (b, 0, 0)),
        ),
        out_shape=out_shape,
        scratch_shapes=[
            pltpu.VMEM((rows, HEADS * QA_W), BF16),
            pltpu.VMEM((rows // CHUNK * HEADS, DK, 2 * DV), F32),
            pltpu.VMEM((rows, D_MODEL), BF16),
            pltpu.VMEM((rows, D_MODEL), F32),
            pltpu.VMEM((CONV_PAD + rows, D_MODEL), F32),
            pltpu.VMEM((LRU_BLOCKS, rows, LRU_BW), F32),
            pltpu.VMEM((LRU_BLOCKS, rows, LRU_BW), F32),
            pltpu.VMEM((LRU_BLOCKS, rows, LRU_BW), F32),
        ],
        name="mix_prompt",
        compiler_params=pltpu.CompilerParams(
            dimension_semantics=("arbitrary", "arbitrary"), vmem_limit_bytes=VMEM_LIMIT_BYTES),
    )(x, p["mix_norm"], p["w_in"], p["lb_raw"], p["o_norm"], p["conv_w"], p["conv_b"],
      p["w_ax"], p["b_a"], p["b_x"], p["lam"], p["w_a_up"], p["w_b_up"], p["w_out"])


def _mix_sample_kernel(x_ref, nmix_ref, win_ref, lbraw_ref, onorm_ref, convw_ref, convb_ref,
                       wax_ref, ba_ref, bx_ref, lam_ref, waup_ref, wbup_ref, wout_ref,
                       sin_ref, hlin_ref, cvin_ref,
                       xo_ref, sout_ref, hl_ref, cv_ref,
                       qt_s, ft_s, v_s, oa_s, gsil_s, ga_s, mb_s):
    nseq = x_ref.shape[0]
    blk = SAMPLE_STATE_BLOCK
    j = pl.program_id(0)

    @pl.when(j == 0)
    def _():
        x = x_ref[...]
        hb = _rms(x, nmix_ref[...]).astype(BF16)

        def proj(i):
            return _dot(hb, win_ref[:, i * D_MODEL:(i + 1) * D_MODEL])

        lb = _lower_bound(lbraw_ref[...], 0)
        qt_s[...] = _silu(proj(0)).T
        ft_s[...] = (lb + (1.0 - lb) * _sigmoid(proj(1))).T
        v_s[...] = proj(2)
        gsil_s[...] = onorm_ref[...] * _silu(proj(3))

        xr = proj(4)
        xc = convb_ref[...] + xr * convw_ref[CONV_W - 1:CONV_W, :]
        for i in range(CONV_W - 1):
            xc = xc + cvin_ref[:, i, :] * convw_ref[i:i + 1, :]
        for i in range(CONV_W - 2):
            cv_ref[:, i, :] = cvin_ref[:, i + 1, :]
        cv_ref[:, CONV_W - 2, :] = xr

        xcb = xc.astype(BF16)
        h_parts = []
        for n in range(LRU_BLOCKS):
            ns = slice(n * LRU_BW, (n + 1) * LRU_BW)
            pre = _dot(xcb[:, ns], wax_ref[n])
            a, bterm = _lru_coeffs(pre[:, :LRU_BW] + ba_ref[:, ns], pre[:, LRU_BW:] + bx_ref[:, ns],
                                   xc[:, ns], lam_ref[:, ns])
            h_parts.append(a * hlin_ref[:, ns] + bterm)
        h_new = jnp.concatenate(h_parts, axis=1)
        hl_ref[...] = h_new
        o_b = (h_new * _gelu_tanh(proj(5))).astype(BF16)
        ga_s[...] = _sigmoid(proj(6))
        mb_s[...] = _sigmoid(proj(7)) * _dot(o_b, wbup_ref[...])

    b0 = pl.multiple_of(j * blk, blk)
    lane_shift = lax.rem(nseq - b0, nseq)
    for h in range(HEADS):
        hrows = slice(h * DK, (h + 1) * DK)
        q_cols = pltpu.roll(qt_s[hrows, :], lane_shift, 1)
        f_cols = pltpu.roll(ft_s[hrows, :], lane_shift, 1)
        k_cols = 1.0 - f_cols
        v_rows = v_s[pl.ds(b0, blk), hrows]
        o_rows = []
        for i in range(blk):
            f_col = jnp.broadcast_to(f_cols[:, i:i + 1], (DK, DV))
            k_col = jnp.broadcast_to(k_cols[:, i:i + 1], (DK, DV))
            q_col = jnp.broadcast_to(q_cols[:, i:i + 1], (DK, DV))
            s_new = f_col * sin_ref[i, h] + k_col * v_rows[i:i + 1, :]
            sout_ref[i, h] = s_new
            o_rows.append(jnp.sum(q_col * s_new, axis=0, keepdims=True))
        o = jnp.concatenate(o_rows, axis=0)
        oa_s[pl.ds(b0, blk), hrows] = o * lax.rsqrt(
            jnp.mean(o * o, axis=-1, keepdims=True) + EPS)

    @pl.when(j == pl.num_programs(0) - 1)
    def _():
        o_a = (oa_s[...] * gsil_s[...]).astype(BF16)
        m = ga_s[...] * _dot(o_a, waup_ref[...]) + mb_s[...]
        xo_ref[...] = x_ref[...] + _dot(m.astype(BF16), wout_ref[...])


def _mix_sample(x2d, s_in, h_in, c_in, p):
    nseq = x2d.shape[0]
    blk = SAMPLE_STATE_BLOCK
    assert nseq == LANES and nseq % blk == 0
    row_vec = _resident((1, D_MODEL))
    tok = _resident((nseq, D_MODEL))
    conv_state = _resident((nseq, CONV_W - 1, D_MODEL))
    out_shape = (
        jax.ShapeDtypeStruct((nseq, D_MODEL), F32),
        jax.ShapeDtypeStruct((nseq, HEADS, DK, DV), F32),
        jax.ShapeDtypeStruct((nseq, D_MODEL), F32),
        jax.ShapeDtypeStruct((nseq, CONV_W - 1, D_MODEL), F32),
    )
    state_spec = pl.BlockSpec((blk, HEADS, DK, DV), lambda j: (j, 0, 0, 0))
    return pl.pallas_call(
        _mix_sample_kernel,
        grid=(nseq // blk,),
        in_specs=[
            tok, row_vec,
            _resident((D_MODEL, N_SPLITS * D_MODEL)),
            _resident(p["lb_raw"].shape),
            row_vec,
            _resident((CONV_W, D_MODEL)),
            row_vec,
            _resident((LRU_BLOCKS, LRU_BW, 2 * LRU_BW)),
            row_vec, row_vec, row_vec,
            _resident((D_MODEL, D_MODEL)),
            _resident((D_MODEL, D_MODEL)),
            _resident((D_MODEL, D_MODEL)),
            state_spec, tok, conv_state,
        ],
        out_specs=(
            pl.BlockSpec((nseq, D_MODEL), lambda j: (0, 0)),
            state_spec,
            pl.BlockSpec((nseq, D_MODEL), lambda j: (0, 0)),
            pl.BlockSpec((nseq, CONV_W - 1, D_MODEL), lambda j: (0, 0, 0)),
        ),
        out_shape=out_shape,
        scratch_shapes=[
            pltpu.VMEM((D_MODEL, nseq), F32),
            pltpu.VMEM((D_MODEL, nseq), F32),
            pltpu.VMEM((nseq, D_MODEL), F32),
            pltpu.VMEM((nseq, D_MODEL), F32),
            pltpu.VMEM((nseq, D_MODEL), F32),
            pltpu.VMEM((nseq, D_MODEL), F32),
            pltpu.VMEM((nseq, D_MODEL), F32),
        ],
        name="mix_sample",
        compiler_params=pltpu.CompilerParams(
            dimension_semantics=("arbitrary",), vmem_limit_bytes=VMEM_LIMIT_BYTES),
    )(x2d, p["mix_norm"], p["w_in"], p["lb_raw"], p["o_norm"], p["conv_w"], p["conv_b"],
      p["w_ax"], p["b_a"], p["b_x"], p["lam"], p["w_a_up"], p["w_b_up"], p["w_out"],
      s_in, h_in, c_in)


def kernel(x_prompt, x_sample, state_hgrn, state_lru, state_conv, ffn1_norm, ffn1_w_gate, ffn1_w_up, ffn1_w_down, mix_norm, w_in, hgrn_lower_bounds, hgrn_out_norm, conv_w, conv_b, lru_w_a, lru_b_a, lru_w_x, lru_b_x, lru_lambda, w_a_up, w_b_up, w_out, ffn2_norm, ffn2_w_gate, ffn2_w_up, ffn2_w_down, final_norm):
    depth = w_in.shape[0]
    assert depth == 1, "single-layer trunk"
    l = 0
    row = lambda v: v.reshape(1, -1).astype(F32)
    bf = lambda w: w.astype(BF16)
    p = {
        "mix_norm": row(mix_norm[l]),
        "w_in": bf(w_in[l]),
        "lb_raw": hgrn_lower_bounds.astype(F32),
        "o_norm": row(hgrn_out_norm[l]),
        "conv_w": conv_w[l].astype(F32),
        "conv_b": row(conv_b[l]),
        "w_ax": bf(jnp.concatenate([lru_w_a[l], lru_w_x[l]], axis=-1)),
        "b_a": row(lru_b_a[l]),
        "b_x": row(lru_b_x[l]),
        "lam": row(lru_lambda[l]),
        "w_a_up": bf(w_a_up[l]),
        "w_b_up": bf(w_b_up[l]),
        "w_out": bf(w_out[l]),
    }
    f1 = (row(ffn1_norm[l]), bf(ffn1_w_gate[l]), bf(ffn1_w_up[l]), bf(ffn1_w_down[l]))
    f2 = (row(ffn2_norm[l]), bf(ffn2_w_gate[l]), bf(ffn2_w_up[l]), bf(ffn2_w_down[l]))
    gfin = row(final_norm)

    bp, tp, _ = x_prompt.shape
    xp = _ffn(x_prompt.reshape(bp * tp, D_MODEL), *f1, gfin, final_norm=False)
    xp, s_p, h_p, c_p = _mix_prompt(xp.reshape(bp, tp, D_MODEL), p)
    y_p = _ffn(xp.reshape(bp * tp, D_MODEL), *f2, gfin, final_norm=True).reshape(bp, tp, D_MODEL)

    bs, ts, _ = x_sample.shape
    assert ts == 1
    xs = _ffn(x_sample.reshape(bs, D_MODEL), *f1, gfin, final_norm=False)
    xs, s_s, h_s, c_s = _mix_sample(xs, state_hgrn[l], state_lru[l], state_conv[l], p)
    y_s = _ffn(xs, *f2, gfin, final_norm=True).reshape(bs, ts, D_MODEL)

    return (y_p, y_s, s_p[None], h_p.reshape(bp, D_MODEL)[None], c_p[None],
            s_s[None], h_s[None], c_s[None])
```

```python
import functools

import jax
import jax.numpy as jnp
from jax import lax
from jax.experimental import pallas as pl
from jax.experimental.pallas import tpu as pltpu

D_MODEL = 1024
HEADS = 8
DK = 128
DV = 128
CHUNK = 64
LRU_BLOCKS = 8
LRU_BW = 128
LRU_C = 8.0
CONV_W = 4
D_FF = 2816
EPS = 1e-6
LOG2_E = 1.4426950408889634
N_SPLITS = 8

SUBLANES = 8
LANES = 128
MXU_COLS = 256
VMEM_LIMIT_BYTES = 56 * 1024 * 1024

FFN_ROWS = 512
MIX_ROWS = 256
SAMPLE_STATE_BLOCK = 8

BF16 = jnp.bfloat16
F32 = jnp.float32


def _dot(a, b):
    return jnp.dot(a, b, preferred_element_type=F32)


def _dot_tn(a, b):
    return lax.dot_general(a, b, (((0,), (0,)), ((), ())), preferred_element_type=F32)


def _dot_nt(a, b):
    return lax.dot_general(a, b, (((1,), (1,)), ((), ())), preferred_element_type=F32)


def _pack_rows(w):
    u = lax.bitcast_convert_type(w.astype(BF16), jnp.uint16).astype(jnp.uint32)
    u = u.reshape(*w.shape[:-2], w.shape[-2] // 2, 2, w.shape[-1])
    return lax.bitcast_convert_type(u[..., 0, :] | (u[..., 1, :] << 16), jnp.int32)


def _unpack(w32):
    return pltpu.bitcast(w32, BF16)


def _sigmoid(x):
    return 1.0 / (1.0 + jnp.exp2(x * (-LOG2_E)))


def _silu(x):
    return x * _sigmoid(x)


def _gelu_tanh(x):
    c = 0.7978845608028654
    return 0.5 * x * (1.0 + jnp.tanh(c * (x + 0.044715 * (x * x * x))))


def _rms(x, g):
    return x * lax.rsqrt(jnp.mean(x * x, axis=-1, keepdims=True) + EPS) * g


def _lower_bound(raw, layer):
    m = jnp.max(raw, axis=0, keepdims=True)
    e = jnp.exp(raw - m)
    den = jnp.sum(e, axis=0, keepdims=True)
    num = jnp.sum(e[0:layer + 1], axis=0, keepdims=True)
    return num / den


def _softplus(x):
    return jnp.maximum(x, 0.0) + jnp.log1p(jnp.exp(-jnp.abs(x)))


def _lru_coeffs(r_pre, i_pre, xc, lam):
    r = _sigmoid(r_pre)
    ig = _sigmoid(i_pre)
    log_a = (-LRU_C) * r * _softplus(-lam)
    a = jnp.exp(log_a)
    th = jnp.tanh(log_a)
    mult = jnp.sqrt(-2.0 * th / (1.0 - th))
    return a, mult * (ig * xc)


def _split3(x):
    p1 = x.astype(BF16)
    r1 = x - p1.astype(F32)
    p2 = r1.astype(BF16)
    r2 = r1 - p2.astype(F32)
    return p1, p2, r2.astype(BF16)


def _ffn_kernel(x_ref, g_ref, wg_ref, wu_ref, wd_ref, gf_ref, o_ref, *, final_norm):
    x = x_ref[...]
    h = _rms(x, g_ref[...]).astype(BF16)
    gate = _dot(h, wg_ref[...])
    up = _dot(h, wu_ref[...])
    act = (_silu(gate) * up).astype(BF16)
    y = x + 0.5 * _dot(act, wd_ref[...])
    if final_norm:
        y = _rms(y, gf_ref[...])
    o_ref[...] = y


def _resident(shape):
    nd = len(shape)
    return pl.BlockSpec(shape, lambda *_: (0,) * nd, pipeline_mode=pl.Buffered(1))


def _ffn(x2d, g, wg, wu, wd, gf, *, final_norm):
    n = x2d.shape[0]
    rows = min(FFN_ROWS, n)
    assert n % rows == 0
    return pl.pallas_call(
        functools.partial(_ffn_kernel, final_norm=final_norm),
        grid=(n // rows,),
        in_specs=[
            pl.BlockSpec((rows, D_MODEL), lambda i: (i, 0)),
            _resident((1, D_MODEL)),
            _resident((D_MODEL, D_FF)),
            _resident((D_MODEL, D_FF)),
            _resident((D_FF, D_MODEL)),
            _resident((1, D_MODEL)),
        ],
        out_specs=pl.BlockSpec((rows, D_MODEL), lambda i: (i, 0)),
        out_shape=jax.ShapeDtypeStruct((n, D_MODEL), F32),
        name="ffn_final" if final_norm else "ffn",
        compiler_params=pltpu.CompilerParams(
            dimension_semantics=("arbitrary",), vmem_limit_bytes=VMEM_LIMIT_BYTES),
    )(x2d, g, wg, wu, wd, gf)


SCAN_SEG = 4
SCAN_GROUP = SCAN_SEG * SUBLANES
CONV_PAD = SUBLANES
PIECE_ROWS = 2 * SUBLANES
QA_W = 2 * LANES


def _mix_prompt_kernel(x_ref, nmix_ref, win_ref, lbraw_ref, onorm_ref, convw_ref, convb_ref,
                       wax_ref, ba_ref, bx_ref, lam_ref, waup_ref, wbup_ref, wout_ref,
                       xo_ref, s_ref, hl_ref, cv_ref,
                       qa_s, ud_s, vb_s, oa_s, xr_s, a_s, b_s, h_s):
    rows = MIX_ROWS
    t = pl.program_id(1)

    @pl.when(t == 0)
    def _():
        s_ref[...] = jnp.zeros_like(s_ref)
        hl_ref[...] = jnp.zeros_like(hl_ref)
        xr_s[0:CONV_PAD, :] = jnp.zeros((CONV_PAD, D_MODEL), F32)

    x = x_ref[0]
    hb = _rms(x, nmix_ref[...]).astype(BF16)

    def proj(i):
        return _dot(hb, _unpack(win_ref[:, i * D_MODEL:(i + 1) * D_MODEL]))

    early = {}

    ncb = D_MODEL // MXU_COLS

    def issue_proj(i, cb):
        if cb < ncb and (i, cb) not in early:
            c0 = i * D_MODEL + cb * MXU_COLS
            early[(i, cb)] = _dot(hb, _unpack(win_ref[:, c0:c0 + MXU_COLS]))

    def gather(i):
        for cb in range(ncb):
            issue_proj(i, cb)
        return jnp.concatenate([early[(i, cb)] for cb in range(ncb)], axis=1)

    lb = _lower_bound(lbraw_ref[...], 0)
    q_act = _silu(proj(0))
    f = lb + (1.0 - lb) * _sigmoid(proj(1))
    kk = 1.0 - f
    vb_s[...] = proj(2).astype(BF16)

    ri = lax.broadcasted_iota(jnp.int32, (rows, rows), 0)
    ci = lax.broadcasted_iota(jnp.int32, (rows, rows), 1)
    same_chunk = (ri // CHUNK) == (ci // CHUNK)
    tril_blk = jnp.where((ri >= ci) & same_chunk, 1.0, 0.0).astype(BF16)
    g1, g2, g3 = _split3(jnp.log(f))
    b_all = _dot(tril_blk, g1) + _dot(tril_blk, g2) + _dot(tril_blk, g3)

    causal = (lax.broadcasted_iota(jnp.int32, (CHUNK, CHUNK), 0)
              >= lax.broadcasted_iota(jnp.int32, (CHUNK, CHUNK), 1))
    piece_row = lax.broadcasted_iota(jnp.int32, (PIECE_ROWS, D_MODEL), 0)
    ones_pv = jnp.ones((3 * PIECE_ROWS, DV), BF16)
    zeros_pv = jnp.zeros((3 * PIECE_ROWS, DV), BF16)
    zeros_cv = jnp.zeros((CHUNK, DV), BF16)

    nchunk = rows // CHUNK
    last = piece_row == PIECE_ROWS - 1
    zero_p = jnp.zeros((PIECE_ROWS, D_MODEL), BF16)

    for c in range(nchunk):
        issue_proj(3, c)
        rs = slice(c * CHUNK, (c + 1) * CHUNK)
        b = b_all[rs, :]
        bl = b[CHUNK - 1:CHUNK, :]
        qe = (q_act[rs, :] * jnp.exp(b)).astype(BF16)
        ke = (kk[rs, :] * jnp.exp(-b)).astype(BF16)
        kd = (kk[rs, :] * jnp.exp(bl - b)).astype(BF16)
        pieces = jnp.concatenate(
            [jnp.where(last, e, zero_p) for e in _split3(jnp.exp(b[CHUNK - PIECE_ROWS:, :]))], axis=0)
        for h in range(HEADS):
            hs = slice(h * DK, (h + 1) * DK)
            qa_s[rs, h * QA_W:h * QA_W + DK] = qe[:, hs]
            att = jnp.where(causal, _dot_nt(qe[:, hs], ke[:, hs]), 0.0)
            qa_s[rs, h * QA_W + DK:h * QA_W + DK + CHUNK] = att.astype(BF16)
            lhs = jnp.concatenate([kd[:, hs], pieces[:, hs]], axis=0)
            rhs = jnp.concatenate([jnp.concatenate([vb_s[rs, hs], zeros_cv], axis=1),
                                   jnp.concatenate([zeros_pv, ones_pv], axis=1)], axis=0)
            ud_s[c * HEADS + h] = _dot_tn(lhs, rhs)

    for c in range(nchunk):
        rs = slice(c * CHUNK, (c + 1) * CHUNK)
        for h in range(HEADS):
            hs = slice(h * DK, (h + 1) * DK)
            s_old = s_ref[0, h]
            rhs = jnp.concatenate([s_old.astype(BF16), vb_s[rs, hs]], axis=0)
            o = _dot(qa_s[rs, h * QA_W:h * QA_W + DK + CHUNK], rhs)
            oa_s[rs, hs] = o * lax.rsqrt(jnp.mean(o * o, axis=-1, keepdims=True) + EPS)
            ud = ud_s[c * HEADS + h]
            s_ref[0, h] = ud[:, DV:] * s_old + ud[:, 0:DV]

    o_a = (oa_s[...] * onorm_ref[...] * _silu(gather(3))).astype(BF16)
    up_a = {}

    def issue_up_a(cb):
        if cb < ncb and cb not in up_a:
            up_a[cb] = _dot(o_a, _unpack(waup_ref[:, cb * MXU_COLS:(cb + 1) * MXU_COLS]))

    xr_s[CONV_PAD:CONV_PAD + rows, :] = proj(4)
    xc = convb_ref[...]
    for j in range(CONV_W):
        off = CONV_PAD - (CONV_W - 1) + j
        xc = xc + xr_s[off:off + rows, :] * convw_ref[j:j + 1, :]
    tail = xr_s[CONV_PAD + rows - (CONV_W - 1):CONV_PAD + rows, :]
    cv_ref[0] = tail
    xr_s[CONV_PAD - (CONV_W - 1):CONV_PAD, :] = tail

    xcb = xc.astype(BF16)
    h_prev = hl_ref[0]
    for n in range(LRU_BLOCKS):
        ns = slice(n * LRU_BW, (n + 1) * LRU_BW)
        pre = _dot(xcb[:, ns], _unpack(wax_ref[n]))
        a, bterm = _lru_coeffs(pre[:, :LRU_BW] + ba_ref[:, ns], pre[:, LRU_BW:] + bx_ref[:, ns],
                               xc[:, ns], lam_ref[:, ns])
        a_s[n] = a
        b_s[n] = bterm
        issue_proj(5 + n % 2, n // 2)

    sub = lax.broadcasted_iota(jnp.int32, (SUBLANES, LANES), 0)
    h_last = []
    for n in range(LRU_BLOCKS):
        carry = h_prev[:, n * LRU_BW:(n + 1) * LRU_BW]
        for grp in range(rows // SCAN_GROUP):
            def strided(j, grp=grp):
                return pl.ds(grp * SCAN_GROUP + j, SUBLANES, stride=SCAN_SEG)
            pa = [a_s[n, strided(0), :]]
            hh = [b_s[n, strided(0), :]]
            for j in range(1, SCAN_SEG):
                aj = a_s[n, strided(j), :]
                hh.append(aj * hh[-1] + b_s[n, strided(j), :])
                pa.append(aj * pa[-1])
            pi, hi = pa[-1], hh[-1]
            for d in (1, 2, 4):
                keep = sub >= d
                hi = pi * jnp.where(keep, pltpu.roll(hi, d, 0), 0.0) + hi
                pi = pi * jnp.where(keep, pltpu.roll(pi, d, 0), 1.0)
            first = sub >= 1
            seg_in = (jnp.where(first, pltpu.roll(pi, 1, 0), 1.0) * carry
                      + jnp.where(first, pltpu.roll(hi, 1, 0), 0.0))
            carry = pi[SUBLANES - 1:, :] * carry + hi[SUBLANES - 1:, :]
            for j in range(SCAN_SEG):
                h_s[n, strided(j), :] = pa[j] * seg_in + hh[j]
        h_last.append(carry)
        if n % 2 == 0:
            issue_proj(7, n // 2)
        else:
            issue_up_a(n // 2)
    hl_ref[0] = jnp.concatenate(h_last, axis=1)
    gel = _gelu_tanh(gather(5))
    o_b = jnp.concatenate(
        [h_s[n] * gel[:, n * LRU_BW:(n + 1) * LRU_BW] for n in range(LRU_BLOCKS)], axis=1).astype(BF16)

    for cb in range(ncb):
        issue_up_a(cb)
    m = (_sigmoid(gather(6)) * jnp.concatenate([up_a[cb] for cb in range(ncb)], axis=1)
         + _sigmoid(gather(7)) * _dot(o_b, _unpack(wbup_ref[...])))
    xo_ref[0] = x + _dot(m.astype(BF16), _unpack(wout_ref[...]))


def _mix_prompt(x, p):
    batch, seq, _ = x.shape
    rows = MIX_ROWS
    assert seq % rows == 0 and rows % CHUNK == 0 and rows >= CONV_W - 1
    row_vec = _resident((1, D_MODEL))
    out_shape = (
        jax.ShapeDtypeStruct((batch, seq, D_MODEL), F32),
        jax.ShapeDtypeStruct((batch, HEADS, DK, DV), F32),
        jax.ShapeDtypeStruct((batch, 1, D_MODEL), F32),
        jax.ShapeDtypeStruct((batch, CONV_W - 1, D_MODEL), F32),
    )
    return pl.pallas_call(
        _mix_prompt_kernel,
        grid=(batch, seq // rows),
        in_specs=[
            pl.BlockSpec((1, rows, D_MODEL), lambda b, t: (b, t, 0)),
            row_vec,
            _resident((D_MODEL // 2, N_SPLITS * D_MODEL)),
            _resident(p["lb_raw"].shape),
            row_vec,
            _resident((CONV_W, D_MODEL)),
            row_vec,
            _resident((LRU_BLOCKS, LRU_BW // 2, 2 * LRU_BW)),
            row_vec, row_vec, row_vec,
            _resident((D_MODEL // 2, D_MODEL)),
            _resident((D_MODEL // 2, D_MODEL)),
            _resident((D_MODEL // 2, D_MODEL)),
        ],
        out_specs=(
            pl.BlockSpec((1, rows, D_MODEL), lambda b, t: (b, t, 0)),
            pl.BlockSpec((1, HEADS, DK, DV), lambda b, t: (b, 0, 0, 0)),
            pl.BlockSpec((1, 1, D_MODEL), lambda b, t: (b, 0, 0)),
            pl.BlockSpec((1, CONV_W - 1, D_MODEL), lambda b, t: (b, 0, 0)),
        ),
        out_shape=out_shape,
        scratch_shapes=[
            pltpu.VMEM((rows, HEADS * QA_W), BF16),
            pltpu.VMEM((rows // CHUNK * HEADS, DK, 2 * DV), F32),
            pltpu.VMEM((rows, D_MODEL), BF16),
            pltpu.VMEM((rows, D_MODEL), F32),
            pltpu.VMEM((CONV_PAD + rows, D_MODEL), F32),
            pltpu.VMEM((LRU_BLOCKS, rows, LRU_BW), F32),
            pltpu.VMEM((LRU_BLOCKS, rows, LRU_BW), F32),
            pltpu.VMEM((LRU_BLOCKS, rows, LRU_BW), F32),
        ],
        name="mix_prompt",
        compiler_params=pltpu.CompilerParams(
            dimension_semantics=("arbitrary", "arbitrary"), vmem_limit_bytes=VMEM_LIMIT_BYTES),
    )(x, p["mix_norm"], p["w_in"], p["lb_raw"], p["o_norm"], p["conv_w"], p["conv_b"],
      p["w_ax"], p["b_a"], p["b_x"], p["lam"], p["w_a_up"], p["w_b_up"], p["w_out"])


def _mix_sample_kernel(x_ref, nmix_ref, win_ref, lbraw_ref, onorm_ref, convw_ref, convb_ref,
                       wax_ref, ba_ref, bx_ref, lam_ref, waup_ref, wbup_ref, wout_ref,
                       sin_ref, hlin_ref, cvin_ref,
                       xo_ref, sout_ref, hl_ref, cv_ref,
                       qt_s, ft_s, v_s, oa_s, gsil_s, ga_s, mb_s):
    nseq = x_ref.shape[0]
    blk = SAMPLE_STATE_BLOCK
    j = pl.program_id(0)

    @pl.when(j == 0)
    def _():
        x = x_ref[...]
        hb = _rms(x, nmix_ref[...]).astype(BF16)

        def proj(i):
            return _dot(hb, _unpack(win_ref[:, i * D_MODEL:(i + 1) * D_MODEL]))

        lb = _lower_bound(lbraw_ref[...], 0)
        qt_s[...] = _silu(proj(0)).T
        ft_s[...] = (lb + (1.0 - lb) * _sigmoid(proj(1))).T
        v_s[...] = proj(2)
        gsil_s[...] = onorm_ref[...] * _silu(proj(3))

        xr = proj(4)
        xc = convb_ref[...] + xr * convw_ref[CONV_W - 1:CONV_W, :]
        for i in range(CONV_W - 1):
            xc = xc + cvin_ref[:, i, :] * convw_ref[i:i + 1, :]
        for i in range(CONV_W - 2):
            cv_ref[:, i, :] = cvin_ref[:, i + 1, :]
        cv_ref[:, CONV_W - 2, :] = xr

        xcb = xc.astype(BF16)
        h_parts = []
        for n in range(LRU_BLOCKS):
            ns = slice(n * LRU_BW, (n + 1) * LRU_BW)
            pre = _dot(xcb[:, ns], _unpack(wax_ref[n]))
            a, bterm = _lru_coeffs(pre[:, :LRU_BW] + ba_ref[:, ns], pre[:, LRU_BW:] + bx_ref[:, ns],
                                   xc[:, ns], lam_ref[:, ns])
            h_parts.append(a * hlin_ref[:, ns] + bterm)
        h_new = jnp.concatenate(h_parts, axis=1)
        hl_ref[...] = h_new
        o_b = (h_new * _gelu_tanh(proj(5))).astype(BF16)
        ga_s[...] = _sigmoid(proj(6))
        mb_s[...] = _sigmoid(proj(7)) * _dot(o_b, _unpack(wbup_ref[...]))

    b0 = pl.multiple_of(j * blk, blk)
    lane_shift = lax.rem(nseq - b0, nseq)
    for h in range(HEADS):
        hrows = slice(h * DK, (h + 1) * DK)
        q_cols = pltpu.roll(qt_s[hrows, :], lane_shift, 1)
        f_cols = pltpu.roll(ft_s[hrows, :], lane_shift, 1)
        k_cols = 1.0 - f_cols
        v_rows = v_s[pl.ds(b0, blk), hrows]
        o_rows = []
        for i in range(blk):
            f_col = jnp.broadcast_to(f_cols[:, i:i + 1], (DK, DV))
            k_col = jnp.broadcast_to(k_cols[:, i:i + 1], (DK, DV))
            q_col = jnp.broadcast_to(q_cols[:, i:i + 1], (DK, DV))
            s_new = f_col * sin_ref[i, h] + k_col * v_rows[i:i + 1, :]
            sout_ref[i, h] = s_new
            o_rows.append(jnp.sum(q_col * s_new, axis=0, keepdims=True))
        o = jnp.concatenate(o_rows, axis=0)
        oa_s[pl.ds(b0, blk), hrows] = o * lax.rsqrt(
            jnp.mean(o * o, axis=-1, keepdims=True) + EPS)

    @pl.when(j == pl.num_programs(0) - 1)
    def _():
        o_a = (oa_s[...] * gsil_s[...]).astype(BF16)
        m = ga_s[...] * _dot(o_a, _unpack(waup_ref[...])) + mb_s[...]
        xo_ref[...] = x_ref[...] + _dot(m.astype(BF16), _unpack(wout_ref[...]))


def _mix_sample(x2d, s_in, h_in, c_in, p):
    nseq = x2d.shape[0]
    blk = SAMPLE_STATE_BLOCK
    assert nseq == LANES and nseq % blk == 0
    row_vec = _resident((1, D_MODEL))
    tok = _resident((nseq, D_MODEL))
    conv_state = _resident((nseq, CONV_W - 1, D_MODEL))
    out_shape = (
        jax.ShapeDtypeStruct((nseq, D_MODEL), F32),
        jax.ShapeDtypeStruct((nseq, HEADS, DK, DV), F32),
        jax.ShapeDtypeStruct((nseq, D_MODEL), F32),
        jax.ShapeDtypeStruct((nseq, CONV_W - 1, D_MODEL), F32),
    )
    state_spec = pl.BlockSpec((blk, HEADS, DK, DV), lambda j: (j, 0, 0, 0))
    return pl.pallas_call(
        _mix_sample_kernel,
        grid=(nseq // blk,),
        in_specs=[
            tok, row_vec,
            _resident((D_MODEL // 2, N_SPLITS * D_MODEL)),
            _resident(p["lb_raw"].shape),
            row_vec,
            _resident((CONV_W, D_MODEL)),
            row_vec,
            _resident((LRU_BLOCKS, LRU_BW // 2, 2 * LRU_BW)),
            row_vec, row_vec, row_vec,
            _resident((D_MODEL // 2, D_MODEL)),
            _resident((D_MODEL // 2, D_MODEL)),
            _resident((D_MODEL // 2, D_MODEL)),
            state_spec, tok, conv_state,
        ],
        out_specs=(
            pl.BlockSpec((nseq, D_MODEL), lambda j: (0, 0)),
            state_spec,
            pl.BlockSpec((nseq, D_MODEL), lambda j: (0, 0)),
            pl.BlockSpec((nseq, CONV_W - 1, D_MODEL), lambda j: (0, 0, 0)),
        ),
        out_shape=out_shape,
        scratch_shapes=[
            pltpu.VMEM((D_MODEL, nseq), F32),
            pltpu.VMEM((D_MODEL, nseq), F32),
            pltpu.VMEM((nseq, D_MODEL), F32),
            pltpu.VMEM((nseq, D_MODEL), F32),
            pltpu.VMEM((nseq, D_MODEL), F32),
            pltpu.VMEM((nseq, D_MODEL), F32),
            pltpu.VMEM((nseq, D_MODEL), F32),
        ],
        name="mix_sample",
        compiler_params=pltpu.CompilerParams(
            dimension_semantics=("arbitrary",), vmem_limit_bytes=VMEM_LIMIT_BYTES),
    )(x2d, p["mix_norm"], p["w_in"], p["lb_raw"], p["o_norm"], p["conv_w"], p["conv_b"],
      p["w_ax"], p["b_a"], p["b_x"], p["lam"], p["w_a_up"], p["w_b_up"], p["w_out"],
      s_in, h_in, c_in)


def kernel(x_prompt, x_sample, state_hgrn, state_lru, state_conv, ffn1_norm, ffn1_w_gate, ffn1_w_up, ffn1_w_down, mix_norm, w_in, hgrn_lower_bounds, hgrn_out_norm, conv_w, conv_b, lru_w_a, lru_b_a, lru_w_x, lru_b_x, lru_lambda, w_a_up, w_b_up, w_out, ffn2_norm, ffn2_w_gate, ffn2_w_up, ffn2_w_down, final_norm):
    depth = w_in.shape[0]
    assert depth == 1, "single-layer trunk"
    l = 0
    row = lambda v: v.reshape(1, -1).astype(F32)
    bf = lambda w: w.astype(BF16)
    p = {
        "mix_norm": row(mix_norm[l]),
        "w_in": _pack_rows(w_in[l]),
        "lb_raw": hgrn_lower_bounds.astype(F32),
        "o_norm": row(hgrn_out_norm[l]),
        "conv_w": conv_w[l].astype(F32),
        "conv_b": row(conv_b[l]),
        "w_ax": _pack_rows(jnp.concatenate([lru_w_a[l], lru_w_x[l]], axis=-1)),
        "b_a": row(lru_b_a[l]),
        "b_x": row(lru_b_x[l]),
        "lam": row(lru_lambda[l]),
        "w_a_up": _pack_rows(w_a_up[l]),
        "w_b_up": _pack_rows(w_b_up[l]),
        "w_out": _pack_rows(w_out[l]),
    }
    f1 = (row(ffn1_norm[l]), bf(ffn1_w_gate[l]), bf(ffn1_w_up[l]), bf(ffn1_w_down[l]))
    f2 = (row(ffn2_norm[l]), bf(ffn2_w_gate[l]), bf(ffn2_w_up[l]), bf(ffn2_w_down[l]))
    gfin = row(final_norm)

    bp, tp, _ = x_prompt.shape
    xp = _ffn(x_prompt.reshape(bp * tp, D_MODEL), *f1, gfin, final_norm=False)
    xp, s_p, h_p, c_p = _mix_prompt(xp.reshape(bp, tp, D_MODEL), p)
    y_p = _ffn(xp.reshape(bp * tp, D_MODEL), *f2, gfin, final_norm=True).reshape(bp, tp, D_MODEL)

    bs, ts, _ = x_sample.shape
    assert ts == 1
    xs = _ffn(x_sample.reshape(bs, D_MODEL), *f1, gfin, final_norm=False)
    xs, s_s, h_s, c_s = _mix_sample(xs, state_hgrn[l], state_lru[l], state_conv[l], p)
    y_s = _ffn(xs, *f2, gfin, final_norm=True).reshape(bs, ts, D_MODEL)

    return (y_p, y_s, s_p[None], h_p.reshape(bp, D_MODEL)[None], c_p[None],
            s_s[None], h_s[None], c_s[None])
```

```python
import functools

import jax
import jax.numpy as jnp
from jax import lax
from jax.experimental import pallas as pl
from jax.experimental.pallas import tpu as pltpu

D_MODEL = 1024
HEADS = 8
DK = 128
DV = 128
CHUNK = 64
LRU_BLOCKS = 8
LRU_BW = 128
LRU_C = 8.0
CONV_W = 4
D_FF = 2816
EPS = 1e-6
LOG2_E = 1.4426950408889634
N_SPLITS = 8

SUBLANES = 8
LANES = 128
MXU_COLS = 256
VMEM_LIMIT_BYTES = 56 * 1024 * 1024

FFN_ROWS = 512
MIX_ROWS = 256
SAMPLE_STATE_BLOCK = 8

BF16 = jnp.bfloat16
F32 = jnp.float32


def _dot(a, b):
    return jnp.dot(a, b, preferred_element_type=F32)


def _dot_tn(a, b):
    return lax.dot_general(a, b, (((0,), (0,)), ((), ())), preferred_element_type=F32)


def _dot_nt(a, b):
    return lax.dot_general(a, b, (((1,), (1,)), ((), ())), preferred_element_type=F32)


def _sigmoid(x):
    return 1.0 / (1.0 + jnp.exp2(x * (-LOG2_E)))


def _silu(x):
    return x * _sigmoid(x)


def _gelu_tanh(x):
    c = 0.7978845608028654
    return 0.5 * x * (1.0 + jnp.tanh(c * (x + 0.044715 * (x * x * x))))


def _rms(x, g):
    return x * lax.rsqrt(jnp.mean(x * x, axis=-1, keepdims=True) + EPS) * g


def _lower_bound(raw, layer):
    m = jnp.max(raw, axis=0, keepdims=True)
    e = jnp.exp(raw - m)
    den = jnp.sum(e, axis=0, keepdims=True)
    num = jnp.sum(e[0:layer + 1], axis=0, keepdims=True)
    return num / den


def _softplus(x):
    return jnp.maximum(x, 0.0) + jnp.log1p(jnp.exp(-jnp.abs(x)))


def _lru_coeffs(r_pre, i_pre, xc, lam):
    r = _sigmoid(r_pre)
    ig = _sigmoid(i_pre)
    log_a = (-LRU_C) * r * _softplus(-lam)
    a = jnp.exp(log_a)
    th = jnp.tanh(log_a)
    mult = jnp.sqrt(-2.0 * th / (1.0 - th))
    return a, mult * (ig * xc)


def _split3(x):
    p1 = x.astype(BF16)
    r1 = x - p1.astype(F32)
    p2 = r1.astype(BF16)
    r2 = r1 - p2.astype(F32)
    return p1, p2, r2.astype(BF16)


def _ffn_kernel(x_ref, g_ref, wg_ref, wu_ref, wd_ref, gf_ref, o_ref, *, final_norm):
    x = x_ref[...]
    h = _rms(x, g_ref[...]).astype(BF16)
    gate = _dot(h, wg_ref[...])
    up = _dot(h, wu_ref[...])
    act = (_silu(gate) * up).astype(BF16)
    y = x + 0.5 * _dot(act, wd_ref[...])
    if final_norm:
        y = _rms(y, gf_ref[...])
    o_ref[...] = y


def _resident(shape):
    nd = len(shape)
    return pl.BlockSpec(shape, lambda *_: (0,) * nd, pipeline_mode=pl.Buffered(1))


def _ffn(x2d, g, wg, wu, wd, gf, *, final_norm):
    n = x2d.shape[0]
    rows = min(FFN_ROWS, n)
    assert n % rows == 0
    return pl.pallas_call(
        functools.partial(_ffn_kernel, final_norm=final_norm),
        grid=(n // rows,),
        in_specs=[
            pl.BlockSpec((rows, D_MODEL), lambda i: (i, 0)),
            _resident((1, D_MODEL)),
            _resident((D_MODEL, D_FF)),
            _resident((D_MODEL, D_FF)),
            _resident((D_FF, D_MODEL)),
            _resident((1, D_MODEL)),
        ],
        out_specs=pl.BlockSpec((rows, D_MODEL), lambda i: (i, 0)),
        out_shape=jax.ShapeDtypeStruct((n, D_MODEL), F32),
        name="ffn_final" if final_norm else "ffn",
        compiler_params=pltpu.CompilerParams(
            dimension_semantics=("arbitrary",), vmem_limit_bytes=VMEM_LIMIT_BYTES),
    )(x2d, g, wg, wu, wd, gf)


SCAN_SEG = 4
SCAN_GROUP = SCAN_SEG * SUBLANES
CONV_PAD = SUBLANES
PIECE_ROWS = 2 * SUBLANES
QA_W = 2 * LANES


def _mix_prompt_kernel(x_ref, nmix_ref, win_ref, lbraw_ref, onorm_ref, convw_ref, convb_ref,
                       wax_ref, ba_ref, bx_ref, lam_ref, waup_ref, wbup_ref, wout_ref,
                       xo_ref, s_ref, hl_ref, cv_ref,
                       qa_s, ud_s, vb_s, oa_s, xr_s, a_s, b_s, h_s):
    rows = MIX_ROWS
    t = pl.program_id(1)

    @pl.when(t == 0)
    def _():
        s_ref[...] = jnp.zeros_like(s_ref)
        hl_ref[...] = jnp.zeros_like(hl_ref)
        xr_s[0:CONV_PAD, :] = jnp.zeros((CONV_PAD, D_MODEL), F32)

    x = x_ref[0]
    hb = _rms(x, nmix_ref[...]).astype(BF16)

    def proj(i):
        return _dot(hb, win_ref[:, i * D_MODEL:(i + 1) * D_MODEL])

    early = {}

    ncb = D_MODEL // MXU_COLS

    def issue_proj(i, cb):
        if cb < ncb and (i, cb) not in early:
            c0 = i * D_MODEL + cb * MXU_COLS
            early[(i, cb)] = _dot(hb, win_ref[:, c0:c0 + MXU_COLS])

    def gather(i):
        for cb in range(ncb):
            issue_proj(i, cb)
        return jnp.concatenate([early[(i, cb)] for cb in range(ncb)], axis=1)

    lb = _lower_bound(lbraw_ref[...], 0)
    q_act = _silu(proj(0))
    f = lb + (1.0 - lb) * _sigmoid(proj(1))
    kk = 1.0 - f
    vb_s[...] = proj(2).astype(BF16)

    ri = lax.broadcasted_iota(jnp.int32, (rows, rows), 0)
    ci = lax.broadcasted_iota(jnp.int32, (rows, rows), 1)
    same_chunk = (ri // CHUNK) == (ci // CHUNK)
    tril_blk = jnp.where((ri >= ci) & same_chunk, 1.0, 0.0).astype(BF16)
    g1, g2, g3 = _split3(jnp.log(f))
    b_all = _dot(tril_blk, g1) + _dot(tril_blk, g2) + _dot(tril_blk, g3)

    causal = (lax.broadcasted_iota(jnp.int32, (CHUNK, CHUNK), 0)
              >= lax.broadcasted_iota(jnp.int32, (CHUNK, CHUNK), 1))
    piece_row = lax.broadcasted_iota(jnp.int32, (PIECE_ROWS, D_MODEL), 0)
    ones_pv = jnp.ones((3 * PIECE_ROWS, DV), BF16)
    zeros_pv = jnp.zeros((3 * PIECE_ROWS, DV), BF16)
    zeros_cv = jnp.zeros((CHUNK, DV), BF16)

    nchunk = rows // CHUNK
    last = piece_row == PIECE_ROWS - 1
    zero_p = jnp.zeros((PIECE_ROWS, D_MODEL), BF16)

    for c in range(nchunk):
        issue_proj(3, c)
        rs = slice(c * CHUNK, (c + 1) * CHUNK)
        b = b_all[rs, :]
        bl = b[CHUNK - 1:CHUNK, :]
        qe = (q_act[rs, :] * jnp.exp(b)).astype(BF16)
        ke = (kk[rs, :] * jnp.exp(-b)).astype(BF16)
        kd = (kk[rs, :] * jnp.exp(bl - b)).astype(BF16)
        pieces = jnp.concatenate(
            [jnp.where(last, e, zero_p) for e in _split3(jnp.exp(b[CHUNK - PIECE_ROWS:, :]))], axis=0)
        for h in range(HEADS):
            hs = slice(h * DK, (h + 1) * DK)
            qa_s[rs, h * QA_W:h * QA_W + DK] = qe[:, hs]
            att = jnp.where(causal, _dot_nt(qe[:, hs], ke[:, hs]), 0.0)
            qa_s[rs, h * QA_W + DK:h * QA_W + DK + CHUNK] = att.astype(BF16)
            lhs = jnp.concatenate([kd[:, hs], pieces[:, hs]], axis=0)
            rhs = jnp.concatenate([jnp.concatenate([vb_s[rs, hs], zeros_cv], axis=1),
                                   jnp.concatenate([zeros_pv, ones_pv], axis=1)], axis=0)
            ud_s[c * HEADS + h] = _dot_tn(lhs, rhs)

    for c in range(nchunk):
        rs = slice(c * CHUNK, (c + 1) * CHUNK)
        for h in range(HEADS):
            hs = slice(h * DK, (h + 1) * DK)
            s_old = s_ref[0, h]
            rhs = jnp.concatenate([s_old.astype(BF16), vb_s[rs, hs]], axis=0)
            o = _dot(qa_s[rs, h * QA_W:h * QA_W + DK + CHUNK], rhs)
            oa_s[rs, hs] = o * lax.rsqrt(jnp.mean(o * o, axis=-1, keepdims=True) + EPS)
            ud = ud_s[c * HEADS + h]
            s_ref[0, h] = ud[:, DV:] * s_old + ud[:, 0:DV]

    o_a = (oa_s[...] * onorm_ref[...] * _silu(gather(3))).astype(BF16)
    up_a = {}

    def issue_up_a(cb):
        if cb < ncb and cb not in up_a:
            up_a[cb] = _dot(o_a, waup_ref[:, cb * MXU_COLS:(cb + 1) * MXU_COLS])

    xr_s[CONV_PAD:CONV_PAD + rows, :] = proj(4)
    xc = convb_ref[...]
    for j in range(CONV_W):
        off = CONV_PAD - (CONV_W - 1) + j
        xc = xc + xr_s[off:off + rows, :] * convw_ref[j:j + 1, :]
    tail = xr_s[CONV_PAD + rows - (CONV_W - 1):CONV_PAD + rows, :]
    cv_ref[0] = tail
    xr_s[CONV_PAD - (CONV_W - 1):CONV_PAD, :] = tail

    xcb = xc.astype(BF16)
    h_prev = hl_ref[0]
    for n in range(LRU_BLOCKS):
        ns = slice(n * LRU_BW, (n + 1) * LRU_BW)
        pre = _dot(xcb[:, ns], wax_ref[n])
        a, bterm = _lru_coeffs(pre[:, :LRU_BW] + ba_ref[:, ns], pre[:, LRU_BW:] + bx_ref[:, ns],
                               xc[:, ns], lam_ref[:, ns])
        a_s[n] = a
        b_s[n] = bterm
        issue_proj(5 + n % 2, n // 2)

    sub = lax.broadcasted_iota(jnp.int32, (SUBLANES, LANES), 0)
    h_last = []
    for n in range(LRU_BLOCKS):
        carry = h_prev[:, n * LRU_BW:(n + 1) * LRU_BW]
        for grp in range(rows // SCAN_GROUP):
            def strided(j, grp=grp):
                return pl.ds(grp * SCAN_GROUP + j, SUBLANES, stride=SCAN_SEG)
            pa = [a_s[n, strided(0), :]]
            hh = [b_s[n, strided(0), :]]
            for j in range(1, SCAN_SEG):
                aj = a_s[n, strided(j), :]
                hh.append(aj * hh[-1] + b_s[n, strided(j), :])
                pa.append(aj * pa[-1])
            pi, hi = pa[-1], hh[-1]
            for d in (1, 2, 4):
                keep = sub >= d
                hi = pi * jnp.where(keep, pltpu.roll(hi, d, 0), 0.0) + hi
                pi = pi * jnp.where(keep, pltpu.roll(pi, d, 0), 1.0)
            first = sub >= 1
            seg_in = (jnp.where(first, pltpu.roll(pi, 1, 0), 1.0) * carry
                      + jnp.where(first, pltpu.roll(hi, 1, 0), 0.0))
            carry = pi[SUBLANES - 1:, :] * carry + hi[SUBLANES - 1:, :]
            for j in range(SCAN_SEG):
                h_s[n, strided(j), :] = pa[j] * seg_in + hh[j]
        h_last.append(carry)
        if n % 2 == 0:
            issue_proj(7, n // 2)
        else:
            issue_up_a(n // 2)
    hl_ref[0] = jnp.concatenate(h_last, axis=1)
    gel = _gelu_tanh(gather(5))
    o_b = jnp.concatenate(
        [h_s[n] * gel[:, n * LRU_BW:(n + 1) * LRU_BW] for n in range(LRU_BLOCKS)], axis=1).astype(BF16)

    for cb in range(ncb):
        issue_up_a(cb)
    m = (_sigmoid(gather(6)) * jnp.concatenate([up_a[cb] for cb in range(ncb)], axis=1)
         + _sigmoid(gather(7)) * _dot(o_b, wbup_ref[...]))
    xo_ref[0] = x + _dot(m.astype(BF16), wout_ref[...])


def _mix_prompt(x, p):
    batch, seq, _ = x.shape
    rows = MIX_ROWS
    assert seq % rows == 0 and rows % CHUNK == 0 and rows >= CONV_W - 1
    row_vec = _resident((1, D_MODEL))
    out_shape = (
        jax.ShapeDtypeStruct((batch, seq, D_MODEL), F32),
        jax.ShapeDtypeStruct((batch, HEADS, DK, DV), F32),
        jax.ShapeDtypeStruct((batch, 1, D_MODEL), F32),
        jax.ShapeDtypeStruct((batch, CONV_W - 1, D_MODEL), F32),
    )
    return pl.pallas_call(
        _mix_prompt_kernel,
        grid=(batch, seq // rows),
        in_specs=[
            pl.BlockSpec((1, rows, D_MODEL), lambda b, t: (b, t, 0)),
            row_vec,
            _resident((D_MODEL, N_SPLITS * D_MODEL)),
            _resident(p["lb_raw"].shape),
            row_vec,
            _resident((CONV_W, D_MODEL)),
            row_vec,
            _resident((LRU_BLOCKS, LRU_BW, 2 * LRU_BW)),
            row_vec, row_vec, row_vec,
            _resident((D_MODEL, D_MODEL)),
            _resident((D_MODEL, D_MODEL)),
            _resident((D_MODEL, D_MODEL)),
        ],
        out_specs=(
            pl.BlockSpec((1, rows, D_MODEL), lambda b, t: (b, t, 0)),
            pl.BlockSpec((1, HEADS, DK, DV), lambda b, t: (b, 0, 0, 0)),
            pl.BlockSpec((1, 1, D_MODEL), lambda b, t: (b, 0, 0)),
            pl.BlockSpec((1, CONV_W - 1, D_MODEL), lambda b, t: (b, 0, 0)),
        ),
        out_shape=out_shape,
        scratch_shapes=[
            pltpu.VMEM((rows, HEADS * QA_W), BF16),
            pltpu.VMEM((rows // CHUNK * HEADS, DK, 2 * DV), F32),
            pltpu.VMEM((rows, D_MODEL), BF16),
            pltpu.VMEM((rows, D_MODEL), F32),
            pltpu.VMEM((CONV_PAD + rows, D_MODEL), F32),
            pltpu.VMEM((LRU_BLOCKS, rows, LRU_BW), F32),
            pltpu.VMEM((LRU_BLOCKS, rows, LRU_BW), F32),
            pltpu.VMEM((LRU_BLOCKS, rows, LRU_BW), F32),
        ],
        name="mix_prompt",
        compiler_params=pltpu.CompilerParams(
            dimension_semantics=("arbitrary", "arbitrary"), vmem_limit_bytes=VMEM_LIMIT_BYTES),
    )(x, p["mix_norm"], p["w_in"], p["lb_raw"], p["o_norm"], p["conv_w"], p["conv_b"],
      p["w_ax"], p["b_a"], p["b_x"], p["lam"], p["w_a_up"], p["w_b_up"], p["w_out"])


def _mix_sample_kernel(x_ref, nmix_ref, win_ref, lbraw_ref, onorm_ref, convw_ref, convb_ref,
                       wax_ref, ba_ref, bx_ref, lam_ref, waup_ref, wbup_ref, wout_ref,
                       sin_ref, hlin_ref, cvin_ref,
                       xo_ref, sout_ref, hl_ref, cv_ref,
                       qt_s, ft_s, v_s, oa_s, gsil_s, ga_s, mb_s):
    nseq = x_ref.shape[0]
    blk = SAMPLE_STATE_BLOCK
    j = pl.program_id(0)

    @pl.when(j == 0)
    def _():
        x = x_ref[...]
        hb = _rms(x, nmix_ref[...]).astype(BF16)

        def proj(i):
            return _dot(hb, win_ref[:, i * D_MODEL:(i + 1) * D_MODEL])

        lb = _lower_bound(lbraw_ref[...], 0)
        qt_s[...] = _silu(proj(0)).T
        ft_s[...] = (lb + (1.0 - lb) * _sigmoid(proj(1))).T
        v_s[...] = proj(2)
        gsil_s[...] = onorm_ref[...] * _silu(proj(3))

        xr = proj(4)
        xc = convb_ref[...] + xr * convw_ref[CONV_W - 1:CONV_W, :]
        for i in range(CONV_W - 1):
            xc = xc + cvin_ref[:, i, :] * convw_ref[i:i + 1, :]
        for i in range(CONV_W - 2):
            cv_ref[:, i, :] = cvin_ref[:, i + 1, :]
        cv_ref[:, CONV_W - 2, :] = xr

        xcb = xc.astype(BF16)
        h_parts = []
        for n in range(LRU_BLOCKS):
            ns = slice(n * LRU_BW, (n + 1) * LRU_BW)
            pre = _dot(xcb[:, ns], wax_ref[n])
            a, bterm = _lru_coeffs(pre[:, :LRU_BW] + ba_ref[:, ns], pre[:, LRU_BW:] + bx_ref[:, ns],
                                   xc[:, ns], lam_ref[:, ns])
            h_parts.append(a * hlin_ref[:, ns] + bterm)
        h_new = jnp.concatenate(h_parts, axis=1)
        hl_ref[...] = h_new
        o_b = (h_new * _gelu_tanh(proj(5))).astype(BF16)
        ga_s[...] = _sigmoid(proj(6))
        mb_s[...] = _sigmoid(proj(7)) * _dot(o_b, wbup_ref[...])

    b0 = pl.multiple_of(j * blk, blk)
    shifts = [lax.rem(nseq - b0 + p * blk, nseq) for p in range(3)]

    lane = lax.broadcasted_iota(jnp.int32, (DK, nseq), 1)
    sel_r = lax.broadcasted_iota(jnp.int32, (nseq, blk * DV), 0)
    sel_c = lax.broadcasted_iota(jnp.int32, (nseq, blk * DV), 1)
    selector = jnp.where((sel_r < 3 * blk) & (jnp.bitwise_and(sel_r, blk - 1) == sel_c // DV),
                         1.0, 0.0).astype(BF16)

    def spread(cols_t):
        p1 = cols_t.astype(BF16).astype(F32)
        r1 = cols_t - p1
        p2 = r1.astype(BF16).astype(F32)
        p3 = r1 - p2
        lhs = jnp.where(lane < blk, pltpu.roll(p1, shifts[0], 1),
                        jnp.where(lane < 2 * blk, pltpu.roll(p2, shifts[1], 1),
                                  jnp.where(lane < 3 * blk, pltpu.roll(p3, shifts[2], 1), 0.0)))
        return _dot(lhs.astype(BF16), selector)

    for h in range(HEADS):
        hrows = slice(h * DK, (h + 1) * DK)
        q_all = spread(qt_s[hrows, :])
        f_all = spread(ft_s[hrows, :])
        v_rows = v_s[pl.ds(b0, blk), hrows]
        o_rows = []
        for i in range(blk):
            f_col = f_all[:, i * DV:(i + 1) * DV]
            s_new = f_col * sin_ref[i, h] + (1.0 - f_col) * v_rows[i:i + 1, :]
            sout_ref[i, h] = s_new
            o_rows.append(jnp.sum(q_all[:, i * DV:(i + 1) * DV] * s_new, axis=0, keepdims=True))
        o = jnp.concatenate(o_rows, axis=0)
        oa_s[pl.ds(b0, blk), hrows] = o * lax.rsqrt(
            jnp.mean(o * o, axis=-1, keepdims=True) + EPS)

    @pl.when(j == pl.num_programs(0) - 1)
    def _():
        o_a = (oa_s[...] * gsil_s[...]).astype(BF16)
        m = ga_s[...] * _dot(o_a, waup_ref[...]) + mb_s[...]
        xo_ref[...] = x_ref[...] + _dot(m.astype(BF16), wout_ref[...])


def _mix_sample(x2d, s_in, h_in, c_in, p):
    nseq = x2d.shape[0]
    blk = SAMPLE_STATE_BLOCK
    assert nseq == LANES and nseq % blk == 0
    row_vec = _resident((1, D_MODEL))
    tok = _resident((nseq, D_MODEL))
    conv_state = _resident((nseq, CONV_W - 1, D_MODEL))
    out_shape = (
        jax.ShapeDtypeStruct((nseq, D_MODEL), F32),
        jax.ShapeDtypeStruct((nseq, HEADS, DK, DV), F32),
        jax.ShapeDtypeStruct((nseq, D_MODEL), F32),
        jax.ShapeDtypeStruct((nseq, CONV_W - 1, D_MODEL), F32),
    )
    state_spec = pl.BlockSpec((blk, HEADS, DK, DV), lambda j: (j, 0, 0, 0))
    return pl.pallas_call(
        _mix_sample_kernel,
        grid=(nseq // blk,),
        in_specs=[
            tok, row_vec,
            _resident((D_MODEL, N_SPLITS * D_MODEL)),
            _resident(p["lb_raw"].shape),
            row_vec,
            _resident((CONV_W, D_MODEL)),
            row_vec,
            _resident((LRU_BLOCKS, LRU_BW, 2 * LRU_BW)),
            row_vec, row_vec, row_vec,
            _resident((D_MODEL, D_MODEL)),
            _resident((D_MODEL, D_MODEL)),
            _resident((D_MODEL, D_MODEL)),
            state_spec, tok, conv_state,
        ],
        out_specs=(
            pl.BlockSpec((nseq, D_MODEL), lambda j: (0, 0)),
            state_spec,
            pl.BlockSpec((nseq, D_MODEL), lambda j: (0, 0)),
            pl.BlockSpec((nseq, CONV_W - 1, D_MODEL), lambda j: (0, 0, 0)),
        ),
        out_shape=out_shape,
        scratch_shapes=[
            pltpu.VMEM((D_MODEL, nseq), F32),
            pltpu.VMEM((D_MODEL, nseq), F32),
            pltpu.VMEM((nseq, D_MODEL), F32),
            pltpu.VMEM((nseq, D_MODEL), F32),
            pltpu.VMEM((nseq, D_MODEL), F32),
            pltpu.VMEM((nseq, D_MODEL), F32),
            pltpu.VMEM((nseq, D_MODEL), F32),
        ],
        name="mix_sample",
        compiler_params=pltpu.CompilerParams(
            dimension_semantics=("arbitrary",), vmem_limit_bytes=VMEM_LIMIT_BYTES),
    )(x2d, p["mix_norm"], p["w_in"], p["lb_raw"], p["o_norm"], p["conv_w"], p["conv_b"],
      p["w_ax"], p["b_a"], p["b_x"], p["lam"], p["w_a_up"], p["w_b_up"], p["w_out"],
      s_in, h_in, c_in)


def kernel(x_prompt, x_sample, state_hgrn, state_lru, state_conv, ffn1_norm, ffn1_w_gate, ffn1_w_up, ffn1_w_down, mix_norm, w_in, hgrn_lower_bounds, hgrn_out_norm, conv_w, conv_b, lru_w_a, lru_b_a, lru_w_x, lru_b_x, lru_lambda, w_a_up, w_b_up, w_out, ffn2_norm, ffn2_w_gate, ffn2_w_up, ffn2_w_down, final_norm):
    depth = w_in.shape[0]
    assert depth == 1, "single-layer trunk"
    l = 0
    row = lambda v: v.reshape(1, -1).astype(F32)
    bf = lambda w: w.astype(BF16)
    p = {
        "mix_norm": row(mix_norm[l]),
        "w_in": bf(w_in[l]),
        "lb_raw": hgrn_lower_bounds.astype(F32),
        "o_norm": row(hgrn_out_norm[l]),
        "conv_w": conv_w[l].astype(F32),
        "conv_b": row(conv_b[l]),
        "w_ax": bf(jnp.concatenate([lru_w_a[l], lru_w_x[l]], axis=-1)),
        "b_a": row(lru_b_a[l]),
        "b_x": row(lru_b_x[l]),
        "lam": row(lru_lambda[l]),
        "w_a_up": bf(w_a_up[l]),
        "w_b_up": bf(w_b_up[l]),
        "w_out": bf(w_out[l]),
    }
    f1 = (row(ffn1_norm[l]), bf(ffn1_w_gate[l]), bf(ffn1_w_up[l]), bf(ffn1_w_down[l]))
    f2 = (row(ffn2_norm[l]), bf(ffn2_w_gate[l]), bf(ffn2_w_up[l]), bf(ffn2_w_down[l]))
    gfin = row(final_norm)

    bp, tp, _ = x_prompt.shape
    xp = _ffn(x_prompt.reshape(bp * tp, D_MODEL), *f1, gfin, final_norm=False)
    xp, s_p, h_p, c_p = _mix_prompt(xp.reshape(bp, tp, D_MODEL), p)
    y_p = _ffn(xp.reshape(bp * tp, D_MODEL), *f2, gfin, final_norm=True).reshape(bp, tp, D_MODEL)

    bs, ts, _ = x_sample.shape
    assert ts == 1
    xs = _ffn(x_sample.reshape(bs, D_MODEL), *f1, gfin, final_norm=False)
    xs, s_s, h_s, c_s = _mix_sample(xs, state_hgrn[l], state_lru[l], state_conv[l], p)
    y_s = _ffn(xs, *f2, gfin, final_norm=True).reshape(bs, ts, D_MODEL)

    return (y_p, y_s, s_p[None], h_p.reshape(bp, D_MODEL)[None], c_p[None],
            s_s[None], h_s[None], c_s[None])
```

```python
import functools

import jax
import jax.numpy as jnp
from jax import lax
from jax.experimental import pallas as pl
from jax.experimental.pallas import tpu as pltpu

D_MODEL = 1024
HEADS = 8
DK = 128
DV = 128
CHUNK = 64
LRU_BLOCKS = 8
LRU_BW = 128
LRU_C = 8.0
CONV_W = 4
D_FF = 2816
EPS = 1e-6
LOG2_E = 1.4426950408889634
N_SPLITS = 8

SUBLANES = 8
LANES = 128
MXU_COLS = 256
VMEM_LIMIT_BYTES = 56 * 1024 * 1024

FFN_ROWS = 512
MIX_ROWS = 256
SAMPLE_STATE_BLOCK = 8

BF16 = jnp.bfloat16
F32 = jnp.float32


def _dot(a, b):
    return jnp.dot(a, b, preferred_element_type=F32)


def _dot_tn(a, b):
    return lax.dot_general(a, b, (((0,), (0,)), ((), ())), preferred_element_type=F32)


def _dot_nt(a, b):
    return lax.dot_general(a, b, (((1,), (1,)), ((), ())), preferred_element_type=F32)


def _sigmoid(x):
    return 1.0 / (1.0 + jnp.exp2(x * (-LOG2_E)))


def _silu(x):
    return x * _sigmoid(x)


def _gelu_tanh(x):
    c = 0.7978845608028654
    return 0.5 * x * (1.0 + jnp.tanh(c * (x + 0.044715 * (x * x * x))))


def _rms(x, g):
    return x * lax.rsqrt(jnp.mean(x * x, axis=-1, keepdims=True) + EPS) * g


def _lower_bound(raw, layer):
    m = jnp.max(raw, axis=0, keepdims=True)
    e = jnp.exp(raw - m)
    den = jnp.sum(e, axis=0, keepdims=True)
    num = jnp.sum(e[0:layer + 1], axis=0, keepdims=True)
    return num / den


def _softplus(x):
    return jnp.maximum(x, 0.0) + jnp.log1p(jnp.exp(-jnp.abs(x)))


def _lru_coeffs(r_pre, i_pre, xc, lam):
    r = _sigmoid(r_pre)
    ig = _sigmoid(i_pre)
    log_a = (-LRU_C) * r * _softplus(-lam)
    a = jnp.exp(log_a)
    th = jnp.tanh(log_a)
    mult = jnp.sqrt(-2.0 * th / (1.0 - th))
    return a, mult * (ig * xc)


def _split3(x):
    p1 = x.astype(BF16)
    r1 = x - p1.astype(F32)
    p2 = r1.astype(BF16)
    r2 = r1 - p2.astype(F32)
    return p1, p2, r2.astype(BF16)


def _ffn_kernel(x_ref, g_ref, wg_ref, wu_ref, wd_ref, gf_ref, o_ref, *, final_norm):
    x = x_ref[...]
    h = _rms(x, g_ref[...]).astype(BF16)
    gate = _dot(h, wg_ref[...])
    up = _dot(h, wu_ref[...])
    act = (_silu(gate) * up).astype(BF16)
    y = x + 0.5 * _dot(act, wd_ref[...])
    if final_norm:
        y = _rms(y, gf_ref[...])
    o_ref[...] = y


def _resident(shape):
    nd = len(shape)
    return pl.BlockSpec(shape, lambda *_: (0,) * nd, pipeline_mode=pl.Buffered(1))


def _ffn(x2d, g, wg, wu, wd, gf, *, final_norm):
    n = x2d.shape[0]
    rows = min(FFN_ROWS, n)
    assert n % rows == 0
    return pl.pallas_call(
        functools.partial(_ffn_kernel, final_norm=final_norm),
        grid=(n // rows,),
        in_specs=[
            pl.BlockSpec((rows, D_MODEL), lambda i: (i, 0)),
            _resident((1, D_MODEL)),
            _resident((D_MODEL, D_FF)),
            _resident((D_MODEL, D_FF)),
            _resident((D_FF, D_MODEL)),
            _resident((1, D_MODEL)),
        ],
        out_specs=pl.BlockSpec((rows, D_MODEL), lambda i: (i, 0)),
        out_shape=jax.ShapeDtypeStruct((n, D_MODEL), F32),
        name="ffn_final" if final_norm else "ffn",
        compiler_params=pltpu.CompilerParams(
            dimension_semantics=("arbitrary",), vmem_limit_bytes=VMEM_LIMIT_BYTES),
    )(x2d, g, wg, wu, wd, gf)


SCAN_SEG = 4
SCAN_GROUP = SCAN_SEG * SUBLANES
CONV_PAD = SUBLANES
PIECE_ROWS = 2 * SUBLANES
QA_W = 2 * LANES


def _mix_prompt_kernel(x_ref, nmix_ref, win_ref, lbraw_ref, onorm_ref, convw_ref, convb_ref,
                       wax_ref, ba_ref, bx_ref, lam_ref, waup_ref, wbup_ref, wout_ref,
                       xo_ref, s_ref, hl_ref, cv_ref,
                       qa_s, ud_s, vb_s, oa_s, xr_s, h_s):
    rows = MIX_ROWS
    t = pl.program_id(1)

    @pl.when(t == 0)
    def _():
        s_ref[...] = jnp.zeros_like(s_ref)
        hl_ref[...] = jnp.zeros_like(hl_ref)
        xr_s[:, 0:CONV_PAD, :] = jnp.zeros((LRU_BLOCKS, CONV_PAD, LRU_BW), F32)

    x = x_ref[0]
    hb = _rms(x, nmix_ref[...]).astype(BF16)

    def proj(i):
        return _dot(hb, win_ref[:, i * D_MODEL:(i + 1) * D_MODEL])

    early = {}

    ncb = D_MODEL // MXU_COLS

    def issue_proj(i, cb):
        if cb < ncb and (i, cb) not in early:
            c0 = i * D_MODEL + cb * MXU_COLS
            early[(i, cb)] = _dot(hb, win_ref[:, c0:c0 + MXU_COLS])

    def gather(i):
        for cb in range(ncb):
            issue_proj(i, cb)
        return jnp.concatenate([early[(i, cb)] for cb in range(ncb)], axis=1)

    xr = proj(4)
    h_prev = hl_ref[0]
    sub = lax.broadcasted_iota(jnp.int32, (SUBLANES, LANES), 0)
    ngroup = rows // SCAN_GROUP
    tails = []
    h_last = []

    def conv_and_gates(n):
        ns = slice(n * LRU_BW, (n + 1) * LRU_BW)
        xr_s[n, CONV_PAD:CONV_PAD + rows, :] = xr[:, ns]
        taps = [jnp.broadcast_to(convw_ref[j:j + 1, ns], (SUBLANES, LRU_BW)) for j in range(CONV_W)]
        bias = jnp.broadcast_to(convb_ref[:, ns], (SUBLANES, LRU_BW))
        xc_parts = []
        for grp in range(ngroup):
            xin = {d: xr_s[n, pl.ds(CONV_PAD + grp * SCAN_GROUP + d, SUBLANES, stride=SCAN_SEG), :]
                   for d in range(1 - CONV_W, SCAN_SEG)}
            for j in range(SCAN_SEG):
                acc = bias
                for tap in range(CONV_W):
                    acc = acc + xin[j + tap - (CONV_W - 1)] * taps[tap]
                xc_parts.append(acc)
        xc = jnp.concatenate(xc_parts, axis=0)
        tail = xr_s[n, CONV_PAD + rows - (CONV_W - 1):CONV_PAD + rows, :]
        tails.append(tail)
        xr_s[n, CONV_PAD - (CONV_W - 1):CONV_PAD, :] = tail
        return xc, _dot(xc.astype(BF16), wax_ref[n])

    def coeffs_and_scan(n, xc, pre):
        ns = slice(n * LRU_BW, (n + 1) * LRU_BW)
        a, bterm = _lru_coeffs(pre[:, :LRU_BW] + ba_ref[:, ns], pre[:, LRU_BW:] + bx_ref[:, ns],
                               xc, lam_ref[:, ns])
        carry = h_prev[:, ns]
        for grp in range(ngroup):
            def vreg(v, j, grp=grp):
                r0 = grp * SCAN_GROUP + j * SUBLANES
                return v[r0:r0 + SUBLANES, :]

            def strided(j, grp=grp):
                return pl.ds(grp * SCAN_GROUP + j, SUBLANES, stride=SCAN_SEG)
            pa = [vreg(a, 0)]
            hh = [vreg(bterm, 0)]
            for j in range(1, SCAN_SEG):
                aj = vreg(a, j)
                hh.append(aj * hh[-1] + vreg(bterm, j))
                pa.append(aj * pa[-1])
            pi, hi = pa[-1], hh[-1]
            for d in (1, 2, 4):
                keep = sub >= d
                hi = pi * jnp.where(keep, pltpu.roll(hi, d, 0), 0.0) + hi
                pi = pi * jnp.where(keep, pltpu.roll(pi, d, 0), 1.0)
            first = sub >= 1
            seg_in = (jnp.where(first, pltpu.roll(pi, 1, 0), 1.0) * carry
                      + jnp.where(first, pltpu.roll(hi, 1, 0), 0.0))
            carry = pi[SUBLANES - 1:, :] * carry + hi[SUBLANES - 1:, :]
            for j in range(SCAN_SEG):
                h_s[n, strided(j), :] = pa[j] * seg_in + hh[j]
        h_last.append(carry)

    staged = conv_and_gates(0)
    for n in range(LRU_BLOCKS):
        ahead = conv_and_gates(n + 1) if n + 1 < LRU_BLOCKS else None
        issue_proj(n // 4, n % 4)
        coeffs_and_scan(n, *staged)
        issue_proj(2 + n // 4, n % 4)
        staged = ahead
    hl_ref[0] = jnp.concatenate(h_last, axis=1)
    cv_ref[0] = jnp.concatenate(tails, axis=1)

    lb = _lower_bound(lbraw_ref[...], 0)
    q_act = _silu(gather(0))
    f = lb + (1.0 - lb) * _sigmoid(gather(1))
    kk = 1.0 - f
    vb_s[...] = gather(2).astype(BF16)

    ri = lax.broadcasted_iota(jnp.int32, (rows, rows), 0)
    ci = lax.broadcasted_iota(jnp.int32, (rows, rows), 1)
    same_chunk = (ri // CHUNK) == (ci // CHUNK)
    tril_blk = jnp.where((ri >= ci) & same_chunk, 1.0, 0.0).astype(BF16)
    g1, g2, g3 = _split3(jnp.log(f))
    b_all = _dot(tril_blk, g1) + _dot(tril_blk, g2) + _dot(tril_blk, g3)

    causal = (lax.broadcasted_iota(jnp.int32, (CHUNK, CHUNK), 0)
              >= lax.broadcasted_iota(jnp.int32, (CHUNK, CHUNK), 1))
    piece_row = lax.broadcasted_iota(jnp.int32, (PIECE_ROWS, D_MODEL), 0)
    ones_pv = jnp.ones((3 * PIECE_ROWS, DV), BF16)
    zeros_pv = jnp.zeros((3 * PIECE_ROWS, DV), BF16)
    zeros_cv = jnp.zeros((CHUNK, DV), BF16)

    nchunk = rows // CHUNK
    last = piece_row == PIECE_ROWS - 1
    zero_p = jnp.zeros((PIECE_ROWS, D_MODEL), BF16)

    for c in range(nchunk):
        issue_proj(5, c)
        rs = slice(c * CHUNK, (c + 1) * CHUNK)
        b = b_all[rs, :]
        bl = b[CHUNK - 1:CHUNK, :]
        qe = (q_act[rs, :] * jnp.exp(b)).astype(BF16)
        ke = (kk[rs, :] * jnp.exp(-b)).astype(BF16)
        kd = (kk[rs, :] * jnp.exp(bl - b)).astype(BF16)
        pieces = jnp.concatenate(
            [jnp.where(last, e, zero_p) for e in _split3(jnp.exp(b[CHUNK - PIECE_ROWS:, :]))], axis=0)
        for h in range(HEADS):
            hs = slice(h * DK, (h + 1) * DK)
            qa_s[rs, h * QA_W:h * QA_W + DK] = qe[:, hs]
            att = jnp.where(causal, _dot_nt(qe[:, hs], ke[:, hs]), 0.0)
            qa_s[rs, h * QA_W + DK:h * QA_W + DK + CHUNK] = att.astype(BF16)
            lhs = jnp.concatenate([kd[:, hs], pieces[:, hs]], axis=0)
            rhs = jnp.concatenate([jnp.concatenate([vb_s[rs, hs], zeros_cv], axis=1),
                                   jnp.concatenate([zeros_pv, ones_pv], axis=1)], axis=0)
            ud_s[c * HEADS + h] = _dot_tn(lhs, rhs)

    for c in range(nchunk):
        issue_proj(6, c)
        issue_proj(7, c)
        rs = slice(c * CHUNK, (c + 1) * CHUNK)
        for h in range(HEADS):
            hs = slice(h * DK, (h + 1) * DK)
            s_old = s_ref[0, h]
            rhs = jnp.concatenate([s_old.astype(BF16), vb_s[rs, hs]], axis=0)
            o = _dot(qa_s[rs, h * QA_W:h * QA_W + DK + CHUNK], rhs)
            oa_s[rs, hs] = o * lax.rsqrt(jnp.mean(o * o, axis=-1, keepdims=True) + EPS)
            ud = ud_s[c * HEADS + h]
            s_ref[0, h] = ud[:, DV:] * s_old + ud[:, 0:DV]

    o_a = (oa_s[...] * onorm_ref[...] * _silu(gather(3))).astype(BF16)
    gel = _gelu_tanh(gather(5))
    o_b = jnp.concatenate(
        [h_s[n] * gel[:, n * LRU_BW:(n + 1) * LRU_BW] for n in range(LRU_BLOCKS)], axis=1).astype(BF16)

    m = (_sigmoid(gather(6)) * _dot(o_a, waup_ref[...])
         + _sigmoid(gather(7)) * _dot(o_b, wbup_ref[...]))
    xo_ref[0] = x + _dot(m.astype(BF16), wout_ref[...])


def _mix_prompt(x, p):
    batch, seq, _ = x.shape
    rows = MIX_ROWS
    assert seq % rows == 0 and rows % CHUNK == 0 and rows >= CONV_W - 1
    row_vec = _resident((1, D_MODEL))
    out_shape = (
        jax.ShapeDtypeStruct((batch, seq, D_MODEL), F32),
        jax.ShapeDtypeStruct((batch, HEADS, DK, DV), F32),
        jax.ShapeDtypeStruct((batch, 1, D_MODEL), F32),
        jax.ShapeDtypeStruct((batch, CONV_W - 1, D_MODEL), F32),
    )
    return pl.pallas_call(
        _mix_prompt_kernel,
        grid=(batch, seq // rows),
        in_specs=[
            pl.BlockSpec((1, rows, D_MODEL), lambda b, t: (b, t, 0)),
            row_vec,
            _resident((D_MODEL, N_SPLITS * D_MODEL)),
            _resident(p["lb_raw"].shape),
            row_vec,
            _resident((CONV_W, D_MODEL)),
            row_vec,
            _resident((LRU_BLOCKS, LRU_BW, 2 * LRU_BW)),
            row_vec, row_vec, row_vec,
            _resident((D_MODEL, D_MODEL)),
            _resident((D_MODEL, D_MODEL)),
            _resident((D_MODEL, D_MODEL)),
        ],
        out_specs=(
            pl.BlockSpec((1, rows, D_MODEL), lambda b, t: (b, t, 0)),
            pl.BlockSpec((1, HEADS, DK, DV), lambda b, t: (b, 0, 0, 0)),
            pl.BlockSpec((1, 1, D_MODEL), lambda b, t: (b, 0, 0)),
            pl.BlockSpec((1, CONV_W - 1, D_MODEL), lambda b, t: (b, 0, 0)),
        ),
        out_shape=out_shape,
        scratch_shapes=[
            pltpu.VMEM((rows, HEADS * QA_W), BF16),
            pltpu.VMEM((rows // CHUNK * HEADS, DK, 2 * DV), F32),
            pltpu.VMEM((rows, D_MODEL), BF16),
            pltpu.VMEM((rows, D_MODEL), F32),
            pltpu.VMEM((LRU_BLOCKS, CONV_PAD + rows, LRU_BW), F32),
            pltpu.VMEM((LRU_BLOCKS, rows, LRU_BW), F32),
        ],
        name="mix_prompt",
        compiler_params=pltpu.CompilerParams(
            dimension_semantics=("arbitrary", "arbitrary"), vmem_limit_bytes=VMEM_LIMIT_BYTES),
    )(x, p["mix_norm"], p["w_in"], p["lb_raw"], p["o_norm"], p["conv_w"], p["conv_b"],
      p["w_ax"], p["b_a"], p["b_x"], p["lam"], p["w_a_up"], p["w_b_up"], p["w_out"])


def _mix_sample_kernel(x_ref, nmix_ref, win_ref, lbraw_ref, onorm_ref, convw_ref, convb_ref,
                       wax_ref, ba_ref, bx_ref, lam_ref, waup_ref, wbup_ref, wout_ref,
                       sin_ref, hlin_ref, cvin_ref,
                       xo_ref, sout_ref, hl_ref, cv_ref,
                       qt_s, ft_s, v_s, oa_s, gsil_s, ga_s, mb_s):
    nseq = x_ref.shape[0]
    blk = SAMPLE_STATE_BLOCK
    j = pl.program_id(0)

    @pl.when(j == 0)
    def _():
        x = x_ref[...]
        hb = _rms(x, nmix_ref[...]).astype(BF16)

        def proj(i):
            return _dot(hb, win_ref[:, i * D_MODEL:(i + 1) * D_MODEL])

        lb = _lower_bound(lbraw_ref[...], 0)
        qt_s[...] = _silu(proj(0)).T
        ft_s[...] = (lb + (1.0 - lb) * _sigmoid(proj(1))).T
        v_s[...] = proj(2)
        gsil_s[...] = onorm_ref[...] * _silu(proj(3))

        xr = proj(4)
        xc = convb_ref[...] + xr * convw_ref[CONV_W - 1:CONV_W, :]
        for i in range(CONV_W - 1):
            xc = xc + cvin_ref[:, i, :] * convw_ref[i:i + 1, :]
        for i in range(CONV_W - 2):
            cv_ref[:, i, :] = cvin_ref[:, i + 1, :]
        cv_ref[:, CONV_W - 2, :] = xr

        xcb = xc.astype(BF16)
        h_parts = []
        for n in range(LRU_BLOCKS):
            ns = slice(n * LRU_BW, (n + 1) * LRU_BW)
            pre = _dot(xcb[:, ns], wax_ref[n])
            a, bterm = _lru_coeffs(pre[:, :LRU_BW] + ba_ref[:, ns], pre[:, LRU_BW:] + bx_ref[:, ns],
                                   xc[:, ns], lam_ref[:, ns])
            h_parts.append(a * hlin_ref[:, ns] + bterm)
        h_new = jnp.concatenate(h_parts, axis=1)
        hl_ref[...] = h_new
        o_b = (h_new * _gelu_tanh(proj(5))).astype(BF16)
        ga_s[...] = _sigmoid(proj(6))
        mb_s[...] = _sigmoid(proj(7)) * _dot(o_b, wbup_ref[...])

    b0 = pl.multiple_of(j * blk, blk)
    shifts = [lax.rem(nseq - b0 + p * blk, nseq) for p in range(3)]

    lane = lax.broadcasted_iota(jnp.int32, (DK, nseq), 1)
    sel_r = lax.broadcasted_iota(jnp.int32, (nseq, blk * DV), 0)
    sel_c = lax.broadcasted_iota(jnp.int32, (nseq, blk * DV), 1)
    selector = jnp.where((sel_r < 3 * blk) & (jnp.bitwise_and(sel_r, blk - 1) == sel_c // DV),
                         1.0, 0.0).astype(BF16)

    def spread(cols_t):
        p1 = cols_t.astype(BF16).astype(F32)
        r1 = cols_t - p1
        p2 = r1.astype(BF16).astype(F32)
        p3 = r1 - p2
        lhs = jnp.where(lane < blk, pltpu.roll(p1, shifts[0], 1),
                        jnp.where(lane < 2 * blk, pltpu.roll(p2, shifts[1], 1),
                                  jnp.where(lane < 3 * blk, pltpu.roll(p3, shifts[2], 1), 0.0)))
        return _dot(lhs.astype(BF16), selector)

    for h in range(HEADS):
        hrows = slice(h * DK, (h + 1) * DK)
        q_all = spread(qt_s[hrows, :])
        f_all = spread(ft_s[hrows, :])
        v_rows = v_s[pl.ds(b0, blk), hrows]
        o_rows = []
        for i in range(blk):
            f_col = f_all[:, i * DV:(i + 1) * DV]
            s_new = f_col * sin_ref[i, h] + (1.0 - f_col) * v_rows[i:i + 1, :]
            sout_ref[i, h] = s_new
            o_rows.append(jnp.sum(q_all[:, i * DV:(i + 1) * DV] * s_new, axis=0, keepdims=True))
        o = jnp.concatenate(o_rows, axis=0)
        oa_s[pl.ds(b0, blk), hrows] = o * lax.rsqrt(
            jnp.mean(o * o, axis=-1, keepdims=True) + EPS)

    @pl.when(j == pl.num_programs(0) - 1)
    def _():
        o_a = (oa_s[...] * gsil_s[...]).astype(BF16)
        m = ga_s[...] * _dot(o_a, waup_ref[...]) + mb_s[...]
        xo_ref[...] = x_ref[...] + _dot(m.astype(BF16), wout_ref[...])


def _mix_sample(x2d, s_in, h_in, c_in, p):
    nseq = x2d.shape[0]
    blk = SAMPLE_STATE_BLOCK
    assert nseq == LANES and nseq % blk == 0
    row_vec = _resident((1, D_MODEL))
    tok = _resident((nseq, D_MODEL))
    conv_state = _resident((nseq, CONV_W - 1, D_MODEL))
    out_shape = (
        jax.ShapeDtypeStruct((nseq, D_MODEL), F32),
        jax.ShapeDtypeStruct((nseq, HEADS, DK, DV), F32),
        jax.ShapeDtypeStruct((nseq, D_MODEL), F32),
        jax.ShapeDtypeStruct((nseq, CONV_W - 1, D_MODEL), F32),
    )
    state_spec = pl.BlockSpec((blk, HEADS, DK, DV), lambda j: (j, 0, 0, 0))
    return pl.pallas_call(
        _mix_sample_kernel,
        grid=(nseq // blk,),
        in_specs=[
            tok, row_vec,
            _resident((D_MODEL, N_SPLITS * D_MODEL)),
            _resident(p["lb_raw"].shape),
            row_vec,
            _resident((CONV_W, D_MODEL)),
            row_vec,
            _resident((LRU_BLOCKS, LRU_BW, 2 * LRU_BW)),
            row_vec, row_vec, row_vec,
            _resident((D_MODEL, D_MODEL)),
            _resident((D_MODEL, D_MODEL)),
            _resident((D_MODEL, D_MODEL)),
            state_spec, tok, conv_state,
        ],
        out_specs=(
            pl.BlockSpec((nseq, D_MODEL), lambda j: (0, 0)),
            state_spec,
            pl.BlockSpec((nseq, D_MODEL), lambda j: (0, 0)),
            pl.BlockSpec((nseq, CONV_W - 1, D_MODEL), lambda j: (0, 0, 0)),
        ),
        out_shape=out_shape,
        scratch_shapes=[
            pltpu.VMEM((D_MODEL, nseq), F32),
            pltpu.VMEM((D_MODEL, nseq), F32),
            pltpu.VMEM((nseq, D_MODEL), F32),
            pltpu.VMEM((nseq, D_MODEL), F32),
            pltpu.VMEM((nseq, D_MODEL), F32),
            pltpu.VMEM((nseq, D_MODEL), F32),
            pltpu.VMEM((nseq, D_MODEL), F32),
        ],
        name="mix_sample",
        compiler_params=pltpu.CompilerParams(
            dimension_semantics=("arbitrary",), vmem_limit_bytes=VMEM_LIMIT_BYTES),
    )(x2d, p["mix_norm"], p["w_in"], p["lb_raw"], p["o_norm"], p["conv_w"], p["conv_b"],
      p["w_ax"], p["b_a"], p["b_x"], p["lam"], p["w_a_up"], p["w_b_up"], p["w_out"],
      s_in, h_in, c_in)


def kernel(x_prompt, x_sample, state_hgrn, state_lru, state_conv, ffn1_norm, ffn1_w_gate, ffn1_w_up, ffn1_w_down, mix_norm, w_in, hgrn_lower_bounds, hgrn_out_norm, conv_w, conv_b, lru_w_a, lru_b_a, lru_w_x, lru_b_x, lru_lambda, w_a_up, w_b_up, w_out, ffn2_norm, ffn2_w_gate, ffn2_w_up, ffn2_w_down, final_norm):
    depth = w_in.shape[0]
    assert depth == 1, "single-layer trunk"
    l = 0
    row = lambda v: v.reshape(1, -1).astype(F32)
    bf = lambda w: w.astype(BF16)
    p = {
        "mix_norm": row(mix_norm[l]),
        "w_in": bf(w_in[l]),
        "lb_raw": hgrn_lower_bounds.astype(F32),
        "o_norm": row(hgrn_out_norm[l]),
        "conv_w": conv_w[l].astype(F32),
        "conv_b": row(conv_b[l]),
        "w_ax": bf(jnp.concatenate([lru_w_a[l], lru_w_x[l]], axis=-1)),
        "b_a": row(lru_b_a[l]),
        "b_x": row(lru_b_x[l]),
        "lam": row(lru_lambda[l]),
        "w_a_up": bf(w_a_up[l]),
        "w_b_up": bf(w_b_up[l]),
        "w_out": bf(w_out[l]),
    }
    f1 = (row(ffn1_norm[l]), bf(ffn1_w_gate[l]), bf(ffn1_w_up[l]), bf(ffn1_w_down[l]))
    f2 = (row(ffn2_norm[l]), bf(ffn2_w_gate[l]), bf(ffn2_w_up[l]), bf(ffn2_w_down[l]))
    gfin = row(final_norm)

    bp, tp, _ = x_prompt.shape
    xp = _ffn(x_prompt.reshape(bp * tp, D_MODEL), *f1, gfin, final_norm=False)
    xp, s_p, h_p, c_p = _mix_prompt(xp.reshape(bp, tp, D_MODEL), p)
    y_p = _ffn(xp.reshape(bp * tp, D_MODEL), *f2, gfin, final_norm=True).reshape(bp, tp, D_MODEL)

    bs, ts, _ = x_sample.shape
    assert ts == 1
    xs = _ffn(x_sample.reshape(bs, D_MODEL), *f1, gfin, final_norm=False)
    xs, s_s, h_s, c_s = _mix_sample(xs, state_hgrn[l], state_lru[l], state_conv[l], p)
    y_s = _ffn(xs, *f2, gfin, final_norm=True).reshape(bs, ts, D_MODEL)

    return (y_p, y_s, s_p[None], h_p.reshape(bp, D_MODEL)[None], c_p[None],
            s_s[None], h_s[None], c_s[None])
```

```python
import functools

import jax
import jax.numpy as jnp
from jax import lax
from jax.experimental import pallas as pl
from jax.experimental.pallas import tpu as pltpu

D_MODEL = 1024
HEADS = 8
DK = 128
DV = 128
CHUNK = 64
LRU_BLOCKS = 8
LRU_BW = 128
LRU_C = 8.0
CONV_W = 4
D_FF = 2816
EPS = 1e-6
LOG2_E = 1.4426950408889634
N_SPLITS = 8

SUBLANES = 8
LANES = 128
MXU_COLS = 256
VMEM_LIMIT_BYTES = 56 * 1024 * 1024

FFN_ROWS = 512
MIX_ROWS = 256
SAMPLE_STATE_BLOCK = 8

BF16 = jnp.bfloat16
F32 = jnp.float32


def _dot(a, b):
    return jnp.dot(a, b, preferred_element_type=F32)


def _dot_tn(a, b):
    return lax.dot_general(a, b, (((0,), (0,)), ((), ())), preferred_element_type=F32)


def _dot_nt(a, b):
    return lax.dot_general(a, b, (((1,), (1,)), ((), ())), preferred_element_type=F32)


def _sigmoid(x):
    return 1.0 / (1.0 + jnp.exp2(x * (-LOG2_E)))


def _silu(x):
    return x * _sigmoid(x)


def _gelu_tanh(x):
    c = 0.7978845608028654
    return 0.5 * x * (1.0 + jnp.tanh(c * (x + 0.044715 * (x * x * x))))


def _rms(x, g):
    return x * lax.rsqrt(jnp.mean(x * x, axis=-1, keepdims=True) + EPS) * g


def _lower_bound(raw, layer):
    m = jnp.max(raw, axis=0, keepdims=True)
    e = jnp.exp(raw - m)
    den = jnp.sum(e, axis=0, keepdims=True)
    num = jnp.sum(e[0:layer + 1], axis=0, keepdims=True)
    return num / den


def _softplus(x):
    return jnp.maximum(x, 0.0) + jnp.log1p(jnp.exp(-jnp.abs(x)))


def _lru_coeffs(r_pre, i_pre, xc, lam):
    r = _sigmoid(r_pre)
    ig = _sigmoid(i_pre)
    log_a = (-LRU_C) * r * _softplus(-lam)
    a = jnp.exp(log_a)
    th = jnp.tanh(log_a)
    mult = jnp.sqrt(-2.0 * th / (1.0 - th))
    return a, mult * (ig * xc)


def _split3(x):
    p1 = x.astype(BF16)
    r1 = x - p1.astype(F32)
    p2 = r1.astype(BF16)
    r2 = r1 - p2.astype(F32)
    return p1, p2, r2.astype(BF16)


def _ffn_kernel(x_ref, g_ref, wg_ref, wu_ref, wd_ref, gf_ref, o_ref, *, final_norm):
    x = x_ref[...]
    h = _rms(x, g_ref[...]).astype(BF16)
    gate = _dot(h, wg_ref[...])
    up = _dot(h, wu_ref[...])
    act = (_silu(gate) * up).astype(BF16)
    y = x + 0.5 * _dot(act, wd_ref[...])
    if final_norm:
        y = _rms(y, gf_ref[...])
    o_ref[...] = y


def _resident(shape):
    nd = len(shape)
    return pl.BlockSpec(shape, lambda *_: (0,) * nd, pipeline_mode=pl.Buffered(1))


def _ffn(x2d, g, wg, wu, wd, gf, *, final_norm):
    n = x2d.shape[0]
    rows = min(FFN_ROWS, n)
    assert n % rows == 0
    return pl.pallas_call(
        functools.partial(_ffn_kernel, final_norm=final_norm),
        grid=(n // rows,),
        in_specs=[
            pl.BlockSpec((rows, D_MODEL), lambda i: (i, 0)),
            _resident((1, D_MODEL)),
            _resident((D_MODEL, D_FF)),
            _resident((D_MODEL, D_FF)),
            _resident((D_FF, D_MODEL)),
            _resident((1, D_MODEL)),
        ],
        out_specs=pl.BlockSpec((rows, D_MODEL), lambda i: (i, 0)),
        out_shape=jax.ShapeDtypeStruct((n, D_MODEL), F32),
        name="ffn_final" if final_norm else "ffn",
        compiler_params=pltpu.CompilerParams(
            dimension_semantics=("arbitrary",), vmem_limit_bytes=VMEM_LIMIT_BYTES),
    )(x2d, g, wg, wu, wd, gf)


SCAN_SEG = 4
SCAN_GROUP = SCAN_SEG * SUBLANES
CONV_PAD = SUBLANES
PIECE_ROWS = 2 * SUBLANES
QA_W = 2 * LANES


def _mix_prompt_kernel(x_ref, nmix_ref, win_ref, lbraw_ref, onorm_ref, convw_ref, convb_ref,
                       wax_ref, ba_ref, bx_ref, lam_ref, waup_ref, wbup_ref, wout_ref,
                       xo_ref, s_ref, hl_ref, cv_ref,
                       qa_s, ud_s, vb_s, oa_s, xr_s, h_s):
    rows = MIX_ROWS
    t = pl.program_id(1)

    @pl.when(t == 0)
    def _():
        s_ref[...] = jnp.zeros_like(s_ref)
        hl_ref[...] = jnp.zeros_like(hl_ref)
        xr_s[:, 0:CONV_PAD, :] = jnp.zeros((LRU_BLOCKS, CONV_PAD, LRU_BW), F32)

    x = x_ref[0]
    hb = _rms(x, nmix_ref[...]).astype(BF16)

    def proj(i):
        return _dot(hb, win_ref[:, i * D_MODEL:(i + 1) * D_MODEL])

    early = {}

    ncb = D_MODEL // MXU_COLS

    def issue_proj(i, cb):
        if cb < ncb and (i, cb) not in early:
            c0 = i * D_MODEL + cb * MXU_COLS
            early[(i, cb)] = _dot(hb, win_ref[:, c0:c0 + MXU_COLS])

    def gather(i):
        for cb in range(ncb):
            issue_proj(i, cb)
        return jnp.concatenate([early[(i, cb)] for cb in range(ncb)], axis=1)

    lb = _lower_bound(lbraw_ref[...], 0)
    q_act = _silu(proj(0))
    f = lb + (1.0 - lb) * _sigmoid(proj(1))
    kk = 1.0 - f
    vb_s[...] = proj(2).astype(BF16)

    ri = lax.broadcasted_iota(jnp.int32, (rows, rows), 0)
    ci = lax.broadcasted_iota(jnp.int32, (rows, rows), 1)
    same_chunk = (ri // CHUNK) == (ci // CHUNK)
    tril_blk = jnp.where((ri >= ci) & same_chunk, 1.0, 0.0).astype(BF16)
    g1, g2, g3 = _split3(jnp.log(f))
    b_all = _dot(tril_blk, g1) + _dot(tril_blk, g2) + _dot(tril_blk, g3)

    causal = (lax.broadcasted_iota(jnp.int32, (CHUNK, CHUNK), 0)
              >= lax.broadcasted_iota(jnp.int32, (CHUNK, CHUNK), 1))
    piece_row = lax.broadcasted_iota(jnp.int32, (PIECE_ROWS, D_MODEL), 0)
    ones_pv = jnp.ones((3 * PIECE_ROWS, DV), BF16)
    zeros_pv = jnp.zeros((3 * PIECE_ROWS, DV), BF16)
    zeros_cv = jnp.zeros((CHUNK, DV), BF16)

    nchunk = rows // CHUNK
    last = piece_row == PIECE_ROWS - 1
    zero_p = jnp.zeros((PIECE_ROWS, D_MODEL), BF16)

    for c in range(nchunk):
        issue_proj(3, c)
        rs = slice(c * CHUNK, (c + 1) * CHUNK)
        b = b_all[rs, :]
        bl = b[CHUNK - 1:CHUNK, :]
        qe = (q_act[rs, :] * jnp.exp(b)).astype(BF16)
        ke = (kk[rs, :] * jnp.exp(-b)).astype(BF16)
        kd = (kk[rs, :] * jnp.exp(bl - b)).astype(BF16)
        pieces = jnp.concatenate(
            [jnp.where(last, e, zero_p) for e in _split3(jnp.exp(b[CHUNK - PIECE_ROWS:, :]))], axis=0)
        for h in range(HEADS):
            hs = slice(h * DK, (h + 1) * DK)
            qa_s[rs, h * QA_W:h * QA_W + DK] = qe[:, hs]
            att = jnp.where(causal, _dot_nt(qe[:, hs], ke[:, hs]), 0.0)
            qa_s[rs, h * QA_W + DK:h * QA_W + DK + CHUNK] = att.astype(BF16)
            lhs = jnp.concatenate([kd[:, hs], pieces[:, hs]], axis=0)
            rhs = jnp.concatenate([jnp.concatenate([vb_s[rs, hs], zeros_cv], axis=1),
                                   jnp.concatenate([zeros_pv, ones_pv], axis=1)], axis=0)
            ud_s[c * HEADS + h] = _dot_tn(lhs, rhs)

    for c in range(nchunk):
        rs = slice(c * CHUNK, (c + 1) * CHUNK)
        for h in range(HEADS):
            hs = slice(h * DK, (h + 1) * DK)
            s_old = s_ref[0, h]
            rhs = jnp.concatenate([s_old.astype(BF16), vb_s[rs, hs]], axis=0)
            o = _dot(qa_s[rs, h * QA_W:h * QA_W + DK + CHUNK], rhs)
            oa_s[rs, hs] = o * lax.rsqrt(jnp.mean(o * o, axis=-1, keepdims=True) + EPS)
            ud = ud_s[c * HEADS + h]
            s_ref[0, h] = ud[:, DV:] * s_old + ud[:, 0:DV]

    o_a = (oa_s[...] * onorm_ref[...] * _silu(gather(3))).astype(BF16)
    up_a = {}

    def issue_up_a(cb):
        if cb < ncb and cb not in up_a:
            up_a[cb] = _dot(o_a, waup_ref[:, cb * MXU_COLS:(cb + 1) * MXU_COLS])

    xr = proj(4)
    h_prev = hl_ref[0]
    sub = lax.broadcasted_iota(jnp.int32, (SUBLANES, LANES), 0)
    ngroup = rows // SCAN_GROUP
    tails = []
    h_last = []
    for n in range(LRU_BLOCKS):
        ns = slice(n * LRU_BW, (n + 1) * LRU_BW)
        xr_s[n, CONV_PAD:CONV_PAD + rows, :] = xr[:, ns]
        taps = [jnp.broadcast_to(convw_ref[j:j + 1, ns], (SUBLANES, LRU_BW)) for j in range(CONV_W)]
        bias = jnp.broadcast_to(convb_ref[:, ns], (SUBLANES, LRU_BW))
        xc_parts = []
        for grp in range(ngroup):
            xin = {d: xr_s[n, pl.ds(CONV_PAD + grp * SCAN_GROUP + d, SUBLANES, stride=SCAN_SEG), :]
                   for d in range(1 - CONV_W, SCAN_SEG)}
            for j in range(SCAN_SEG):
                acc = bias
                for tap in range(CONV_W):
                    acc = acc + xin[j + tap - (CONV_W - 1)] * taps[tap]
                xc_parts.append(acc)
        xc = jnp.concatenate(xc_parts, axis=0)
        tail = xr_s[n, CONV_PAD + rows - (CONV_W - 1):CONV_PAD + rows, :]
        tails.append(tail)
        xr_s[n, CONV_PAD - (CONV_W - 1):CONV_PAD, :] = tail
        pre = _dot(xc.astype(BF16), wax_ref[n])
        issue_proj(5 + n % 2, n // 2)
        a, bterm = _lru_coeffs(pre[:, :LRU_BW] + ba_ref[:, ns], pre[:, LRU_BW:] + bx_ref[:, ns],
                               xc, lam_ref[:, ns])
        carry = h_prev[:, ns]
        for grp in range(ngroup):
            def vreg(v, j, grp=grp):
                r0 = grp * SCAN_GROUP + j * SUBLANES
                return v[r0:r0 + SUBLANES, :]

            def strided(j, grp=grp):
                return pl.ds(grp * SCAN_GROUP + j, SUBLANES, stride=SCAN_SEG)
            pa = [vreg(a, 0)]
            hh = [vreg(bterm, 0)]
            for j in range(1, SCAN_SEG):
                aj = vreg(a, j)
                hh.append(aj * hh[-1] + vreg(bterm, j))
                pa.append(aj * pa[-1])
            pi, hi = pa[-1], hh[-1]
            for d in (1, 2, 4):
                keep = sub >= d
                hi = pi * jnp.where(keep, pltpu.roll(hi, d, 0), 0.0) + hi
                pi = pi * jnp.where(keep, pltpu.roll(pi, d, 0), 1.0)
            first = sub >= 1
            seg_in = (jnp.where(first, pltpu.roll(pi, 1, 0), 1.0) * carry
                      + jnp.where(first, pltpu.roll(hi, 1, 0), 0.0))
            carry = pi[SUBLANES - 1:, :] * carry + hi[SUBLANES - 1:, :]
            for j in range(SCAN_SEG):
                h_s[n, strided(j), :] = pa[j] * seg_in + hh[j]
        h_last.append(carry)
        if n % 2 == 0:
            issue_proj(7, n // 2)
        else:
            issue_up_a(n // 2)
    hl_ref[0] = jnp.concatenate(h_last, axis=1)
    cv_ref[0] = jnp.concatenate(tails, axis=1)
    gel = _gelu_tanh(gather(5))
    o_b = jnp.concatenate(
        [h_s[n] * gel[:, n * LRU_BW:(n + 1) * LRU_BW] for n in range(LRU_BLOCKS)], axis=1).astype(BF16)

    for cb in range(ncb):
        issue_up_a(cb)
    m = (_sigmoid(gather(6)) * jnp.concatenate([up_a[cb] for cb in range(ncb)], axis=1)
         + _sigmoid(gather(7)) * _dot(o_b, wbup_ref[...]))
    xo_ref[0] = x + _dot(m.astype(BF16), wout_ref[...])


def _mix_prompt(x, p):
    batch, seq, _ = x.shape
    rows = MIX_ROWS
    assert seq % rows == 0 and rows % CHUNK == 0 and rows >= CONV_W - 1
    row_vec = _resident((1, D_MODEL))
    out_shape = (
        jax.ShapeDtypeStruct((batch, seq, D_MODEL), F32),
        jax.ShapeDtypeStruct((batch, HEADS, DK, DV), F32),
        jax.ShapeDtypeStruct((batch, 1, D_MODEL), F32),
        jax.ShapeDtypeStruct((batch, CONV_W - 1, D_MODEL), F32),
    )
    return pl.pallas_call(
        _mix_prompt_kernel,
        grid=(batch, seq // rows),
        in_specs=[
            pl.BlockSpec((1, rows, D_MODEL), lambda b, t: (b, t, 0)),
            row_vec,
            _resident((D_MODEL, N_SPLITS * D_MODEL)),
            _resident(p["lb_raw"].shape),
            row_vec,
            _resident((CONV_W, D_MODEL)),
            row_vec,
            _resident((LRU_BLOCKS, LRU_BW, 2 * LRU_BW)),
            row_vec, row_vec, row_vec,
            _resident((D_MODEL, D_MODEL)),
            _resident((D_MODEL, D_MODEL)),
            _resident((D_MODEL, D_MODEL)),
        ],
        out_specs=(
            pl.BlockSpec((1, rows, D_MODEL), lambda b, t: (b, t, 0)),
            pl.BlockSpec((1, HEADS, DK, DV), lambda b, t: (b, 0, 0, 0)),
            pl.BlockSpec((1, 1, D_MODEL), lambda b, t: (b, 0, 0)),
            pl.BlockSpec((1, CONV_W - 1, D_MODEL), lambda b, t: (b, 0, 0)),
        ),
        out_shape=out_shape,
        scratch_shapes=[
            pltpu.VMEM((rows, HEADS * QA_W), BF16),
            pltpu.VMEM((rows // CHUNK * HEADS, DK, 2 * DV), F32),
            pltpu.VMEM((rows, D_MODEL), BF16),
            pltpu.VMEM((rows, D_MODEL), F32),
            pltpu.VMEM((LRU_BLOCKS, CONV_PAD + rows, LRU_BW), F32),
            pltpu.VMEM((LRU_BLOCKS, rows, LRU_BW), F32),
        ],
        name="mix_prompt",
        compiler_params=pltpu.CompilerParams(
            dimension_semantics=("arbitrary", "arbitrary"), vmem_limit_bytes=VMEM_LIMIT_BYTES),
    )(x, p["mix_norm"], p["w_in"], p["lb_raw"], p["o_norm"], p["conv_w"], p["conv_b"],
      p["w_ax"], p["b_a"], p["b_x"], p["lam"], p["w_a_up"], p["w_b_up"], p["w_out"])


def _mix_sample_kernel(x_ref, nmix_ref, win_ref, lbraw_ref, onorm_ref, convw_ref, convb_ref,
                       wax_ref, ba_ref, bx_ref, lam_ref, waup_ref, wbup_ref, wout_ref,
                       sin_ref, hlin_ref, cvin_ref,
                       xo_ref, sout_ref, hl_ref, cv_ref,
                       qt_s, ft_s, v_s, oa_s, gsil_s, ga_s, mb_s):
    nseq = x_ref.shape[0]
    blk = SAMPLE_STATE_BLOCK
    j = pl.program_id(0)

    @pl.when(j == 0)
    def _():
        x = x_ref[...]
        hb = _rms(x, nmix_ref[...]).astype(BF16)

        def proj(i):
            return _dot(hb, win_ref[:, i * D_MODEL:(i + 1) * D_MODEL])

        lb = _lower_bound(lbraw_ref[...], 0)
        qt_s[...] = _silu(proj(0)).T
        ft_s[...] = (lb + (1.0 - lb) * _sigmoid(proj(1))).T
        v_s[...] = proj(2)
        gsil_s[...] = onorm_ref[...] * _silu(proj(3))

        xr = proj(4)
        xc = convb_ref[...] + xr * convw_ref[CONV_W - 1:CONV_W, :]
        for i in range(CONV_W - 1):
            xc = xc + cvin_ref[:, i, :] * convw_ref[i:i + 1, :]
        for i in range(CONV_W - 2):
            cv_ref[:, i, :] = cvin_ref[:, i + 1, :]
        cv_ref[:, CONV_W - 2, :] = xr

        xcb = xc.astype(BF16)
        h_parts = []
        for n in range(LRU_BLOCKS):
            ns = slice(n * LRU_BW, (n + 1) * LRU_BW)
            pre = _dot(xcb[:, ns], wax_ref[n])
            a, bterm = _lru_coeffs(pre[:, :LRU_BW] + ba_ref[:, ns], pre[:, LRU_BW:] + bx_ref[:, ns],
                                   xc[:, ns], lam_ref[:, ns])
            h_parts.append(a * hlin_ref[:, ns] + bterm)
        h_new = jnp.concatenate(h_parts, axis=1)
        hl_ref[...] = h_new
        o_b = (h_new * _gelu_tanh(proj(5))).astype(BF16)
        ga_s[...] = _sigmoid(proj(6))
        mb_s[...] = _sigmoid(proj(7)) * _dot(o_b, wbup_ref[...])

    b0 = pl.multiple_of(j * blk, blk)
    shifts = [lax.rem(nseq - b0 + p * blk, nseq) for p in range(3)]

    lane = lax.broadcasted_iota(jnp.int32, (DK, nseq), 1)
    sel_r = lax.broadcasted_iota(jnp.int32, (nseq, blk * DV), 0)
    sel_c = lax.broadcasted_iota(jnp.int32, (nseq, blk * DV), 1)
    selector = jnp.where((sel_r < 3 * blk) & (jnp.bitwise_and(sel_r, blk - 1) == sel_c // DV),
                         1.0, 0.0).astype(BF16)

    def spread(cols_t):
        p1 = cols_t.astype(BF16).astype(F32)
        r1 = cols_t - p1
        p2 = r1.astype(BF16).astype(F32)
        p3 = r1 - p2
        lhs = jnp.where(lane < blk, pltpu.roll(p1, shifts[0], 1),
                        jnp.where(lane < 2 * blk, pltpu.roll(p2, shifts[1], 1),
                                  jnp.where(lane < 3 * blk, pltpu.roll(p3, shifts[2], 1), 0.0)))
        return _dot(lhs.astype(BF16), selector)

    for h in range(HEADS):
        hrows = slice(h * DK, (h + 1) * DK)
        q_all = spread(qt_s[hrows, :])
        f_all = spread(ft_s[hrows, :])
        v_rows = v_s[pl.ds(b0, blk), hrows]
        o_rows = []
        for i in range(blk):
            f_col = f_all[:, i * DV:(i + 1) * DV]
            s_new = f_col * sin_ref[i, h] + (1.0 - f_col) * v_rows[i:i + 1, :]
            sout_ref[i, h] = s_new
            o_rows.append(jnp.sum(q_all[:, i * DV:(i + 1) * DV] * s_new, axis=0, keepdims=True))
        o = jnp.concatenate(o_rows, axis=0)
        oa_s[pl.ds(b0, blk), hrows] = o * lax.rsqrt(
            jnp.mean(o * o, axis=-1, keepdims=True) + EPS)

    @pl.when(j == pl.num_programs(0) - 1)
    def _():
        o_a = (oa_s[...] * gsil_s[...]).astype(BF16)
        m = ga_s[...] * _dot(o_a, waup_ref[...]) + mb_s[...]
        xo_ref[...] = x_ref[...] + _dot(m.astype(BF16), wout_ref[...])


def _mix_sample(x2d, s_in, h_in, c_in, p):
    nseq = x2d.shape[0]
    blk = SAMPLE_STATE_BLOCK
    assert nseq == LANES and nseq % blk == 0
    row_vec = _resident((1, D_MODEL))
    tok = _resident((nseq, D_MODEL))
    conv_state = _resident((nseq, CONV_W - 1, D_MODEL))
    out_shape = (
        jax.ShapeDtypeStruct((nseq, D_MODEL), F32),
        jax.ShapeDtypeStruct((nseq, HEADS, DK, DV), F32),
        jax.ShapeDtypeStruct((nseq, D_MODEL), F32),
        jax.ShapeDtypeStruct((nseq, CONV_W - 1, D_MODEL), F32),
    )
    state_spec = pl.BlockSpec((blk, HEADS, DK, DV), lambda j: (j, 0, 0, 0))
    return pl.pallas_call(
        _mix_sample_kernel,
        grid=(nseq // blk,),
        in_specs=[
            tok, row_vec,
            _resident((D_MODEL, N_SPLITS * D_MODEL)),
            _resident(p["lb_raw"].shape),
            row_vec,
            _resident((CONV_W, D_MODEL)),
            row_vec,
            _resident((LRU_BLOCKS, LRU_BW, 2 * LRU_BW)),
            row_vec, row_vec, row_vec,
            _resident((D_MODEL, D_MODEL)),
            _resident((D_MODEL, D_MODEL)),
            _resident((D_MODEL, D_MODEL)),
            state_spec, tok, conv_state,
        ],
        out_specs=(
            pl.BlockSpec((nseq, D_MODEL), lambda j: (0, 0)),
            state_spec,
            pl.BlockSpec((nseq, D_MODEL), lambda j: (0, 0)),
            pl.BlockSpec((nseq, CONV_W - 1, D_MODEL), lambda j: (0, 0, 0)),
        ),
        out_shape=out_shape,
        scratch_shapes=[
            pltpu.VMEM((D_MODEL, nseq), F32),
            pltpu.VMEM((D_MODEL, nseq), F32),
            pltpu.VMEM((nseq, D_MODEL), F32),
            pltpu.VMEM((nseq, D_MODEL), F32),
            pltpu.VMEM((nseq, D_MODEL), F32),
            pltpu.VMEM((nseq, D_MODEL), F32),
            pltpu.VMEM((nseq, D_MODEL), F32),
        ],
        name="mix_sample",
        compiler_params=pltpu.CompilerParams(
            dimension_semantics=("arbitrary",), vmem_limit_bytes=VMEM_LIMIT_BYTES),
    )(x2d, p["mix_norm"], p["w_in"], p["lb_raw"], p["o_norm"], p["conv_w"], p["conv_b"],
      p["w_ax"], p["b_a"], p["b_x"], p["lam"], p["w_a_up"], p["w_b_up"], p["w_out"],
      s_in, h_in, c_in)


def kernel(x_prompt, x_sample, state_hgrn, state_lru, state_conv, ffn1_norm, ffn1_w_gate, ffn1_w_up, ffn1_w_down, mix_norm, w_in, hgrn_lower_bounds, hgrn_out_norm, conv_w, conv_b, lru_w_a, lru_b_a, lru_w_x, lru_b_x, lru_lambda, w_a_up, w_b_up, w_out, ffn2_norm, ffn2_w_gate, ffn2_w_up, ffn2_w_down, final_norm):
    depth = w_in.shape[0]
    assert depth == 1, "single-layer trunk"
    l = 0
    row = lambda v: v.reshape(1, -1).astype(F32)
    bf = lambda w: w.astype(BF16)
    p = {
        "mix_norm": row(mix_norm[l]),
        "w_in": bf(w_in[l]),
        "lb_raw": hgrn_lower_bounds.astype(F32),
        "o_norm": row(hgrn_out_norm[l]),
        "conv_w": conv_w[l].astype(F32),
        "conv_b": row(conv_b[l]),
        "w_ax": bf(jnp.concatenate([lru_w_a[l], lru_w_x[l]], axis=-1)),
        "b_a": row(lru_b_a[l]),
        "b_x": row(lru_b_x[l]),
        "lam": row(lru_lambda[l]),
        "w_a_up": bf(w_a_up[l]),
        "w_b_up": bf(w_b_up[l]),
        "w_out": bf(w_out[l]),
    }
    f1 = (row(ffn1_norm[l]), bf(ffn1_w_gate[l]), bf(ffn1_w_up[l]), bf(ffn1_w_down[l]))
    f2 = (row(ffn2_norm[l]), bf(ffn2_w_gate[l]), bf(ffn2_w_up[l]), bf(ffn2_w_down[l]))
    gfin = row(final_norm)

    bp, tp, _ = x_prompt.shape
    xp = _ffn(x_prompt.reshape(bp * tp, D_MODEL), *f1, gfin, final_norm=False)
    xp, s_p, h_p, c_p = _mix_prompt(xp.reshape(bp, tp, D_MODEL), p)
    y_p = _ffn(xp.reshape(bp * tp, D_MODEL), *f2, gfin, final_norm=True).reshape(bp, tp, D_MODEL)

    bs, ts, _ = x_sample.shape
    assert ts == 1
    xs = _ffn(x_sample.reshape(bs, D_MODEL), *f1, gfin, final_norm=False)
    xs, s_s, h_s, c_s = _mix_sample(xs, state_hgrn[l], state_lru[l], state_conv[l], p)
    y_s = _ffn(xs, *f2, gfin, final_norm=True).reshape(bs, ts, D_MODEL)

    return (y_p, y_s, s_p[None], h_p.reshape(bp, D_MODEL)[None], c_p[None],
            s_s[None], h_s[None], c_s[None])
```

```python
import functools

import jax
import jax.numpy as jnp
from jax import lax
from jax.experimental import pallas as pl
from jax.experimental.pallas import tpu as pltpu

D_MODEL = 1024
HEADS = 8
DK = 128
DV = 128
CHUNK = 64
LRU_BLOCKS = 8
LRU_BW = 128
LRU_C = 8.0
CONV_W = 4
D_FF = 2816
EPS = 1e-6
LOG2_E = 1.4426950408889634
N_SPLITS = 8

SUBLANES = 8
LANES = 128
MXU_COLS = 256
VMEM_LIMIT_BYTES = 56 * 1024 * 1024

FFN_ROWS = 512
MIX_ROWS = 256
SAMPLE_STATE_BLOCK = 8

BF16 = jnp.bfloat16
F32 = jnp.float32


def _dot(a, b):
    return jnp.dot(a, b, preferred_element_type=F32)


def _dot_tn(a, b):
    return lax.dot_general(a, b, (((0,), (0,)), ((), ())), preferred_element_type=F32)


def _dot_nt(a, b):
    return lax.dot_general(a, b, (((1,), (1,)), ((), ())), preferred_element_type=F32)


def _sigmoid(x):
    return 1.0 / (1.0 + jnp.exp2(x * (-LOG2_E)))


def _silu(x):
    return x * _sigmoid(x)


def _gelu_tanh(x):
    c = 0.7978845608028654
    return 0.5 * x * (1.0 + jnp.tanh(c * (x + 0.044715 * (x * x * x))))


def _rms(x, g):
    return x * lax.rsqrt(jnp.mean(x * x, axis=-1, keepdims=True) + EPS) * g


def _lower_bound(raw, layer):
    m = jnp.max(raw, axis=0, keepdims=True)
    e = jnp.exp(raw - m)
    den = jnp.sum(e, axis=0, keepdims=True)
    num = jnp.sum(e[0:layer + 1], axis=0, keepdims=True)
    return num / den


def _softplus(x):
    return jnp.maximum(x, 0.0) + jnp.log1p(jnp.exp(-jnp.abs(x)))


def _lru_coeffs(r_pre, i_pre, xc, lam):
    r = _sigmoid(r_pre)
    ig = _sigmoid(i_pre)
    log_a = (-LRU_C) * r * _softplus(-lam)
    a = jnp.exp(log_a)
    th = jnp.tanh(log_a)
    mult = jnp.sqrt(-2.0 * th / (1.0 - th))
    return a, mult * (ig * xc)


def _split3(x):
    p1 = x.astype(BF16)
    r1 = x - p1.astype(F32)
    p2 = r1.astype(BF16)
    r2 = r1 - p2.astype(F32)
    return p1, p2, r2.astype(BF16)


def _ffn_kernel(x_ref, g_ref, wg_ref, wu_ref, wd_ref, gf_ref, o_ref, *, final_norm):
    x = x_ref[...]
    h = _rms(x, g_ref[...]).astype(BF16)
    gate = _dot(h, wg_ref[...])
    up = _dot(h, wu_ref[...])
    act = (_silu(gate) * up).astype(BF16)
    y = x + 0.5 * _dot(act, wd_ref[...])
    if final_norm:
        y = _rms(y, gf_ref[...])
    o_ref[...] = y


def _resident(shape):
    nd = len(shape)
    return pl.BlockSpec(shape, lambda *_: (0,) * nd, pipeline_mode=pl.Buffered(1))


def _ffn(x2d, g, wg, wu, wd, gf, *, final_norm):
    n = x2d.shape[0]
    rows = min(FFN_ROWS, n)
    assert n % rows == 0
    return pl.pallas_call(
        functools.partial(_ffn_kernel, final_norm=final_norm),
        grid=(n // rows,),
        in_specs=[
            pl.BlockSpec((rows, D_MODEL), lambda i: (i, 0)),
            _resident((1, D_MODEL)),
            _resident((D_MODEL, D_FF)),
            _resident((D_MODEL, D_FF)),
            _resident((D_FF, D_MODEL)),
            _resident((1, D_MODEL)),
        ],
        out_specs=pl.BlockSpec((rows, D_MODEL), lambda i: (i, 0)),
        out_shape=jax.ShapeDtypeStruct((n, D_MODEL), F32),
        name="ffn_final" if final_norm else "ffn",
        compiler_params=pltpu.CompilerParams(
            dimension_semantics=("arbitrary",), vmem_limit_bytes=VMEM_LIMIT_BYTES),
    )(x2d, g, wg, wu, wd, gf)


SCAN_SEG = 4
SCAN_GROUP = SCAN_SEG * SUBLANES
CONV_PAD = SUBLANES
PIECE_ROWS = 2 * SUBLANES
QA_W = 2 * LANES


def _mix_prompt_kernel(x_ref, nmix_ref, win_ref, lbraw_ref, onorm_ref, convw_ref, convb_ref,
                       wax_ref, ba_ref, bx_ref, lam_ref, waup_ref, wbup_ref, wout_ref,
                       xo_ref, s_ref, hl_ref, cv_ref,
                       qa_s, ud_s, vb_s, oa_s, xr_s, h_s, g_s, b_s):
    rows = MIX_ROWS
    t = pl.program_id(1)

    @pl.when(t == 0)
    def _():
        s_ref[...] = jnp.zeros_like(s_ref)
        hl_ref[...] = jnp.zeros_like(hl_ref)
        xr_s[:, 0:CONV_PAD, :] = jnp.zeros((LRU_BLOCKS, CONV_PAD, LRU_BW), F32)

    x = x_ref[0]
    hb = _rms(x, nmix_ref[...]).astype(BF16)

    def proj(i):
        return _dot(hb, win_ref[:, i * D_MODEL:(i + 1) * D_MODEL])

    early = {}

    ncb = D_MODEL // MXU_COLS

    def issue_proj(i, cb):
        if cb < ncb and (i, cb) not in early:
            c0 = i * D_MODEL + cb * MXU_COLS
            early[(i, cb)] = _dot(hb, win_ref[:, c0:c0 + MXU_COLS])

    def gather(i):
        for cb in range(ncb):
            issue_proj(i, cb)
        return jnp.concatenate([early[(i, cb)] for cb in range(ncb)], axis=1)

    lb = _lower_bound(lbraw_ref[...], 0)
    q_act = _silu(proj(0))
    f = lb + (1.0 - lb) * _sigmoid(proj(1))
    kk = 1.0 - f
    vb_s[...] = proj(2).astype(BF16)

    logf = jnp.log(f)
    sub = lax.broadcasted_iota(jnp.int32, (SUBLANES, LANES), 0)
    for n in range(HEADS):
        g_s[n] = logf[:, n * DK:(n + 1) * DK]
        for grp in range(rows // SCAN_GROUP):
            def strided(j, grp=grp):
                return pl.ds(grp * SCAN_GROUP + j, SUBLANES, stride=SCAN_SEG)
            if grp % (CHUNK // SCAN_GROUP) == 0:
                run = jnp.zeros((1, DK), F32)
            part = [g_s[n, strided(0), :]]
            for j in range(1, SCAN_SEG):
                part.append(part[-1] + g_s[n, strided(j), :])
            tot = part[-1]
            for d in (1, 2, 4):
                tot = tot + jnp.where(sub >= d, pltpu.roll(tot, d, 0), 0.0)
            before = jnp.where(sub >= 1, pltpu.roll(tot, 1, 0), 0.0) + run
            run = run + tot[SUBLANES - 1:, :]
            for j in range(SCAN_SEG):
                b_s[n, strided(j), :] = part[j] + before
    b_all = jnp.concatenate([b_s[n] for n in range(HEADS)], axis=1)

    causal = (lax.broadcasted_iota(jnp.int32, (CHUNK, CHUNK), 0)
              >= lax.broadcasted_iota(jnp.int32, (CHUNK, CHUNK), 1))
    piece_row = lax.broadcasted_iota(jnp.int32, (PIECE_ROWS, D_MODEL), 0)
    ones_pv = jnp.ones((3 * PIECE_ROWS, DV), BF16)
    zeros_pv = jnp.zeros((3 * PIECE_ROWS, DV), BF16)
    zeros_cv = jnp.zeros((CHUNK, DV), BF16)

    nchunk = rows // CHUNK
    last = piece_row == PIECE_ROWS - 1
    zero_p = jnp.zeros((PIECE_ROWS, D_MODEL), BF16)

    for c in range(nchunk):
        issue_proj(3, c)
        rs = slice(c * CHUNK, (c + 1) * CHUNK)
        b = b_all[rs, :]
        bl = b[CHUNK - 1:CHUNK, :]
        qe = (q_act[rs, :] * jnp.exp(b)).astype(BF16)
        ke = (kk[rs, :] * jnp.exp(-b)).astype(BF16)
        kd = (kk[rs, :] * jnp.exp(bl - b)).astype(BF16)
        pieces = jnp.concatenate(
            [jnp.where(last, e, zero_p) for e in _split3(jnp.exp(b[CHUNK - PIECE_ROWS:, :]))], axis=0)
        for h in range(HEADS):
            hs = slice(h * DK, (h + 1) * DK)
            qa_s[rs, h * QA_W:h * QA_W + DK] = qe[:, hs]
            att = jnp.where(causal, _dot_nt(qe[:, hs], ke[:, hs]), 0.0)
            qa_s[rs, h * QA_W + DK:h * QA_W + DK + CHUNK] = att.astype(BF16)
            lhs = jnp.concatenate([kd[:, hs], pieces[:, hs]], axis=0)
            rhs = jnp.concatenate([jnp.concatenate([vb_s[rs, hs], zeros_cv], axis=1),
                                   jnp.concatenate([zeros_pv, ones_pv], axis=1)], axis=0)
            ud_s[c * HEADS + h] = _dot_tn(lhs, rhs)

    for c in range(nchunk):
        rs = slice(c * CHUNK, (c + 1) * CHUNK)
        for h in range(HEADS):
            hs = slice(h * DK, (h + 1) * DK)
            s_old = s_ref[0, h]
            rhs = jnp.concatenate([s_old.astype(BF16), vb_s[rs, hs]], axis=0)
            o = _dot(qa_s[rs, h * QA_W:h * QA_W + DK + CHUNK], rhs)
            oa_s[rs, hs] = o * lax.rsqrt(jnp.mean(o * o, axis=-1, keepdims=True) + EPS)
            ud = ud_s[c * HEADS + h]
            s_ref[0, h] = ud[:, DV:] * s_old + ud[:, 0:DV]

    o_a = (oa_s[...] * onorm_ref[...] * _silu(gather(3))).astype(BF16)
    up_a = {}

    def issue_up_a(cb):
        if cb < ncb and cb not in up_a:
            up_a[cb] = _dot(o_a, waup_ref[:, cb * MXU_COLS:(cb + 1) * MXU_COLS])

    xr = proj(4)
    h_prev = hl_ref[0]
    sub = lax.broadcasted_iota(jnp.int32, (SUBLANES, LANES), 0)
    ngroup = rows // SCAN_GROUP
    tails = []
    h_last = []
    for n in range(LRU_BLOCKS):
        ns = slice(n * LRU_BW, (n + 1) * LRU_BW)
        xr_s[n, CONV_PAD:CONV_PAD + rows, :] = xr[:, ns]
        taps = [jnp.broadcast_to(convw_ref[j:j + 1, ns], (SUBLANES, LRU_BW)) for j in range(CONV_W)]
        bias = jnp.broadcast_to(convb_ref[:, ns], (SUBLANES, LRU_BW))
        xc_parts = []
        for grp in range(ngroup):
            xin = {d: xr_s[n, pl.ds(CONV_PAD + grp * SCAN_GROUP + d, SUBLANES, stride=SCAN_SEG), :]
                   for d in range(1 - CONV_W, SCAN_SEG)}
            for j in range(SCAN_SEG):
                acc = bias
                for tap in range(CONV_W):
                    acc = acc + xin[j + tap - (CONV_W - 1)] * taps[tap]
                xc_parts.append(acc)
        xc = jnp.concatenate(xc_parts, axis=0)
        tail = xr_s[n, CONV_PAD + rows - (CONV_W - 1):CONV_PAD + rows, :]
        tails.append(tail)
        xr_s[n, CONV_PAD - (CONV_W - 1):CONV_PAD, :] = tail
        pre = _dot(xc.astype(BF16), wax_ref[n])
        issue_proj(5 + n % 2, n // 2)
        a, bterm = _lru_coeffs(pre[:, :LRU_BW] + ba_ref[:, ns], pre[:, LRU_BW:] + bx_ref[:, ns],
                               xc, lam_ref[:, ns])
        carry = h_prev[:, ns]
        for grp in range(ngroup):
            def vreg(v, j, grp=grp):
                r0 = grp * SCAN_GROUP + j * SUBLANES
                return v[r0:r0 + SUBLANES, :]

            def strided(j, grp=grp):
                return pl.ds(grp * SCAN_GROUP + j, SUBLANES, stride=SCAN_SEG)
            pa = [vreg(a, 0)]
            hh = [vreg(bterm, 0)]
            for j in range(1, SCAN_SEG):
                aj = vreg(a, j)
                hh.append(aj * hh[-1] + vreg(bterm, j))
                pa.append(aj * pa[-1])
            pi, hi = pa[-1], hh[-1]
            for d in (1, 2, 4):
                keep = sub >= d
                hi = pi * jnp.where(keep, pltpu.roll(hi, d, 0), 0.0) + hi
                pi = pi * jnp.where(keep, pltpu.roll(pi, d, 0), 1.0)
            first = sub >= 1
            seg_in = (jnp.where(first, pltpu.roll(pi, 1, 0), 1.0) * carry
                      + jnp.where(first, pltpu.roll(hi, 1, 0), 0.0))
            carry = pi[SUBLANES - 1:, :] * carry + hi[SUBLANES - 1:, :]
            for j in range(SCAN_SEG):
                h_s[n, strided(j), :] = pa[j] * seg_in + hh[j]
        h_last.append(carry)
        if n % 2 == 0:
            issue_proj(7, n // 2)
        else:
            issue_up_a(n // 2)
    hl_ref[0] = jnp.concatenate(h_last, axis=1)
    cv_ref[0] = jnp.concatenate(tails, axis=1)
    gel = _gelu_tanh(gather(5))
    o_b = jnp.concatenate(
        [h_s[n] * gel[:, n * LRU_BW:(n + 1) * LRU_BW] for n in range(LRU_BLOCKS)], axis=1).astype(BF16)

    for cb in range(ncb):
        issue_up_a(cb)
    m = (_sigmoid(gather(6)) * jnp.concatenate([up_a[cb] for cb in range(ncb)], axis=1)
         + _sigmoid(gather(7)) * _dot(o_b, wbup_ref[...]))
    xo_ref[0] = x + _dot(m.astype(BF16), wout_ref[...])


def _mix_prompt(x, p):
    batch, seq, _ = x.shape
    rows = MIX_ROWS
    assert seq % rows == 0 and rows % CHUNK == 0 and rows >= CONV_W - 1
    row_vec = _resident((1, D_MODEL))
    out_shape = (
        jax.ShapeDtypeStruct((batch, seq, D_MODEL), F32),
        jax.ShapeDtypeStruct((batch, HEADS, DK, DV), F32),
        jax.ShapeDtypeStruct((batch, 1, D_MODEL), F32),
        jax.ShapeDtypeStruct((batch, CONV_W - 1, D_MODEL), F32),
    )
    return pl.pallas_call(
        _mix_prompt_kernel,
        grid=(batch, seq // rows),
        in_specs=[
            pl.BlockSpec((1, rows, D_MODEL), lambda b, t: (b, t, 0)),
            row_vec,
            _resident((D_MODEL, N_SPLITS * D_MODEL)),
            _resident(p["lb_raw"].shape),
            row_vec,
            _resident((CONV_W, D_MODEL)),
            row_vec,
            _resident((LRU_BLOCKS, LRU_BW, 2 * LRU_BW)),
            row_vec, row_vec, row_vec,
            _resident((D_MODEL, D_MODEL)),
            _resident((D_MODEL, D_MODEL)),
            _resident((D_MODEL, D_MODEL)),
        ],
        out_specs=(
            pl.BlockSpec((1, rows, D_MODEL), lambda b, t: (b, t, 0)),
            pl.BlockSpec((1, HEADS, DK, DV), lambda b, t: (b, 0, 0, 0)),
            pl.BlockSpec((1, 1, D_MODEL), lambda b, t: (b, 0, 0)),
            pl.BlockSpec((1, CONV_W - 1, D_MODEL), lambda b, t: (b, 0, 0)),
        ),
        out_shape=out_shape,
        scratch_shapes=[
            pltpu.VMEM((rows, HEADS * QA_W), BF16),
            pltpu.VMEM((rows // CHUNK * HEADS, DK, 2 * DV), F32),
            pltpu.VMEM((rows, D_MODEL), BF16),
            pltpu.VMEM((rows, D_MODEL), F32),
            pltpu.VMEM((LRU_BLOCKS, CONV_PAD + rows, LRU_BW), F32),
            pltpu.VMEM((LRU_BLOCKS, rows, LRU_BW), F32),
            pltpu.VMEM((HEADS, rows, DK), F32),
            pltpu.VMEM((HEADS, rows, DK), F32),
        ],
        name="mix_prompt",
        compiler_params=pltpu.CompilerParams(
            dimension_semantics=("arbitrary", "arbitrary"), vmem_limit_bytes=VMEM_LIMIT_BYTES),
    )(x, p["mix_norm"], p["w_in"], p["lb_raw"], p["o_norm"], p["conv_w"], p["conv_b"],
      p["w_ax"], p["b_a"], p["b_x"], p["lam"], p["w_a_up"], p["w_b_up"], p["w_out"])


def _mix_sample_kernel(x_ref, nmix_ref, win_ref, lbraw_ref, onorm_ref, convw_ref, convb_ref,
                       wax_ref, ba_ref, bx_ref, lam_ref, waup_ref, wbup_ref, wout_ref,
                       sin_ref, hlin_ref, cvin_ref,
                       xo_ref, sout_ref, hl_ref, cv_ref,
                       qt_s, ft_s, v_s, oa_s, gsil_s, ga_s, mb_s):
    nseq = x_ref.shape[0]
    blk = SAMPLE_STATE_BLOCK
    j = pl.program_id(0)

    @pl.when(j == 0)
    def _():
        x = x_ref[...]
        hb = _rms(x, nmix_ref[...]).astype(BF16)

        def proj(i):
            return _dot(hb, win_ref[:, i * D_MODEL:(i + 1) * D_MODEL])

        lb = _lower_bound(lbraw_ref[...], 0)
        qt_s[...] = _silu(proj(0)).T
        ft_s[...] = (lb + (1.0 - lb) * _sigmoid(proj(1))).T
        v_s[...] = proj(2)
        gsil_s[...] = onorm_ref[...] * _silu(proj(3))

        xr = proj(4)
        xc = convb_ref[...] + xr * convw_ref[CONV_W - 1:CONV_W, :]
        for i in range(CONV_W - 1):
            xc = xc + cvin_ref[:, i, :] * convw_ref[i:i + 1, :]
        for i in range(CONV_W - 2):
            cv_ref[:, i, :] = cvin_ref[:, i + 1, :]
        cv_ref[:, CONV_W - 2, :] = xr

        xcb = xc.astype(BF16)
        h_parts = []
        for n in range(LRU_BLOCKS):
            ns = slice(n * LRU_BW, (n + 1) * LRU_BW)
            pre = _dot(xcb[:, ns], wax_ref[n])
            a, bterm = _lru_coeffs(pre[:, :LRU_BW] + ba_ref[:, ns], pre[:, LRU_BW:] + bx_ref[:, ns],
                                   xc[:, ns], lam_ref[:, ns])
            h_parts.append(a * hlin_ref[:, ns] + bterm)
        h_new = jnp.concatenate(h_parts, axis=1)
        hl_ref[...] = h_new
        o_b = (h_new * _gelu_tanh(proj(5))).astype(BF16)
        ga_s[...] = _sigmoid(proj(6))
        mb_s[...] = _sigmoid(proj(7)) * _dot(o_b, wbup_ref[...])

    b0 = pl.multiple_of(j * blk, blk)
    shifts = [lax.rem(nseq - b0 + p * blk, nseq) for p in range(3)]

    lane = lax.broadcasted_iota(jnp.int32, (DK, nseq), 1)
    sel_r = lax.broadcasted_iota(jnp.int32, (nseq, blk * DV), 0)
    sel_c = lax.broadcasted_iota(jnp.int32, (nseq, blk * DV), 1)
    selector = jnp.where((sel_r < 3 * blk) & (jnp.bitwise_and(sel_r, blk - 1) == sel_c // DV),
                         1.0, 0.0).astype(BF16)

    def spread(cols_t):
        p1 = cols_t.astype(BF16).astype(F32)
        r1 = cols_t - p1
        p2 = r1.astype(BF16).astype(F32)
        p3 = r1 - p2
        lhs = jnp.where(lane < blk, pltpu.roll(p1, shifts[0], 1),
                        jnp.where(lane < 2 * blk, pltpu.roll(p2, shifts[1], 1),
                                  jnp.where(lane < 3 * blk, pltpu.roll(p3, shifts[2], 1), 0.0)))
        return _dot(lhs.astype(BF16), selector)

    for h in range(HEADS):
        hrows = slice(h * DK, (h + 1) * DK)
        q_all = spread(qt_s[hrows, :])
        f_all = spread(ft_s[hrows, :])
        v_rows = v_s[pl.ds(b0, blk), hrows]
        o_rows = []
        for i in range(blk):
            f_col = f_all[:, i * DV:(i + 1) * DV]
            s_new = f_col * sin_ref[i, h] + (1.0 - f_col) * v_rows[i:i + 1, :]
            sout_ref[i, h] = s_new
            o_rows.append(jnp.sum(q_all[:, i * DV:(i + 1) * DV] * s_new, axis=0, keepdims=True))
        o = jnp.concatenate(o_rows, axis=0)
        oa_s[pl.ds(b0, blk), hrows] = o * lax.rsqrt(
            jnp.mean(o * o, axis=-1, keepdims=True) + EPS)

    @pl.when(j == pl.num_programs(0) - 1)
    def _():
        o_a = (oa_s[...] * gsil_s[...]).astype(BF16)
        m = ga_s[...] * _dot(o_a, waup_ref[...]) + mb_s[...]
        xo_ref[...] = x_ref[...] + _dot(m.astype(BF16), wout_ref[...])


def _mix_sample(x2d, s_in, h_in, c_in, p):
    nseq = x2d.shape[0]
    blk = SAMPLE_STATE_BLOCK
    assert nseq == LANES and nseq % blk == 0
    row_vec = _resident((1, D_MODEL))
    tok = _resident((nseq, D_MODEL))
    conv_state = _resident((nseq, CONV_W - 1, D_MODEL))
    out_shape = (
        jax.ShapeDtypeStruct((nseq, D_MODEL), F32),
        jax.ShapeDtypeStruct((nseq, HEADS, DK, DV), F32),
        jax.ShapeDtypeStruct((nseq, D_MODEL), F32),
        jax.ShapeDtypeStruct((nseq, CONV_W - 1, D_MODEL), F32),
    )
    state_spec = pl.BlockSpec((blk, HEADS, DK, DV), lambda j: (j, 0, 0, 0))
    return pl.pallas_call(
        _mix_sample_kernel,
        grid=(nseq // blk,),
        in_specs=[
            tok, row_vec,
            _resident((D_MODEL, N_SPLITS * D_MODEL)),
            _resident(p["lb_raw"].shape),
            row_vec,
            _resident((CONV_W, D_MODEL)),
            row_vec,
            _resident((LRU_BLOCKS, LRU_BW, 2 * LRU_BW)),
            row_vec, row_vec, row_vec,
            _resident((D_MODEL, D_MODEL)),
            _resident((D_MODEL, D_MODEL)),
            _resident((D_MODEL, D_MODEL)),
            state_spec, tok, conv_state,
        ],
        out_specs=(
            pl.BlockSpec((nseq, D_MODEL), lambda j: (0, 0)),
            state_spec,
            pl.BlockSpec((nseq, D_MODEL), lambda j: (0, 0)),
            pl.BlockSpec((nseq, CONV_W - 1, D_MODEL), lambda j: (0, 0, 0)),
        ),
        out_shape=out_shape,
        scratch_shapes=[
            pltpu.VMEM((D_MODEL, nseq), F32),
            pltpu.VMEM((D_MODEL, nseq), F32),
            pltpu.VMEM((nseq, D_MODEL), F32),
            pltpu.VMEM((nseq, D_MODEL), F32),
            pltpu.VMEM((nseq, D_MODEL), F32),
            pltpu.VMEM((nseq, D_MODEL), F32),
            pltpu.VMEM((nseq, D_MODEL), F32),
        ],
        name="mix_sample",
        compiler_params=pltpu.CompilerParams(
            dimension_semantics=("arbitrary",), vmem_limit_bytes=VMEM_LIMIT_BYTES),
    )(x2d, p["mix_norm"], p["w_in"], p["lb_raw"], p["o_norm"], p["conv_w"], p["conv_b"],
      p["w_ax"], p["b_a"], p["b_x"], p["lam"], p["w_a_up"], p["w_b_up"], p["w_out"],
      s_in, h_in, c_in)


def kernel(x_prompt, x_sample, state_hgrn, state_lru, state_conv, ffn1_norm, ffn1_w_gate, ffn1_w_up, ffn1_w_down, mix_norm, w_in, hgrn_lower_bounds, hgrn_out_norm, conv_w, conv_b, lru_w_a, lru_b_a, lru_w_x, lru_b_x, lru_lambda, w_a_up, w_b_up, w_out, ffn2_norm, ffn2_w_gate, ffn2_w_up, ffn2_w_down, final_norm):
    depth = w_in.shape[0]
    assert depth == 1, "single-layer trunk"
    l = 0
    row = lambda v: v.reshape(1, -1).astype(F32)
    bf = lambda w: w.astype(BF16)
    p = {
        "mix_norm": row(mix_norm[l]),
        "w_in": bf(w_in[l]),
        "lb_raw": hgrn_lower_bounds.astype(F32),
        "o_norm": row(hgrn_out_norm[l]),
        "conv_w": conv_w[l].astype(F32),
        "conv_b": row(conv_b[l]),
        "w_ax": bf(jnp.concatenate([lru_w_a[l], lru_w_x[l]], axis=-1)),
        "b_a": row(lru_b_a[l]),
        "b_x": row(lru_b_x[l]),
        "lam": row(lru_lambda[l]),
        "w_a_up": bf(w_a_up[l]),
        "w_b_up": bf(w_b_up[l]),
        "w_out": bf(w_out[l]),
    }
    f1 = (row(ffn1_norm[l]), bf(ffn1_w_gate[l]), bf(ffn1_w_up[l]), bf(ffn1_w_down[l]))
    f2 = (row(ffn2_norm[l]), bf(ffn2_w_gate[l]), bf(ffn2_w_up[l]), bf(ffn2_w_down[l]))
    gfin = row(final_norm)

    bp, tp, _ = x_prompt.shape
    xp = _ffn(x_prompt.reshape(bp * tp, D_MODEL), *f1, gfin, final_norm=False)
    xp, s_p, h_p, c_p = _mix_prompt(xp.reshape(bp, tp, D_MODEL), p)
    y_p = _ffn(xp.reshape(bp * tp, D_MODEL), *f2, gfin, final_norm=True).reshape(bp, tp, D_MODEL)

    bs, ts, _ = x_sample.shape
    assert ts == 1
    xs = _ffn(x_sample.reshape(bs, D_MODEL), *f1, gfin, final_norm=False)
    xs, s_s, h_s, c_s = _mix_sample(xs, state_hgrn[l], state_lru[l], state_conv[l], p)
    y_s = _ffn(xs, *f2, gfin, final_norm=True).reshape(bs, ts, D_MODEL)

    return (y_p, y_s, s_p[None], h_p.reshape(bp, D_MODEL)[None], c_p[None],
            s_s[None], h_s[None], c_s[None])
```

```python
import functools

import jax
import jax.numpy as jnp
from jax import lax
from jax.experimental import pallas as pl
from jax.experimental.pallas import tpu as pltpu

D_MODEL = 1024
HEADS = 8
DK = 128
DV = 128
CHUNK = 64
LRU_BLOCKS = 8
LRU_BW = 128
LRU_C = 8.0
CONV_W = 4
D_FF = 2816
EPS = 1e-6
LOG2_E = 1.4426950408889634
N_SPLITS = 8

SUBLANES = 8
LANES = 128
MXU_COLS = 256
VMEM_LIMIT_BYTES = 56 * 1024 * 1024

FFN_ROWS = 512
MIX_ROWS = 256
SAMPLE_STATE_BLOCK = 8

BF16 = jnp.bfloat16
F32 = jnp.float32


def _dot(a, b):
    return jnp.dot(a, b, preferred_element_type=F32)


def _dot_tn(a, b):
    return lax.dot_general(a, b, (((0,), (0,)), ((), ())), preferred_element_type=F32)


def _dot_nt(a, b):
    return lax.dot_general(a, b, (((1,), (1,)), ((), ())), preferred_element_type=F32)


def _sigmoid(x):
    return 1.0 / (1.0 + jnp.exp2(x * (-LOG2_E)))


def _silu(x):
    return x * _sigmoid(x)


def _gelu_tanh(x):
    c = 0.7978845608028654
    return 0.5 * x * (1.0 + jnp.tanh(c * (x + 0.044715 * (x * x * x))))


def _rms(x, g):
    return x * lax.rsqrt(jnp.mean(x * x, axis=-1, keepdims=True) + EPS) * g


def _lower_bound(raw, layer):
    m = jnp.max(raw, axis=0, keepdims=True)
    e = jnp.exp(raw - m)
    den = jnp.sum(e, axis=0, keepdims=True)
    num = jnp.sum(e[0:layer + 1], axis=0, keepdims=True)
    return num / den


def _softplus(x):
    return jnp.maximum(x, 0.0) + jnp.log1p(jnp.exp(-jnp.abs(x)))


def _lru_coeffs(r_pre, i_pre, xc, lam):
    r = _sigmoid(r_pre)
    ig = _sigmoid(i_pre)
    log_a = (-LRU_C) * r * _softplus(-lam)
    a = jnp.exp(log_a)
    th = jnp.tanh(log_a)
    mult = jnp.sqrt(-2.0 * th / (1.0 - th))
    return a, mult * (ig * xc)


def _split3(x):
    p1 = x.astype(BF16)
    r1 = x - p1.astype(F32)
    p2 = r1.astype(BF16)
    r2 = r1 - p2.astype(F32)
    return p1, p2, r2.astype(BF16)


def _ffn_kernel(x_ref, g_ref, wg_ref, wu_ref, wd_ref, gf_ref, o_ref, *, final_norm):
    x = x_ref[...]
    h = _rms(x, g_ref[...]).astype(BF16)
    gate = _dot(h, wg_ref[...])
    up = _dot(h, wu_ref[...])
    act = (_silu(gate) * up).astype(BF16)
    y = x + 0.5 * _dot(act, wd_ref[...])
    if final_norm:
        y = _rms(y, gf_ref[...])
    o_ref[...] = y


def _resident(shape):
    nd = len(shape)
    return pl.BlockSpec(shape, lambda *_: (0,) * nd, pipeline_mode=pl.Buffered(1))


def _ffn(x2d, g, wg, wu, wd, gf, *, final_norm):
    n = x2d.shape[0]
    rows = min(FFN_ROWS, n)
    assert n % rows == 0
    return pl.pallas_call(
        functools.partial(_ffn_kernel, final_norm=final_norm),
        grid=(n // rows,),
        in_specs=[
            pl.BlockSpec((rows, D_MODEL), lambda i: (i, 0)),
            _resident((1, D_MODEL)),
            _resident((D_MODEL, D_FF)),
            _resident((D_MODEL, D_FF)),
            _resident((D_FF, D_MODEL)),
            _resident((1, D_MODEL)),
        ],
        out_specs=pl.BlockSpec((rows, D_MODEL), lambda i: (i, 0)),
        out_shape=jax.ShapeDtypeStruct((n, D_MODEL), F32),
        name="ffn_final" if final_norm else "ffn",
        compiler_params=pltpu.CompilerParams(
            dimension_semantics=("arbitrary",), vmem_limit_bytes=VMEM_LIMIT_BYTES),
    )(x2d, g, wg, wu, wd, gf)


SCAN_SEG = 4
SCAN_GROUP = SCAN_SEG * SUBLANES
CONV_PAD = SUBLANES
PIECE_ROWS = 2 * SUBLANES
QA_W = 2 * LANES


def _mix_prompt_kernel(x_ref, nmix_ref, win_ref, lbraw_ref, onorm_ref, convw_ref, convb_ref,
                       wax_ref, ba_ref, bx_ref, lam_ref, waup_ref, wbup_ref, wout_ref,
                       xo_ref, s_ref, hl_ref, cv_ref,
                       qa_s, ud_s, vb_s, oa_s, xr_s, h_s):
    rows = MIX_ROWS
    t = pl.program_id(1)

    @pl.when(t == 0)
    def _():
        s_ref[...] = jnp.zeros_like(s_ref)
        hl_ref[...] = jnp.zeros_like(hl_ref)
        xr_s[:, 0:CONV_PAD, :] = jnp.zeros((LRU_BLOCKS, CONV_PAD, LRU_BW), F32)

    x = x_ref[0]
    hb = _rms(x, nmix_ref[...]).astype(BF16)

    def proj(i):
        return _dot(hb, win_ref[:, i * D_MODEL:(i + 1) * D_MODEL])

    early = {}

    ncb = D_MODEL // MXU_COLS

    def issue_proj(i, cb):
        if cb < ncb and (i, cb) not in early:
            c0 = i * D_MODEL + cb * MXU_COLS
            early[(i, cb)] = _dot(hb, win_ref[:, c0:c0 + MXU_COLS])

    def gather(i):
        for cb in range(ncb):
            issue_proj(i, cb)
        return jnp.concatenate([early[(i, cb)] for cb in range(ncb)], axis=1)

    lb = _lower_bound(lbraw_ref[...], 0)
    q_act = _silu(proj(0))
    f = lb + (1.0 - lb) * _sigmoid(proj(1))
    kk = 1.0 - f
    vb_s[...] = proj(2).astype(BF16)

    ri = lax.broadcasted_iota(jnp.int32, (rows, rows), 0)
    ci = lax.broadcasted_iota(jnp.int32, (rows, rows), 1)
    same_chunk = (ri // CHUNK) == (ci // CHUNK)
    tril_blk = jnp.where((ri >= ci) & same_chunk, 1.0, 0.0).astype(BF16)
    g1, g2, g3 = _split3(jnp.log(f))
    b_all = _dot(tril_blk, g1) + _dot(tril_blk, g2) + _dot(tril_blk, g3)

    causal = (lax.broadcasted_iota(jnp.int32, (CHUNK, CHUNK), 0)
              >= lax.broadcasted_iota(jnp.int32, (CHUNK, CHUNK), 1))
    piece_row = lax.broadcasted_iota(jnp.int32, (PIECE_ROWS, D_MODEL), 0)
    ones_pv = jnp.ones((3 * PIECE_ROWS, DV), BF16)
    zeros_pv = jnp.zeros((3 * PIECE_ROWS, DV), BF16)
    zeros_cv = jnp.zeros((CHUNK, DV), BF16)

    nchunk = rows // CHUNK
    last = piece_row == PIECE_ROWS - 1
    zero_p = jnp.zeros((PIECE_ROWS, D_MODEL), BF16)

    for c in range(nchunk):
        issue_proj(3, c)
        rs = slice(c * CHUNK, (c + 1) * CHUNK)
        b = b_all[rs, :]
        bl = b[CHUNK - 1:CHUNK, :]
        qe = (q_act[rs, :] * jnp.exp(b)).astype(BF16)
        ke = (kk[rs, :] * jnp.exp(-b)).astype(BF16)
        kd = (kk[rs, :] * jnp.exp(bl - b)).astype(BF16)
        pieces = jnp.concatenate(
            [jnp.where(last, e, zero_p) for e in _split3(jnp.exp(b[CHUNK - PIECE_ROWS:, :]))], axis=0)
        for h in range(HEADS):
            hs = slice(h * DK, (h + 1) * DK)
            qa_s[rs, h * QA_W:h * QA_W + DK] = qe[:, hs]
            att = jnp.where(causal, _dot_nt(qe[:, hs], ke[:, hs]), 0.0)
            qa_s[rs, h * QA_W + DK:h * QA_W + DK + CHUNK] = att.astype(BF16)
            lhs = jnp.concatenate([kd[:, hs], pieces[:, hs]], axis=0)
            rhs = jnp.concatenate([jnp.concatenate([vb_s[rs, hs], zeros_cv], axis=1),
                                   jnp.concatenate([zeros_pv, ones_pv], axis=1)], axis=0)
            ud_s[c * HEADS + h] = _dot_tn(lhs, rhs)

    for c in range(nchunk):
        rs = slice(c * CHUNK, (c + 1) * CHUNK)
        for h in range(HEADS):
            hs = slice(h * DK, (h + 1) * DK)
            s_old = s_ref[0, h]
            rhs = jnp.concatenate([s_old.astype(BF16), vb_s[rs, hs]], axis=0)
            o = _dot(qa_s[rs, h * QA_W:h * QA_W + DK + CHUNK], rhs)
            oa_s[rs, hs] = o * lax.rsqrt(jnp.mean(o * o, axis=-1, keepdims=True) + EPS)
            ud = ud_s[c * HEADS + h]
            s_ref[0, h] = ud[:, DV:] * s_old + ud[:, 0:DV]

    o_a = (oa_s[...] * onorm_ref[...] * _silu(gather(3))).astype(BF16)
    up_a = {}

    def issue_up_a(cb):
        if cb < ncb and cb not in up_a:
            up_a[cb] = _dot(o_a, waup_ref[:, cb * MXU_COLS:(cb + 1) * MXU_COLS])

    xr = proj(4)
    h_prev = hl_ref[0]
    sub = lax.broadcasted_iota(jnp.int32, (SUBLANES, LANES), 0)
    ngroup = rows // SCAN_GROUP
    tails = []
    h_last = []
    for n in range(LRU_BLOCKS):
        ns = slice(n * LRU_BW, (n + 1) * LRU_BW)
        xr_s[n, CONV_PAD:CONV_PAD + rows, :] = xr[:, ns]
        taps = [jnp.broadcast_to(convw_ref[j:j + 1, ns], (SUBLANES, LRU_BW)) for j in range(CONV_W)]
        bias = jnp.broadcast_to(convb_ref[:, ns], (SUBLANES, LRU_BW))
        xc_parts = []
        rolled = {j: jnp.broadcast_to(xr_s[n, CONV_PAD + j - SCAN_SEG:CONV_PAD + j - SCAN_SEG + 1, :],
                                      (SUBLANES, LRU_BW)) for j in range(SCAN_SEG - CONV_W + 1, SCAN_SEG)}
        for grp in range(ngroup):
            xin = {d: xr_s[n, pl.ds(CONV_PAD + grp * SCAN_GROUP + d, SUBLANES, stride=SCAN_SEG), :]
                   for d in range(SCAN_SEG)}
            for j in range(SCAN_SEG - CONV_W + 1, SCAN_SEG):
                down = pltpu.roll(xin[j], 1, 0)
                xin[j - SCAN_SEG] = jnp.where(sub >= 1, down, rolled[j])
                rolled[j] = down
            for j in range(SCAN_SEG):
                acc = bias
                for tap in range(CONV_W):
                    acc = acc + xin[j + tap - (CONV_W - 1)] * taps[tap]
                xc_parts.append(acc)
        xc = jnp.concatenate(xc_parts, axis=0)
        tail = xr_s[n, CONV_PAD + rows - (CONV_W - 1):CONV_PAD + rows, :]
        tails.append(tail)
        xr_s[n, CONV_PAD - (CONV_W - 1):CONV_PAD, :] = tail
        pre = _dot(xc.astype(BF16), wax_ref[n])
        issue_proj(5 + n % 2, n // 2)
        a, bterm = _lru_coeffs(pre[:, :LRU_BW] + ba_ref[:, ns], pre[:, LRU_BW:] + bx_ref[:, ns],
                               xc, lam_ref[:, ns])
        carry = h_prev[:, ns]
        for grp in range(ngroup):
            def vreg(v, j, grp=grp):
                r0 = grp * SCAN_GROUP + j * SUBLANES
                return v[r0:r0 + SUBLANES, :]

            def strided(j, grp=grp):
                return pl.ds(grp * SCAN_GROUP + j, SUBLANES, stride=SCAN_SEG)
            pa = [vreg(a, 0)]
            hh = [vreg(bterm, 0)]
            for j in range(1, SCAN_SEG):
                aj = vreg(a, j)
                hh.append(aj * hh[-1] + vreg(bterm, j))
                pa.append(aj * pa[-1])
            pi, hi = pa[-1], hh[-1]
            for d in (1, 2, 4):
                keep = sub >= d
                hi = pi * jnp.where(keep, pltpu.roll(hi, d, 0), 0.0) + hi
                pi = pi * jnp.where(keep, pltpu.roll(pi, d, 0), 1.0)
            first = sub >= 1
            seg_in = (jnp.where(first, pltpu.roll(pi, 1, 0), 1.0) * carry
                      + jnp.where(first, pltpu.roll(hi, 1, 0), 0.0))
            carry = pi[SUBLANES - 1:, :] * carry + hi[SUBLANES - 1:, :]
            for j in range(SCAN_SEG):
                h_s[n, strided(j), :] = pa[j] * seg_in + hh[j]
        h_last.append(carry)
        if n % 2 == 0:
            issue_proj(7, n // 2)
        else:
            issue_up_a(n // 2)
    hl_ref[0] = jnp.concatenate(h_last, axis=1)
    cv_ref[0] = jnp.concatenate(tails, axis=1)
    gel = _gelu_tanh(gather(5))
    o_b = jnp.concatenate(
        [h_s[n] * gel[:, n * LRU_BW:(n + 1) * LRU_BW] for n in range(LRU_BLOCKS)], axis=1).astype(BF16)

    for cb in range(ncb):
        issue_up_a(cb)
    m = (_sigmoid(gather(6)) * jnp.concatenate([up_a[cb] for cb in range(ncb)], axis=1)
         + _sigmoid(gather(7)) * _dot(o_b, wbup_ref[...]))
    xo_ref[0] = x + _dot(m.astype(BF16), wout_ref[...])


def _mix_prompt(x, p):
    batch, seq, _ = x.shape
    rows = MIX_ROWS
    assert seq % rows == 0 and rows % CHUNK == 0 and rows >= CONV_W - 1
    row_vec = _resident((1, D_MODEL))
    out_shape = (
        jax.ShapeDtypeStruct((batch, seq, D_MODEL), F32),
        jax.ShapeDtypeStruct((batch, HEADS, DK, DV), F32),
        jax.ShapeDtypeStruct((batch, 1, D_MODEL), F32),
        jax.ShapeDtypeStruct((batch, CONV_W - 1, D_MODEL), F32),
    )
    return pl.pallas_call(
        _mix_prompt_kernel,
        grid=(batch, seq // rows),
        in_specs=[
            pl.BlockSpec((1, rows, D_MODEL), lambda b, t: (b, t, 0)),
            row_vec,
            _resident((D_MODEL, N_SPLITS * D_MODEL)),
            _resident(p["lb_raw"].shape),
            row_vec,
            _resident((CONV_W, D_MODEL)),
            row_vec,
            _resident((LRU_BLOCKS, LRU_BW, 2 * LRU_BW)),
            row_vec, row_vec, row_vec,
            _resident((D_MODEL, D_MODEL)),
            _resident((D_MODEL, D_MODEL)),
            _resident((D_MODEL, D_MODEL)),
        ],
        out_specs=(
            pl.BlockSpec((1, rows, D_MODEL), lambda b, t: (b, t, 0)),
            pl.BlockSpec((1, HEADS, DK, DV), lambda b, t: (b, 0, 0, 0)),
            pl.BlockSpec((1, 1, D_MODEL), lambda b, t: (b, 0, 0)),
            pl.BlockSpec((1, CONV_W - 1, D_MODEL), lambda b, t: (b, 0, 0)),
        ),
        out_shape=out_shape,
        scratch_shapes=[
            pltpu.VMEM((rows, HEADS * QA_W), BF16),
            pltpu.VMEM((rows // CHUNK * HEADS, DK, 2 * DV), F32),
            pltpu.VMEM((rows, D_MODEL), BF16),
            pltpu.VMEM((rows, D_MODEL), F32),
            pltpu.VMEM((LRU_BLOCKS, CONV_PAD + rows, LRU_BW), F32),
            pltpu.VMEM((LRU_BLOCKS, rows, LRU_BW), F32),
        ],
        name="mix_prompt",
        compiler_params=pltpu.CompilerParams(
            dimension_semantics=("arbitrary", "arbitrary"), vmem_limit_bytes=VMEM_LIMIT_BYTES),
    )(x, p["mix_norm"], p["w_in"], p["lb_raw"], p["o_norm"], p["conv_w"], p["conv_b"],
      p["w_ax"], p["b_a"], p["b_x"], p["lam"], p["w_a_up"], p["w_b_up"], p["w_out"])


def _mix_sample_kernel(x_ref, nmix_ref, win_ref, lbraw_ref, onorm_ref, convw_ref, convb_ref,
                       wax_ref, ba_ref, bx_ref, lam_ref, waup_ref, wbup_ref, wout_ref,
                       sin_ref, hlin_ref, cvin_ref,
                       xo_ref, sout_ref, hl_ref, cv_ref,
                       qt_s, ft_s, v_s, oa_s, gsil_s, ga_s, mb_s):
    nseq = x_ref.shape[0]
    blk = SAMPLE_STATE_BLOCK
    j = pl.program_id(0)

    @pl.when(j == 0)
    def _():
        x = x_ref[...]
        hb = _rms(x, nmix_ref[...]).astype(BF16)

        def proj(i):
            return _dot(hb, win_ref[:, i * D_MODEL:(i + 1) * D_MODEL])

        lb = _lower_bound(lbraw_ref[...], 0)
        qt_s[...] = _silu(proj(0)).T
        ft_s[...] = (lb + (1.0 - lb) * _sigmoid(proj(1))).T
        v_s[...] = proj(2)
        gsil_s[...] = onorm_ref[...] * _silu(proj(3))

        xr = proj(4)
        xc = convb_ref[...] + xr * convw_ref[CONV_W - 1:CONV_W, :]
        for i in range(CONV_W - 1):
            xc = xc + cvin_ref[:, i, :] * convw_ref[i:i + 1, :]
        for i in range(CONV_W - 2):
            cv_ref[:, i, :] = cvin_ref[:, i + 1, :]
        cv_ref[:, CONV_W - 2, :] = xr

        xcb = xc.astype(BF16)
        h_parts = []
        for n in range(LRU_BLOCKS):
            ns = slice(n * LRU_BW, (n + 1) * LRU_BW)
            pre = _dot(xcb[:, ns], wax_ref[n])
            a, bterm = _lru_coeffs(pre[:, :LRU_BW] + ba_ref[:, ns], pre[:, LRU_BW:] + bx_ref[:, ns],
                                   xc[:, ns], lam_ref[:, ns])
            h_parts.append(a * hlin_ref[:, ns] + bterm)
        h_new = jnp.concatenate(h_parts, axis=1)
        hl_ref[...] = h_new
        o_b = (h_new * _gelu_tanh(proj(5))).astype(BF16)
        ga_s[...] = _sigmoid(proj(6))
        mb_s[...] = _sigmoid(proj(7)) * _dot(o_b, wbup_ref[...])

    b0 = pl.multiple_of(j * blk, blk)
    shifts = [lax.rem(nseq - b0 + p * blk, nseq) for p in range(3)]

    lane = lax.broadcasted_iota(jnp.int32, (DK, nseq), 1)
    sel_r = lax.broadcasted_iota(jnp.int32, (nseq, blk * DV), 0)
    sel_c = lax.broadcasted_iota(jnp.int32, (nseq, blk * DV), 1)
    selector = jnp.where((sel_r < 3 * blk) & (jnp.bitwise_and(sel_r, blk - 1) == sel_c // DV),
                         1.0, 0.0).astype(BF16)

    def spread(cols_t):
        p1 = cols_t.astype(BF16).astype(F32)
        r1 = cols_t - p1
        p2 = r1.astype(BF16).astype(F32)
        p3 = r1 - p2
        lhs = jnp.where(lane < blk, pltpu.roll(p1, shifts[0], 1),
                        jnp.where(lane < 2 * blk, pltpu.roll(p2, shifts[1], 1),
                                  jnp.where(lane < 3 * blk, pltpu.roll(p3, shifts[2], 1), 0.0)))
        return _dot(lhs.astype(BF16), selector)

    for h in range(HEADS):
        hrows = slice(h * DK, (h + 1) * DK)
        q_all = spread(qt_s[hrows, :])
        f_all = spread(ft_s[hrows, :])
        v_rows = v_s[pl.ds(b0, blk), hrows]
        o_rows = []
        for i in range(blk):
            f_col = f_all[:, i * DV:(i + 1) * DV]
            s_new = f_col * sin_ref[i, h] + (1.0 - f_col) * v_rows[i:i + 1, :]
            sout_ref[i, h] = s_new
            o_rows.append(jnp.sum(q_all[:, i * DV:(i + 1) * DV] * s_new, axis=0, keepdims=True))
        o = jnp.concatenate(o_rows, axis=0)
        oa_s[pl.ds(b0, blk), hrows] = o * lax.rsqrt(
            jnp.mean(o * o, axis=-1, keepdims=True) + EPS)

    @pl.when(j == pl.num_programs(0) - 1)
    def _():
        o_a = (oa_s[...] * gsil_s[...]).astype(BF16)
        m = ga_s[...] * _dot(o_a, waup_ref[...]) + mb_s[...]
        xo_ref[...] = x_ref[...] + _dot(m.astype(BF16), wout_ref[...])


def _mix_sample(x2d, s_in, h_in, c_in, p):
    nseq = x2d.shape[0]
    blk = SAMPLE_STATE_BLOCK
    assert nseq == LANES and nseq % blk == 0
    row_vec = _resident((1, D_MODEL))
    tok = _resident((nseq, D_MODEL))
    conv_state = _resident((nseq, CONV_W - 1, D_MODEL))
    out_shape = (
        jax.ShapeDtypeStruct((nseq, D_MODEL), F32),
        jax.ShapeDtypeStruct((nseq, HEADS, DK, DV), F32),
        jax.ShapeDtypeStruct((nseq, D_MODEL), F32),
        jax.ShapeDtypeStruct((nseq, CONV_W - 1, D_MODEL), F32),
    )
    state_spec = pl.BlockSpec((blk, HEADS, DK, DV), lambda j: (j, 0, 0, 0))
    return pl.pallas_call(
        _mix_sample_kernel,
        grid=(nseq // blk,),
        in_specs=[
            tok, row_vec,
            _resident((D_MODEL, N_SPLITS * D_MODEL)),
            _resident(p["lb_raw"].shape),
            row_vec,
            _resident((CONV_W, D_MODEL)),
            row_vec,
            _resident((LRU_BLOCKS, LRU_BW, 2 * LRU_BW)),
            row_vec, row_vec, row_vec,
            _resident((D_MODEL, D_MODEL)),
            _resident((D_MODEL, D_MODEL)),
            _resident((D_MODEL, D_MODEL)),
            state_spec, tok, conv_state,
        ],
        out_specs=(
            pl.BlockSpec((nseq, D_MODEL), lambda j: (0, 0)),
            state_spec,
            pl.BlockSpec((nseq, D_MODEL), lambda j: (0, 0)),
            pl.BlockSpec((nseq, CONV_W - 1, D_MODEL), lambda j: (0, 0, 0)),
        ),
        out_shape=out_shape,
        scratch_shapes=[
            pltpu.VMEM((D_MODEL, nseq), F32),
            pltpu.VMEM((D_MODEL, nseq), F32),
            pltpu.VMEM((nseq, D_MODEL), F32),
            pltpu.VMEM((nseq, D_MODEL), F32),
            pltpu.VMEM((nseq, D_MODEL), F32),
            pltpu.VMEM((nseq, D_MODEL), F32),
            pltpu.VMEM((nseq, D_MODEL), F32),
        ],
        name="mix_sample",
        compiler_params=pltpu.CompilerParams(
            dimension_semantics=("arbitrary",), vmem_limit_bytes=VMEM_LIMIT_BYTES),
    )(x2d, p["mix_norm"], p["w_in"], p["lb_raw"], p["o_norm"], p["conv_w"], p["conv_b"],
      p["w_ax"], p["b_a"], p["b_x"], p["lam"], p["w_a_up"], p["w_b_up"], p["w_out"],
      s_in, h_in, c_in)


def kernel(x_prompt, x_sample, state_hgrn, state_lru, state_conv, ffn1_norm, ffn1_w_gate, ffn1_w_up, ffn1_w_down, mix_norm, w_in, hgrn_lower_bounds, hgrn_out_norm, conv_w, conv_b, lru_w_a, lru_b_a, lru_w_x, lru_b_x, lru_lambda, w_a_up, w_b_up, w_out, ffn2_norm, ffn2_w_gate, ffn2_w_up, ffn2_w_down, final_norm):
    depth = w_in.shape[0]
    assert depth == 1, "single-layer trunk"
    l = 0
    row = lambda v: v.reshape(1, -1).astype(F32)
    bf = lambda w: w.astype(BF16)
    p = {
        "mix_norm": row(mix_norm[l]),
        "w_in": bf(w_in[l]),
        "lb_raw": hgrn_lower_bounds.astype(F32),
        "o_norm": row(hgrn_out_norm[l]),
        "conv_w": conv_w[l].astype(F32),
        "conv_b": row(conv_b[l]),
        "w_ax": bf(jnp.concatenate([lru_w_a[l], lru_w_x[l]], axis=-1)),
        "b_a": row(lru_b_a[l]),
        "b_x": row(lru_b_x[l]),
        "lam": row(lru_lambda[l]),
        "w_a_up": bf(w_a_up[l]),
        "w_b_up": bf(w_b_up[l]),
        "w_out": bf(w_out[l]),
    }
    f1 = (row(ffn1_norm[l]), bf(ffn1_w_gate[l]), bf(ffn1_w_up[l]), bf(ffn1_w_down[l]))
    f2 = (row(ffn2_norm[l]), bf(ffn2_w_gate[l]), bf(ffn2_w_up[l]), bf(ffn2_w_down[l]))
    gfin = row(final_norm)

    bp, tp, _ = x_prompt.shape
    xp = _ffn(x_prompt.reshape(bp * tp, D_MODEL), *f1, gfin, final_norm=False)
    xp, s_p, h_p, c_p = _mix_prompt(xp.reshape(bp, tp, D_MODEL), p)
    y_p = _ffn(xp.reshape(bp * tp, D_MODEL), *f2, gfin, final_norm=True).reshape(bp, tp, D_MODEL)

    bs, ts, _ = x_sample.shape
    assert ts == 1
    xs = _ffn(x_sample.reshape(bs, D_MODEL), *f1, gfin, final_norm=False)
    xs, s_s, h_s, c_s = _mix_sample(xs, state_hgrn[l], state_lru[l], state_conv[l], p)
    y_s = _ffn(xs, *f2, gfin, final_norm=True).reshape(bs, ts, D_MODEL)

    return (y_p, y_s, s_p[None], h_p.reshape(bp, D_MODEL)[None], c_p[None],
            s_s[None], h_s[None], c_s[None])
```

```python
import functools

import jax
import jax.numpy as jnp
from jax import lax
from jax.experimental import pallas as pl
from jax.experimental.pallas import tpu as pltpu

D_MODEL = 1024
HEADS = 8
DK = 128
DV = 128
CHUNK = 64
LRU_BLOCKS = 8
LRU_BW = 128
LRU_C = 8.0
CONV_W = 4
D_FF = 2816
EPS = 1e-6
LOG2_E = 1.4426950408889634
N_SPLITS = 8

SUBLANES = 8
LANES = 128
MXU_COLS = 256
VMEM_LIMIT_BYTES = 56 * 1024 * 1024

FFN_ROWS = 512
MIX_ROWS = 512
CUMSUM_ROWS = 256
SAMPLE_STATE_BLOCK = 8

BF16 = jnp.bfloat16
F32 = jnp.float32


def _dot(a, b):
    return jnp.dot(a, b, preferred_element_type=F32)


def _dot_tn(a, b):
    return lax.dot_general(a, b, (((0,), (0,)), ((), ())), preferred_element_type=F32)


def _dot_nt(a, b):
    return lax.dot_general(a, b, (((1,), (1,)), ((), ())), preferred_element_type=F32)


def _sigmoid(x):
    return 1.0 / (1.0 + jnp.exp2(x * (-LOG2_E)))


def _silu(x):
    return x * _sigmoid(x)


def _gelu_tanh(x):
    c = 0.7978845608028654
    return 0.5 * x * (1.0 + jnp.tanh(c * (x + 0.044715 * (x * x * x))))


def _rms(x, g):
    return x * lax.rsqrt(jnp.mean(x * x, axis=-1, keepdims=True) + EPS) * g


def _lower_bound(raw, layer):
    m = jnp.max(raw, axis=0, keepdims=True)
    e = jnp.exp(raw - m)
    den = jnp.sum(e, axis=0, keepdims=True)
    num = jnp.sum(e[0:layer + 1], axis=0, keepdims=True)
    return num / den


def _softplus(x):
    return jnp.maximum(x, 0.0) + jnp.log1p(jnp.exp(-jnp.abs(x)))


def _lru_coeffs(r_pre, i_pre, xc, lam):
    r = _sigmoid(r_pre)
    ig = _sigmoid(i_pre)
    log_a = (-LRU_C) * r * _softplus(-lam)
    a = jnp.exp(log_a)
    th = jnp.tanh(log_a)
    mult = jnp.sqrt(-2.0 * th / (1.0 - th))
    return a, mult * (ig * xc)


def _split3(x):
    p1 = x.astype(BF16)
    r1 = x - p1.astype(F32)
    p2 = r1.astype(BF16)
    r2 = r1 - p2.astype(F32)
    return p1, p2, r2.astype(BF16)


def _ffn_kernel(x_ref, g_ref, wg_ref, wu_ref, wd_ref, gf_ref, o_ref, *, final_norm):
    x = x_ref[...]
    h = _rms(x, g_ref[...]).astype(BF16)
    gate = _dot(h, wg_ref[...])
    up = _dot(h, wu_ref[...])
    act = (_silu(gate) * up).astype(BF16)
    y = x + 0.5 * _dot(act, wd_ref[...])
    if final_norm:
        y = _rms(y, gf_ref[...])
    o_ref[...] = y


def _resident(shape):
    nd = len(shape)
    return pl.BlockSpec(shape, lambda *_: (0,) * nd, pipeline_mode=pl.Buffered(1))


def _ffn(x2d, g, wg, wu, wd, gf, *, final_norm):
    n = x2d.shape[0]
    rows = min(FFN_ROWS, n)
    assert n % rows == 0
    return pl.pallas_call(
        functools.partial(_ffn_kernel, final_norm=final_norm),
        grid=(n // rows,),
        in_specs=[
            pl.BlockSpec((rows, D_MODEL), lambda i: (i, 0)),
            _resident((1, D_MODEL)),
            _resident((D_MODEL, D_FF)),
            _resident((D_MODEL, D_FF)),
            _resident((D_FF, D_MODEL)),
            _resident((1, D_MODEL)),
        ],
        out_specs=pl.BlockSpec((rows, D_MODEL), lambda i: (i, 0)),
        out_shape=jax.ShapeDtypeStruct((n, D_MODEL), F32),
        name="ffn_final" if final_norm else "ffn",
        compiler_params=pltpu.CompilerParams(
            dimension_semantics=("arbitrary",), vmem_limit_bytes=VMEM_LIMIT_BYTES),
    )(x2d, g, wg, wu, wd, gf)


SCAN_SEG = 4
SCAN_GROUP = SCAN_SEG * SUBLANES
CONV_PAD = SUBLANES
PIECE_ROWS = 2 * SUBLANES
QA_W = 2 * LANES


def _mix_prompt_kernel(x_ref, nmix_ref, win_ref, lbraw_ref, onorm_ref, convw_ref, convb_ref,
                       wax_ref, ba_ref, bx_ref, lam_ref, waup_ref, wbup_ref, wout_ref,
                       xo_ref, s_ref, hl_ref, cv_ref,
                       qa_s, ud_s, vb_s, oa_s, xr_s, h_s):
    rows = MIX_ROWS
    t = pl.program_id(1)

    @pl.when(t == 0)
    def _():
        s_ref[...] = jnp.zeros_like(s_ref)
        hl_ref[...] = jnp.zeros_like(hl_ref)
        xr_s[:, 0:CONV_PAD, :] = jnp.zeros((LRU_BLOCKS, CONV_PAD, LRU_BW), F32)

    x = x_ref[0]
    hb = _rms(x, nmix_ref[...]).astype(BF16)

    def proj(i):
        return _dot(hb, win_ref[:, i * D_MODEL:(i + 1) * D_MODEL])

    early = {}

    ncb = D_MODEL // MXU_COLS

    def issue_proj(i, cb):
        if cb < ncb and (i, cb) not in early:
            c0 = i * D_MODEL + cb * MXU_COLS
            early[(i, cb)] = _dot(hb, win_ref[:, c0:c0 + MXU_COLS])

    def gather(i):
        for cb in range(ncb):
            issue_proj(i, cb)
        return jnp.concatenate([early[(i, cb)] for cb in range(ncb)], axis=1)

    lb = _lower_bound(lbraw_ref[...], 0)
    q_act = _silu(proj(0))
    f = lb + (1.0 - lb) * _sigmoid(proj(1))
    kk = 1.0 - f
    vb_s[...] = proj(2).astype(BF16)

    ri = lax.broadcasted_iota(jnp.int32, (CUMSUM_ROWS, CUMSUM_ROWS), 0)
    ci = lax.broadcasted_iota(jnp.int32, (CUMSUM_ROWS, CUMSUM_ROWS), 1)
    same_chunk = (ri // CHUNK) == (ci // CHUNK)
    tril_blk = jnp.where((ri >= ci) & same_chunk, 1.0, 0.0).astype(BF16)
    g1, g2, g3 = _split3(jnp.log(f))
    b_all = jnp.concatenate(
        [_dot(tril_blk, g1[r0:r0 + CUMSUM_ROWS]) + _dot(tril_blk, g2[r0:r0 + CUMSUM_ROWS])
         + _dot(tril_blk, g3[r0:r0 + CUMSUM_ROWS]) for r0 in range(0, rows, CUMSUM_ROWS)], axis=0)

    causal = (lax.broadcasted_iota(jnp.int32, (CHUNK, CHUNK), 0)
              >= lax.broadcasted_iota(jnp.int32, (CHUNK, CHUNK), 1))
    piece_row = lax.broadcasted_iota(jnp.int32, (PIECE_ROWS, D_MODEL), 0)
    ones_pv = jnp.ones((3 * PIECE_ROWS, DV), BF16)
    zeros_pv = jnp.zeros((3 * PIECE_ROWS, DV), BF16)
    zeros_cv = jnp.zeros((CHUNK, DV), BF16)

    nchunk = rows // CHUNK
    last = piece_row == PIECE_ROWS - 1
    zero_p = jnp.zeros((PIECE_ROWS, D_MODEL), BF16)

    for c in range(nchunk):
        issue_proj(3, c)
        rs = slice(c * CHUNK, (c + 1) * CHUNK)
        b = b_all[rs, :]
        bl = b[CHUNK - 1:CHUNK, :]
        qe = (q_act[rs, :] * jnp.exp(b)).astype(BF16)
        ke = (kk[rs, :] * jnp.exp(-b)).astype(BF16)
        kd = (kk[rs, :] * jnp.exp(bl - b)).astype(BF16)
        pieces = jnp.concatenate(
            [jnp.where(last, e, zero_p) for e in _split3(jnp.exp(b[CHUNK - PIECE_ROWS:, :]))], axis=0)
        for h in range(HEADS):
            hs = slice(h * DK, (h + 1) * DK)
            qa_s[rs, h * QA_W:h * QA_W + DK] = qe[:, hs]
            att = jnp.where(causal, _dot_nt(qe[:, hs], ke[:, hs]), 0.0)
            qa_s[rs, h * QA_W + DK:h * QA_W + DK + CHUNK] = att.astype(BF16)
            lhs = jnp.concatenate([kd[:, hs], pieces[:, hs]], axis=0)
            rhs = jnp.concatenate([jnp.concatenate([vb_s[rs, hs], zeros_cv], axis=1),
                                   jnp.concatenate([zeros_pv, ones_pv], axis=1)], axis=0)
            ud_s[c * HEADS + h] = _dot_tn(lhs, rhs)

    for c in range(nchunk):
        rs = slice(c * CHUNK, (c + 1) * CHUNK)
        for h in range(HEADS):
            hs = slice(h * DK, (h + 1) * DK)
            s_old = s_ref[0, h]
            rhs = jnp.concatenate([s_old.astype(BF16), vb_s[rs, hs]], axis=0)
            o = _dot(qa_s[rs, h * QA_W:h * QA_W + DK + CHUNK], rhs)
            oa_s[rs, hs] = o * lax.rsqrt(jnp.mean(o * o, axis=-1, keepdims=True) + EPS)
            ud = ud_s[c * HEADS + h]
            s_ref[0, h] = ud[:, DV:] * s_old + ud[:, 0:DV]

    o_a = (oa_s[...] * onorm_ref[...] * _silu(gather(3))).astype(BF16)
    up_a = {}

    def issue_up_a(cb):
        if cb < ncb and cb not in up_a:
            up_a[cb] = _dot(o_a, waup_ref[:, cb * MXU_COLS:(cb + 1) * MXU_COLS])

    xr = proj(4)
    h_prev = hl_ref[0]
    sub = lax.broadcasted_iota(jnp.int32, (SUBLANES, LANES), 0)
    ngroup = rows // SCAN_GROUP
    tails = []
    h_last = []
    for n in range(LRU_BLOCKS):
        ns = slice(n * LRU_BW, (n + 1) * LRU_BW)
        xr_s[n, CONV_PAD:CONV_PAD + rows, :] = xr[:, ns]
        taps = [jnp.broadcast_to(convw_ref[j:j + 1, ns], (SUBLANES, LRU_BW)) for j in range(CONV_W)]
        bias = jnp.broadcast_to(convb_ref[:, ns], (SUBLANES, LRU_BW))
        xc_parts = []
        for grp in range(ngroup):
            xin = {d: xr_s[n, pl.ds(CONV_PAD + grp * SCAN_GROUP + d, SUBLANES, stride=SCAN_SEG), :]
                   for d in range(1 - CONV_W, SCAN_SEG)}
            for j in range(SCAN_SEG):
                acc = bias
                for tap in range(CONV_W):
                    acc = acc + xin[j + tap - (CONV_W - 1)] * taps[tap]
                xc_parts.append(acc)
        xc = jnp.concatenate(xc_parts, axis=0)
        tail = xr_s[n, CONV_PAD + rows - (CONV_W - 1):CONV_PAD + rows, :]
        tails.append(tail)
        xr_s[n, CONV_PAD - (CONV_W - 1):CONV_PAD, :] = tail
        pre = _dot(xc.astype(BF16), wax_ref[n])
        issue_proj(5 + n % 2, n // 2)
        a, bterm = _lru_coeffs(pre[:, :LRU_BW] + ba_ref[:, ns], pre[:, LRU_BW:] + bx_ref[:, ns],
                               xc, lam_ref[:, ns])
        carry = h_prev[:, ns]
        for grp in range(ngroup):
            def vreg(v, j, grp=grp):
                r0 = grp * SCAN_GROUP + j * SUBLANES
                return v[r0:r0 + SUBLANES, :]

            def strided(j, grp=grp):
                return pl.ds(grp * SCAN_GROUP + j, SUBLANES, stride=SCAN_SEG)
            pa = [vreg(a, 0)]
            hh = [vreg(bterm, 0)]
            for j in range(1, SCAN_SEG):
                aj = vreg(a, j)
                hh.append(aj * hh[-1] + vreg(bterm, j))
                pa.append(aj * pa[-1])
            pi, hi = pa[-1], hh[-1]
            for d in (1, 2, 4):
                keep = sub >= d
                hi = pi * jnp.where(keep, pltpu.roll(hi, d, 0), 0.0) + hi
                pi = pi * jnp.where(keep, pltpu.roll(pi, d, 0), 1.0)
            first = sub >= 1
            seg_in = (jnp.where(first, pltpu.roll(pi, 1, 0), 1.0) * carry
                      + jnp.where(first, pltpu.roll(hi, 1, 0), 0.0))
            carry = pi[SUBLANES - 1:, :] * carry + hi[SUBLANES - 1:, :]
            for j in range(SCAN_SEG):
                h_s[n, strided(j), :] = pa[j] * seg_in + hh[j]
        h_last.append(carry)
        if n % 2 == 0:
            issue_proj(7, n // 2)
        else:
            issue_up_a(n // 2)
    hl_ref[0] = jnp.concatenate(h_last, axis=1)
    cv_ref[0] = jnp.concatenate(tails, axis=1)
    gel = _gelu_tanh(gather(5))
    o_b = jnp.concatenate(
        [h_s[n] * gel[:, n * LRU_BW:(n + 1) * LRU_BW] for n in range(LRU_BLOCKS)], axis=1).astype(BF16)

    for cb in range(ncb):
        issue_up_a(cb)
    m = (_sigmoid(gather(6)) * jnp.concatenate([up_a[cb] for cb in range(ncb)], axis=1)
         + _sigmoid(gather(7)) * _dot(o_b, wbup_ref[...]))
    xo_ref[0] = x + _dot(m.astype(BF16), wout_ref[...])


def _mix_prompt(x, p):
    batch, seq, _ = x.shape
    rows = MIX_ROWS
    assert seq % rows == 0 and rows % CHUNK == 0 and rows >= CONV_W - 1
    row_vec = _resident((1, D_MODEL))
    out_shape = (
        jax.ShapeDtypeStruct((batch, seq, D_MODEL), F32),
        jax.ShapeDtypeStruct((batch, HEADS, DK, DV), F32),
        jax.ShapeDtypeStruct((batch, 1, D_MODEL), F32),
        jax.ShapeDtypeStruct((batch, CONV_W - 1, D_MODEL), F32),
    )
    return pl.pallas_call(
        _mix_prompt_kernel,
        grid=(batch, seq // rows),
        in_specs=[
            pl.BlockSpec((1, rows, D_MODEL), lambda b, t: (b, t, 0)),
            row_vec,
            _resident((D_MODEL, N_SPLITS * D_MODEL)),
            _resident(p["lb_raw"].shape),
            row_vec,
            _resident((CONV_W, D_MODEL)),
            row_vec,
            _resident((LRU_BLOCKS, LRU_BW, 2 * LRU_BW)),
            row_vec, row_vec, row_vec,
            _resident((D_MODEL, D_MODEL)),
            _resident((D_MODEL, D_MODEL)),
            _resident((D_MODEL, D_MODEL)),
        ],
        out_specs=(
            pl.BlockSpec((1, rows, D_MODEL), lambda b, t: (b, t, 0)),
            pl.BlockSpec((1, HEADS, DK, DV), lambda b, t: (b, 0, 0, 0)),
            pl.BlockSpec((1, 1, D_MODEL), lambda b, t: (b, 0, 0)),
            pl.BlockSpec((1, CONV_W - 1, D_MODEL), lambda b, t: (b, 0, 0)),
        ),
        out_shape=out_shape,
        scratch_shapes=[
            pltpu.VMEM((rows, HEADS * QA_W), BF16),
            pltpu.VMEM((rows // CHUNK * HEADS, DK, 2 * DV), F32),
            pltpu.VMEM((rows, D_MODEL), BF16),
            pltpu.VMEM((rows, D_MODEL), F32),
            pltpu.VMEM((LRU_BLOCKS, CONV_PAD + rows, LRU_BW), F32),
            pltpu.VMEM((LRU_BLOCKS, rows, LRU_BW), F32),
        ],
        name="mix_prompt",
        compiler_params=pltpu.CompilerParams(
            dimension_semantics=("arbitrary", "arbitrary"), vmem_limit_bytes=VMEM_LIMIT_BYTES),
    )(x, p["mix_norm"], p["w_in"], p["lb_raw"], p["o_norm"], p["conv_w"], p["conv_b"],
      p["w_ax"], p["b_a"], p["b_x"], p["lam"], p["w_a_up"], p["w_b_up"], p["w_out"])


def _mix_sample_kernel(x_ref, nmix_ref, win_ref, lbraw_ref, onorm_ref, convw_ref, convb_ref,
                       wax_ref, ba_ref, bx_ref, lam_ref, waup_ref, wbup_ref, wout_ref,
                       sin_ref, hlin_ref, cvin_ref,
                       xo_ref, sout_ref, hl_ref, cv_ref,
                       qt_s, ft_s, v_s, oa_s, gsil_s, ga_s, mb_s):
    nseq = x_ref.shape[0]
    blk = SAMPLE_STATE_BLOCK
    j = pl.program_id(0)

    @pl.when(j == 0)
    def _():
        x = x_ref[...]
        hb = _rms(x, nmix_ref[...]).astype(BF16)

        def proj(i):
            return _dot(hb, win_ref[:, i * D_MODEL:(i + 1) * D_MODEL])

        lb = _lower_bound(lbraw_ref[...], 0)
        qt_s[...] = _silu(proj(0)).T
        ft_s[...] = (lb + (1.0 - lb) * _sigmoid(proj(1))).T
        v_s[...] = proj(2)
        gsil_s[...] = onorm_ref[...] * _silu(proj(3))

        xr = proj(4)
        xc = convb_ref[...] + xr * convw_ref[CONV_W - 1:CONV_W, :]
        for i in range(CONV_W - 1):
            xc = xc + cvin_ref[:, i, :] * convw_ref[i:i + 1, :]
        for i in range(CONV_W - 2):
            cv_ref[:, i, :] = cvin_ref[:, i + 1, :]
        cv_ref[:, CONV_W - 2, :] = xr

        xcb = xc.astype(BF16)
        h_parts = []
        for n in range(LRU_BLOCKS):
            ns = slice(n * LRU_BW, (n + 1) * LRU_BW)
            pre = _dot(xcb[:, ns], wax_ref[n])
            a, bterm = _lru_coeffs(pre[:, :LRU_BW] + ba_ref[:, ns], pre[:, LRU_BW:] + bx_ref[:, ns],
                                   xc[:, ns], lam_ref[:, ns])
            h_parts.append(a * hlin_ref[:, ns] + bterm)
        h_new = jnp.concatenate(h_parts, axis=1)
        hl_ref[...] = h_new
        o_b = (h_new * _gelu_tanh(proj(5))).astype(BF16)
        ga_s[...] = _sigmoid(proj(6))
        mb_s[...] = _sigmoid(proj(7)) * _dot(o_b, wbup_ref[...])

    b0 = pl.multiple_of(j * blk, blk)
    shifts = [lax.rem(nseq - b0 + p * blk, nseq) for p in range(3)]

    lane = lax.broadcasted_iota(jnp.int32, (DK, nseq), 1)
    sel_r = lax.broadcasted_iota(jnp.int32, (nseq, blk * DV), 0)
    sel_c = lax.broadcasted_iota(jnp.int32, (nseq, blk * DV), 1)
    selector = jnp.where((sel_r < 3 * blk) & (jnp.bitwise_and(sel_r, blk - 1) == sel_c // DV),
                         1.0, 0.0).astype(BF16)

    def spread(cols_t):
        p1 = cols_t.astype(BF16).astype(F32)
        r1 = cols_t - p1
        p2 = r1.astype(BF16).astype(F32)
        p3 = r1 - p2
        lhs = jnp.where(lane < blk, pltpu.roll(p1, shifts[0], 1),
                        jnp.where(lane < 2 * blk, pltpu.roll(p2, shifts[1], 1),
                                  jnp.where(lane < 3 * blk, pltpu.roll(p3, shifts[2], 1), 0.0)))
        return _dot(lhs.astype(BF16), selector)

    for h in range(HEADS):
        hrows = slice(h * DK, (h + 1) * DK)
        q_all = spread(qt_s[hrows, :])
        f_all = spread(ft_s[hrows, :])
        v_rows = v_s[pl.ds(b0, blk), hrows]
        o_rows = []
        for i in range(blk):
            f_col = f_all[:, i * DV:(i + 1) * DV]
            s_new = f_col * sin_ref[i, h] + (1.0 - f_col) * v_rows[i:i + 1, :]
            sout_ref[i, h] = s_new
            o_rows.append(jnp.sum(q_all[:, i * DV:(i + 1) * DV] * s_new, axis=0, keepdims=True))
        o = jnp.concatenate(o_rows, axis=0)
        oa_s[pl.ds(b0, blk), hrows] = o * lax.rsqrt(
            jnp.mean(o * o, axis=-1, keepdims=True) + EPS)

    @pl.when(j == pl.num_programs(0) - 1)
    def _():
        o_a = (oa_s[...] * gsil_s[...]).astype(BF16)
        m = ga_s[...] * _dot(o_a, waup_ref[...]) + mb_s[...]
        xo_ref[...] = x_ref[...] + _dot(m.astype(BF16), wout_ref[...])


def _mix_sample(x2d, s_in, h_in, c_in, p):
    nseq = x2d.shape[0]
    blk = SAMPLE_STATE_BLOCK
    assert nseq == LANES and nseq % blk == 0
    row_vec = _resident((1, D_MODEL))
    tok = _resident((nseq, D_MODEL))
    conv_state = _resident((nseq, CONV_W - 1, D_MODEL))
    out_shape = (
        jax.ShapeDtypeStruct((nseq, D_MODEL), F32),
        jax.ShapeDtypeStruct((nseq, HEADS, DK, DV), F32),
        jax.ShapeDtypeStruct((nseq, D_MODEL), F32),
        jax.ShapeDtypeStruct((nseq, CONV_W - 1, D_MODEL), F32),
    )
    state_spec = pl.BlockSpec((blk, HEADS, DK, DV), lambda j: (j, 0, 0, 0))
    return pl.pallas_call(
        _mix_sample_kernel,
        grid=(nseq // blk,),
        in_specs=[
            tok, row_vec,
            _resident((D_MODEL, N_SPLITS * D_MODEL)),
            _resident(p["lb_raw"].shape),
            row_vec,
            _resident((CONV_W, D_MODEL)),
            row_vec,
            _resident((LRU_BLOCKS, LRU_BW, 2 * LRU_BW)),
            row_vec, row_vec, row_vec,
            _resident((D_MODEL, D_MODEL)),
            _resident((D_MODEL, D_MODEL)),
            _resident((D_MODEL, D_MODEL)),
            state_spec, tok, conv_state,
        ],
        out_specs=(
            pl.BlockSpec((nseq, D_MODEL), lambda j: (0, 0)),
            state_spec,
            pl.BlockSpec((nseq, D_MODEL), lambda j: (0, 0)),
            pl.BlockSpec((nseq, CONV_W - 1, D_MODEL), lambda j: (0, 0, 0)),
        ),
        out_shape=out_shape,
        scratch_shapes=[
            pltpu.VMEM((D_MODEL, nseq), F32),
            pltpu.VMEM((D_MODEL, nseq), F32),
            pltpu.VMEM((nseq, D_MODEL), F32),
            pltpu.VMEM((nseq, D_MODEL), F32),
            pltpu.VMEM((nseq, D_MODEL), F32),
            pltpu.VMEM((nseq, D_MODEL), F32),
            pltpu.VMEM((nseq, D_MODEL), F32),
        ],
        name="mix_sample",
        compiler_params=pltpu.CompilerParams(
            dimension_semantics=("arbitrary",), vmem_limit_bytes=VMEM_LIMIT_BYTES),
    )(x2d, p["mix_norm"], p["w_in"], p["lb_raw"], p["o_norm"], p["conv_w"], p["conv_b"],
      p["w_ax"], p["b_a"], p["b_x"], p["lam"], p["w_a_up"], p["w_b_up"], p["w_out"],
      s_in, h_in, c_in)


def kernel(x_prompt, x_sample, state_hgrn, state_lru, state_conv, ffn1_norm, ffn1_w_gate, ffn1_w_up, ffn1_w_down, mix_norm, w_in, hgrn_lower_bounds, hgrn_out_norm, conv_w, conv_b, lru_w_a, lru_b_a, lru_w_x, lru_b_x, lru_lambda, w_a_up, w_b_up, w_out, ffn2_norm, ffn2_w_gate, ffn2_w_up, ffn2_w_down, final_norm):
    depth = w_in.shape[0]
    assert depth == 1, "single-layer trunk"
    l = 0
    row = lambda v: v.reshape(1, -1).astype(F32)
    bf = lambda w: w.astype(BF16)
    p = {
        "mix_norm": row(mix_norm[l]),
        "w_in": bf(w_in[l]),
        "lb_raw": hgrn_lower_bounds.astype(F32),
        "o_norm": row(hgrn_out_norm[l]),
        "conv_w": conv_w[l].astype(F32),
        "conv_b": row(conv_b[l]),
        "w_ax": bf(jnp.concatenate([lru_w_a[l], lru_w_x[l]], axis=-1)),
        "b_a": row(lru_b_a[l]),
        "b_x": row(lru_b_x[l]),
        "lam": row(lru_lambda[l]),
        "w_a_up": bf(w_a_up[l]),
        "w_b_up": bf(w_b_up[l]),
        "w_out": bf(w_out[l]),
    }
    f1 = (row(ffn1_norm[l]), bf(ffn1_w_gate[l]), bf(ffn1_w_up[l]), bf(ffn1_w_down[l]))
    f2 = (row(ffn2_norm[l]), bf(ffn2_w_gate[l]), bf(ffn2_w_up[l]), bf(ffn2_w_down[l]))
    gfin = row(final_norm)

    bp, tp, _ = x_prompt.shape
    xp = _ffn(x_prompt.reshape(bp * tp, D_MODEL), *f1, gfin, final_norm=False)
    xp, s_p, h_p, c_p = _mix_prompt(xp.reshape(bp, tp, D_MODEL), p)
    y_p = _ffn(xp.reshape(bp * tp, D_MODEL), *f2, gfin, final_norm=True).reshape(bp, tp, D_MODEL)

    bs, ts, _ = x_sample.shape
    assert ts == 1
    xs = _ffn(x_sample.reshape(bs, D_MODEL), *f1, gfin, final_norm=False)
    xs, s_s, h_s, c_s = _mix_sample(xs, state_hgrn[l], state_lru[l], state_conv[l], p)
    y_s = _ffn(xs, *f2, gfin, final_norm=True).reshape(bs, ts, D_MODEL)

    return (y_p, y_s, s_p[None], h_p.reshape(bp, D_MODEL)[None], c_p[None],
            s_s[None], h_s[None], c_s[None])
```

```python
import functools

import jax
import jax.numpy as jnp
from jax import lax
from jax.experimental import pallas as pl
from jax.experimental.pallas import tpu as pltpu

D_MODEL = 1024
HEADS = 8
DK = 128
DV = 128
CHUNK = 64
LRU_BLOCKS = 8
LRU_BW = 128
LRU_C = 8.0
CONV_W = 4
D_FF = 2816
EPS = 1e-6
LOG2_E = 1.4426950408889634
N_SPLITS = 8

SUBLANES = 8
LANES = 128
MXU_COLS = 256
VMEM_LIMIT_BYTES = 56 * 1024 * 1024

FFN_ROWS = 512
MIX_ROWS = 256
SAMPLE_STATE_BLOCK = 8

BF16 = jnp.bfloat16
F32 = jnp.float32


def _dot(a, b):
    return jnp.dot(a, b, preferred_element_type=F32)


def _dot_tn(a, b):
    return lax.dot_general(a, b, (((0,), (0,)), ((), ())), preferred_element_type=F32)


def _dot_nt(a, b):
    return lax.dot_general(a, b, (((1,), (1,)), ((), ())), preferred_element_type=F32)


def _sigmoid(x):
    return 1.0 / (1.0 + jnp.exp2(x * (-LOG2_E)))


def _silu(x):
    return x * _sigmoid(x)


def _gelu_tanh(x):
    c = 0.7978845608028654
    return 0.5 * x * (1.0 + jnp.tanh(c * (x + 0.044715 * (x * x * x))))


def _rms(x, g):
    return x * lax.rsqrt(jnp.mean(x * x, axis=-1, keepdims=True) + EPS) * g


def _lower_bound(raw, layer):
    m = jnp.max(raw, axis=0, keepdims=True)
    e = jnp.exp(raw - m)
    den = jnp.sum(e, axis=0, keepdims=True)
    num = jnp.sum(e[0:layer + 1], axis=0, keepdims=True)
    return num / den


def _softplus(x):
    return jnp.maximum(x, 0.0) + jnp.log1p(jnp.exp(-jnp.abs(x)))


def _lru_coeffs(r_pre, i_pre, xc, lam):
    r = _sigmoid(r_pre)
    ig = _sigmoid(i_pre)
    log_a = (-LRU_C) * r * _softplus(-lam)
    a = jnp.exp(log_a)
    th = jnp.tanh(log_a)
    mult = jnp.sqrt(-2.0 * th / (1.0 - th))
    return a, mult * (ig * xc)


def _split3(x):
    p1 = x.astype(BF16)
    r1 = x - p1.astype(F32)
    p2 = r1.astype(BF16)
    r2 = r1 - p2.astype(F32)
    return p1, p2, r2.astype(BF16)


def _ffn_kernel(xa_ref, xb_ref, g_ref, wg_ref, wu_ref, wd_ref, gf_ref, oa_ref, ob_ref, *,
                final_norm, a_tiles):
    def half_step(x_ref, o_ref):
        x = x_ref[...]
        h = _rms(x, g_ref[...]).astype(BF16)
        gate = _dot(h, wg_ref[...])
        up = _dot(h, wu_ref[...])
        act = (_silu(gate) * up).astype(BF16)
        y = x + 0.5 * _dot(act, wd_ref[...])
        if final_norm:
            y = _rms(y, gf_ref[...])
        o_ref[...] = y

    i = pl.program_id(0)

    @pl.when(i < a_tiles)
    def _():
        half_step(xa_ref, oa_ref)

    @pl.when(i == a_tiles)
    def _():
        half_step(xb_ref, ob_ref)


def _resident(shape):
    nd = len(shape)
    return pl.BlockSpec(shape, lambda *_: (0,) * nd, pipeline_mode=pl.Buffered(1))


def _ffn(xa, xb, g, wg, wu, wd, gf, *, final_norm):
    na, nb = xa.shape[0], xb.shape[0]
    assert na % FFN_ROWS == 0 and nb <= FFN_ROWS and nb % SUBLANES == 0
    a_tiles = na // FFN_ROWS
    a_tile = lambda i: (jnp.minimum(i, a_tiles - 1), 0)
    b_whole = lambda i: (0, 0)
    return pl.pallas_call(
        functools.partial(_ffn_kernel, final_norm=final_norm, a_tiles=a_tiles),
        grid=(a_tiles + 1,),
        in_specs=[
            pl.BlockSpec((FFN_ROWS, D_MODEL), a_tile),
            pl.BlockSpec((nb, D_MODEL), b_whole),
            _resident((1, D_MODEL)),
            _resident((D_MODEL, D_FF)),
            _resident((D_MODEL, D_FF)),
            _resident((D_FF, D_MODEL)),
            _resident((1, D_MODEL)),
        ],
        out_specs=(pl.BlockSpec((FFN_ROWS, D_MODEL), a_tile),
                   pl.BlockSpec((nb, D_MODEL), b_whole)),
        out_shape=(jax.ShapeDtypeStruct((na, D_MODEL), F32),
                   jax.ShapeDtypeStruct((nb, D_MODEL), F32)),
        name="ffn_final" if final_norm else "ffn",
        compiler_params=pltpu.CompilerParams(
            dimension_semantics=("arbitrary",), vmem_limit_bytes=VMEM_LIMIT_BYTES),
    )(xa, xb, g, wg, wu, wd, gf)


SCAN_SEG = 4
SCAN_GROUP = SCAN_SEG * SUBLANES
CONV_PAD = SUBLANES
PIECE_ROWS = 2 * SUBLANES
QA_W = 2 * LANES


def _mix_prompt_kernel(x_ref, nmix_ref, win_ref, lbraw_ref, onorm_ref, convw_ref, convb_ref,
                       wax_ref, ba_ref, bx_ref, lam_ref, waup_ref, wbup_ref, wout_ref,
                       xo_ref, s_ref, hl_ref, cv_ref,
                       qa_s, ud_s, vb_s, oa_s, xr_s, h_s):
    rows = MIX_ROWS
    t = pl.program_id(1)

    @pl.when(t == 0)
    def _():
        s_ref[...] = jnp.zeros_like(s_ref)
        hl_ref[...] = jnp.zeros_like(hl_ref)
        xr_s[:, 0:CONV_PAD, :] = jnp.zeros((LRU_BLOCKS, CONV_PAD, LRU_BW), F32)

    x = x_ref[0]
    hb = _rms(x, nmix_ref[...]).astype(BF16)

    def proj(i):
        return _dot(hb, win_ref[:, i * D_MODEL:(i + 1) * D_MODEL])

    early = {}

    ncb = D_MODEL // MXU_COLS

    def issue_proj(i, cb):
        if cb < ncb and (i, cb) not in early:
            c0 = i * D_MODEL + cb * MXU_COLS
            early[(i, cb)] = _dot(hb, win_ref[:, c0:c0 + MXU_COLS])

    def gather(i):
        for cb in range(ncb):
            issue_proj(i, cb)
        return jnp.concatenate([early[(i, cb)] for cb in range(ncb)], axis=1)

    lb = _lower_bound(lbraw_ref[...], 0)
    q_act = _silu(proj(0))
    f = lb + (1.0 - lb) * _sigmoid(proj(1))
    kk = 1.0 - f
    vb_s[...] = proj(2).astype(BF16)

    ri = lax.broadcasted_iota(jnp.int32, (rows, rows), 0)
    ci = lax.broadcasted_iota(jnp.int32, (rows, rows), 1)
    same_chunk = (ri // CHUNK) == (ci // CHUNK)
    tril_blk = jnp.where((ri >= ci) & same_chunk, 1.0, 0.0).astype(BF16)
    g1, g2, g3 = _split3(jnp.log(f))
    b_all = _dot(tril_blk, g1) + _dot(tril_blk, g2) + _dot(tril_blk, g3)

    causal = (lax.broadcasted_iota(jnp.int32, (CHUNK, CHUNK), 0)
              >= lax.broadcasted_iota(jnp.int32, (CHUNK, CHUNK), 1))
    piece_row = lax.broadcasted_iota(jnp.int32, (PIECE_ROWS, D_MODEL), 0)
    ones_pv = jnp.ones((3 * PIECE_ROWS, DV), BF16)
    zeros_pv = jnp.zeros((3 * PIECE_ROWS, DV), BF16)
    zeros_cv = jnp.zeros((CHUNK, DV), BF16)

    nchunk = rows // CHUNK
    last = piece_row == PIECE_ROWS - 1
    zero_p = jnp.zeros((PIECE_ROWS, D_MODEL), BF16)

    for c in range(nchunk):
        issue_proj(3, c)
        rs = slice(c * CHUNK, (c + 1) * CHUNK)
        b = b_all[rs, :]
        bl = b[CHUNK - 1:CHUNK, :]
        qe = (q_act[rs, :] * jnp.exp(b)).astype(BF16)
        ke = (kk[rs, :] * jnp.exp(-b)).astype(BF16)
        kd = (kk[rs, :] * jnp.exp(bl - b)).astype(BF16)
        pieces = jnp.concatenate(
            [jnp.where(last, e, zero_p) for e in _split3(jnp.exp(b[CHUNK - PIECE_ROWS:, :]))], axis=0)
        for h in range(HEADS):
            hs = slice(h * DK, (h + 1) * DK)
            qa_s[rs, h * QA_W:h * QA_W + DK] = qe[:, hs]
            att = jnp.where(causal, _dot_nt(qe[:, hs], ke[:, hs]), 0.0)
            qa_s[rs, h * QA_W + DK:h * QA_W + DK + CHUNK] = att.astype(BF16)
            lhs = jnp.concatenate([kd[:, hs], pieces[:, hs]], axis=0)
            rhs = jnp.concatenate([jnp.concatenate([vb_s[rs, hs], zeros_cv], axis=1),
                                   jnp.concatenate([zeros_pv, ones_pv], axis=1)], axis=0)
            ud_s[c * HEADS + h] = _dot_tn(lhs, rhs)

    for c in range(nchunk):
        rs = slice(c * CHUNK, (c + 1) * CHUNK)
        for h in range(HEADS):
            hs = slice(h * DK, (h + 1) * DK)
            s_old = s_ref[0, h]
            rhs = jnp.concatenate([s_old.astype(BF16), vb_s[rs, hs]], axis=0)
            o = _dot(qa_s[rs, h * QA_W:h * QA_W + DK + CHUNK], rhs)
            oa_s[rs, hs] = o * lax.rsqrt(jnp.mean(o * o, axis=-1, keepdims=True) + EPS)
            ud = ud_s[c * HEADS + h]
            s_ref[0, h] = ud[:, DV:] * s_old + ud[:, 0:DV]

    o_a = (oa_s[...] * onorm_ref[...] * _silu(gather(3))).astype(BF16)
    up_a = {}

    def issue_up_a(cb):
        if cb < ncb and cb not in up_a:
            up_a[cb] = _dot(o_a, waup_ref[:, cb * MXU_COLS:(cb + 1) * MXU_COLS])

    xr = proj(4)
    h_prev = hl_ref[0]
    sub = lax.broadcasted_iota(jnp.int32, (SUBLANES, LANES), 0)
    ngroup = rows // SCAN_GROUP
    tails = []
    h_last = []
    for n in range(LRU_BLOCKS):
        ns = slice(n * LRU_BW, (n + 1) * LRU_BW)
        xr_s[n, CONV_PAD:CONV_PAD + rows, :] = xr[:, ns]
        taps = [jnp.broadcast_to(convw_ref[j:j + 1, ns], (SUBLANES, LRU_BW)) for j in range(CONV_W)]
        bias = jnp.broadcast_to(convb_ref[:, ns], (SUBLANES, LRU_BW))
        xc_parts = []
        for grp in range(ngroup):
            xin = {d: xr_s[n, pl.ds(CONV_PAD + grp * SCAN_GROUP + d, SUBLANES, stride=SCAN_SEG), :]
                   for d in range(1 - CONV_W, SCAN_SEG)}
            for j in range(SCAN_SEG):
                acc = bias
                for tap in range(CONV_W):
                    acc = acc + xin[j + tap - (CONV_W - 1)] * taps[tap]
                xc_parts.append(acc)
        xc = jnp.concatenate(xc_parts, axis=0)
        tail = xr_s[n, CONV_PAD + rows - (CONV_W - 1):CONV_PAD + rows, :]
        tails.append(tail)
        xr_s[n, CONV_PAD - (CONV_W - 1):CONV_PAD, :] = tail
        pre = _dot(xc.astype(BF16), wax_ref[n])
        issue_proj(5 + n % 2, n // 2)
        a, bterm = _lru_coeffs(pre[:, :LRU_BW] + ba_ref[:, ns], pre[:, LRU_BW:] + bx_ref[:, ns],
                               xc, lam_ref[:, ns])
        carry = h_prev[:, ns]
        for grp in range(ngroup):
            def vreg(v, j, grp=grp):
                r0 = grp * SCAN_GROUP + j * SUBLANES
                return v[r0:r0 + SUBLANES, :]

            def strided(j, grp=grp):
                return pl.ds(grp * SCAN_GROUP + j, SUBLANES, stride=SCAN_SEG)
            pa = [vreg(a, 0)]
            hh = [vreg(bterm, 0)]
            for j in range(1, SCAN_SEG):
                aj = vreg(a, j)
                hh.append(aj * hh[-1] + vreg(bterm, j))
                pa.append(aj * pa[-1])
            pi, hi = pa[-1], hh[-1]
            for d in (1, 2, 4):
                keep = sub >= d
                hi = pi * jnp.where(keep, pltpu.roll(hi, d, 0), 0.0) + hi
                pi = pi * jnp.where(keep, pltpu.roll(pi, d, 0), 1.0)
            first = sub >= 1
            seg_in = (jnp.where(first, pltpu.roll(pi, 1, 0), 1.0) * carry
                      + jnp.where(first, pltpu.roll(hi, 1, 0), 0.0))
            carry = pi[SUBLANES - 1:, :] * carry + hi[SUBLANES - 1:, :]
            for j in range(SCAN_SEG):
                h_s[n, strided(j), :] = pa[j] * seg_in + hh[j]
        h_last.append(carry)
        if n % 2 == 0:
            issue_proj(7, n // 2)
        else:
            issue_up_a(n // 2)
    hl_ref[0] = jnp.concatenate(h_last, axis=1)
    cv_ref[0] = jnp.concatenate(tails, axis=1)
    gel = _gelu_tanh(gather(5))
    o_b = jnp.concatenate(
        [h_s[n] * gel[:, n * LRU_BW:(n + 1) * LRU_BW] for n in range(LRU_BLOCKS)], axis=1).astype(BF16)

    for cb in range(ncb):
        issue_up_a(cb)
    m = (_sigmoid(gather(6)) * jnp.concatenate([up_a[cb] for cb in range(ncb)], axis=1)
         + _sigmoid(gather(7)) * _dot(o_b, wbup_ref[...]))
    xo_ref[0] = x + _dot(m.astype(BF16), wout_ref[...])


def _mix_prompt(x, p):
    batch, seq, _ = x.shape
    rows = MIX_ROWS
    assert seq % rows == 0 and rows % CHUNK == 0 and rows >= CONV_W - 1
    row_vec = _resident((1, D_MODEL))
    out_shape = (
        jax.ShapeDtypeStruct((batch, seq, D_MODEL), F32),
        jax.ShapeDtypeStruct((batch, HEADS, DK, DV), F32),
        jax.ShapeDtypeStruct((batch, 1, D_MODEL), F32),
        jax.ShapeDtypeStruct((batch, CONV_W - 1, D_MODEL), F32),
    )
    return pl.pallas_call(
        _mix_prompt_kernel,
        grid=(batch, seq // rows),
        in_specs=[
            pl.BlockSpec((1, rows, D_MODEL), lambda b, t: (b, t, 0)),
            row_vec,
            _resident((D_MODEL, N_SPLITS * D_MODEL)),
            _resident(p["lb_raw"].shape),
            row_vec,
            _resident((CONV_W, D_MODEL)),
            row_vec,
            _resident((LRU_BLOCKS, LRU_BW, 2 * LRU_BW)),
            row_vec, row_vec, row_vec,
            _resident((D_MODEL, D_MODEL)),
            _resident((D_MODEL, D_MODEL)),
            _resident((D_MODEL, D_MODEL)),
        ],
        out_specs=(
            pl.BlockSpec((1, rows, D_MODEL), lambda b, t: (b, t, 0)),
            pl.BlockSpec((1, HEADS, DK, DV), lambda b, t: (b, 0, 0, 0)),
            pl.BlockSpec((1, 1, D_MODEL), lambda b, t: (b, 0, 0)),
            pl.BlockSpec((1, CONV_W - 1, D_MODEL), lambda b, t: (b, 0, 0)),
        ),
        out_shape=out_shape,
        scratch_shapes=[
            pltpu.VMEM((rows, HEADS * QA_W), BF16),
            pltpu.VMEM((rows // CHUNK * HEADS, DK, 2 * DV), F32),
            pltpu.VMEM((rows, D_MODEL), BF16),
            pltpu.VMEM((rows, D_MODEL), F32),
            pltpu.VMEM((LRU_BLOCKS, CONV_PAD + rows, LRU_BW), F32),
            pltpu.VMEM((LRU_BLOCKS, rows, LRU_BW), F32),
        ],
        name="mix_prompt",
        compiler_params=pltpu.CompilerParams(
            dimension_semantics=("arbitrary", "arbitrary"), vmem_limit_bytes=VMEM_LIMIT_BYTES),
    )(x, p["mix_norm"], p["w_in"], p["lb_raw"], p["o_norm"], p["conv_w"], p["conv_b"],
      p["w_ax"], p["b_a"], p["b_x"], p["lam"], p["w_a_up"], p["w_b_up"], p["w_out"])


def _mix_sample_kernel(x_ref, nmix_ref, win_ref, lbraw_ref, onorm_ref, convw_ref, convb_ref,
                       wax_ref, ba_ref, bx_ref, lam_ref, waup_ref, wbup_ref, wout_ref,
                       sin_ref, hlin_ref, cvin_ref,
                       xo_ref, sout_ref, hl_ref, cv_ref,
                       qt_s, ft_s, v_s, oa_s, gsil_s, ga_s, mb_s):
    nseq = x_ref.shape[0]
    blk = SAMPLE_STATE_BLOCK
    j = pl.program_id(0)

    @pl.when(j == 0)
    def _():
        x = x_ref[...]
        hb = _rms(x, nmix_ref[...]).astype(BF16)

        def proj(i):
            return _dot(hb, win_ref[:, i * D_MODEL:(i + 1) * D_MODEL])

        lb = _lower_bound(lbraw_ref[...], 0)
        qt_s[...] = _silu(proj(0)).T
        ft_s[...] = (lb + (1.0 - lb) * _sigmoid(proj(1))).T
        v_s[...] = proj(2)
        gsil_s[...] = onorm_ref[...] * _silu(proj(3))

        xr = proj(4)
        xc = convb_ref[...] + xr * convw_ref[CONV_W - 1:CONV_W, :]
        for i in range(CONV_W - 1):
            xc = xc + cvin_ref[:, i, :] * convw_ref[i:i + 1, :]
        for i in range(CONV_W - 2):
            cv_ref[:, i, :] = cvin_ref[:, i + 1, :]
        cv_ref[:, CONV_W - 2, :] = xr

        xcb = xc.astype(BF16)
        h_parts = []
        for n in range(LRU_BLOCKS):
            ns = slice(n * LRU_BW, (n + 1) * LRU_BW)
            pre = _dot(xcb[:, ns], wax_ref[n])
            a, bterm = _lru_coeffs(pre[:, :LRU_BW] + ba_ref[:, ns], pre[:, LRU_BW:] + bx_ref[:, ns],
                                   xc[:, ns], lam_ref[:, ns])
            h_parts.append(a * hlin_ref[:, ns] + bterm)
        h_new = jnp.concatenate(h_parts, axis=1)
        hl_ref[...] = h_new
        o_b = (h_new * _gelu_tanh(proj(5))).astype(BF16)
        ga_s[...] = _sigmoid(proj(6))
        mb_s[...] = _sigmoid(proj(7)) * _dot(o_b, wbup_ref[...])

    b0 = pl.multiple_of(j * blk, blk)
    shifts = [lax.rem(nseq - b0 + p * blk, nseq) for p in range(3)]

    lane = lax.broadcasted_iota(jnp.int32, (DK, nseq), 1)
    sel_r = lax.broadcasted_iota(jnp.int32, (nseq, blk * DV), 0)
    sel_c = lax.broadcasted_iota(jnp.int32, (nseq, blk * DV), 1)
    selector = jnp.where((sel_r < 3 * blk) & (jnp.bitwise_and(sel_r, blk - 1) == sel_c // DV),
                         1.0, 0.0).astype(BF16)

    def spread(cols_t):
        p1 = cols_t.astype(BF16).astype(F32)
        r1 = cols_t - p1
        p2 = r1.astype(BF16).astype(F32)
        p3 = r1 - p2
        lhs = jnp.where(lane < blk, pltpu.roll(p1, shifts[0], 1),
                        jnp.where(lane < 2 * blk, pltpu.roll(p2, shifts[1], 1),
                                  jnp.where(lane < 3 * blk, pltpu.roll(p3, shifts[2], 1), 0.0)))
        return _dot(lhs.astype(BF16), selector)

    for h in range(HEADS):
        hrows = slice(h * DK, (h + 1) * DK)
        q_all = spread(qt_s[hrows, :])
        f_all = spread(ft_s[hrows, :])
        v_rows = v_s[pl.ds(b0, blk), hrows]
        o_rows = []
        for i in range(blk):
            f_col = f_all[:, i * DV:(i + 1) * DV]
            s_new = f_col * sin_ref[i, h] + (1.0 - f_col) * v_rows[i:i + 1, :]
            sout_ref[i, h] = s_new
            o_rows.append(jnp.sum(q_all[:, i * DV:(i + 1) * DV] * s_new, axis=0, keepdims=True))
        o = jnp.concatenate(o_rows, axis=0)
        oa_s[pl.ds(b0, blk), hrows] = o * lax.rsqrt(
            jnp.mean(o * o, axis=-1, keepdims=True) + EPS)

    @pl.when(j == pl.num_programs(0) - 1)
    def _():
        o_a = (oa_s[...] * gsil_s[...]).astype(BF16)
        m = ga_s[...] * _dot(o_a, waup_ref[...]) + mb_s[...]
        xo_ref[...] = x_ref[...] + _dot(m.astype(BF16), wout_ref[...])


def _mix_sample(x2d, s_in, h_in, c_in, p):
    nseq = x2d.shape[0]
    blk = SAMPLE_STATE_BLOCK
    assert nseq == LANES and nseq % blk == 0
    row_vec = _resident((1, D_MODEL))
    tok = _resident((nseq, D_MODEL))
    conv_state = _resident((nseq, CONV_W - 1, D_MODEL))
    out_shape = (
        jax.ShapeDtypeStruct((nseq, D_MODEL), F32),
        jax.ShapeDtypeStruct((nseq, HEADS, DK, DV), F32),
        jax.ShapeDtypeStruct((nseq, D_MODEL), F32),
        jax.ShapeDtypeStruct((nseq, CONV_W - 1, D_MODEL), F32),
    )
    state_spec = pl.BlockSpec((blk, HEADS, DK, DV), lambda j: (j, 0, 0, 0))
    return pl.pallas_call(
        _mix_sample_kernel,
        grid=(nseq // blk,),
        in_specs=[
            tok, row_vec,
            _resident((D_MODEL, N_SPLITS * D_MODEL)),
            _resident(p["lb_raw"].shape),
            row_vec,
            _resident((CONV_W, D_MODEL)),
            row_vec,
            _resident((LRU_BLOCKS, LRU_BW, 2 * LRU_BW)),
            row_vec, row_vec, row_vec,
            _resident((D_MODEL, D_MODEL)),
            _resident((D_MODEL, D_MODEL)),
            _resident((D_MODEL, D_MODEL)),
            state_spec, tok, conv_state,
        ],
        out_specs=(
            pl.BlockSpec((nseq, D_MODEL), lambda j: (0, 0)),
            state_spec,
            pl.BlockSpec((nseq, D_MODEL), lambda j: (0, 0)),
            pl.BlockSpec((nseq, CONV_W - 1, D_MODEL), lambda j: (0, 0, 0)),
        ),
        out_shape=out_shape,
        scratch_shapes=[
            pltpu.VMEM((D_MODEL, nseq), F32),
            pltpu.VMEM((D_MODEL, nseq), F32),
            pltpu.VMEM((nseq, D_MODEL), F32),
            pltpu.VMEM((nseq, D_MODEL), F32),
            pltpu.VMEM((nseq, D_MODEL), F32),
            pltpu.VMEM((nseq, D_MODEL), F32),
            pltpu.VMEM((nseq, D_MODEL), F32),
        ],
        name="mix_sample",
        compiler_params=pltpu.CompilerParams(
            dimension_semantics=("arbitrary",), vmem_limit_bytes=VMEM_LIMIT_BYTES),
    )(x2d, p["mix_norm"], p["w_in"], p["lb_raw"], p["o_norm"], p["conv_w"], p["conv_b"],
      p["w_ax"], p["b_a"], p["b_x"], p["lam"], p["w_a_up"], p["w_b_up"], p["w_out"],
      s_in, h_in, c_in)


def kernel(x_prompt, x_sample, state_hgrn, state_lru, state_conv, ffn1_norm, ffn1_w_gate, ffn1_w_up, ffn1_w_down, mix_norm, w_in, hgrn_lower_bounds, hgrn_out_norm, conv_w, conv_b, lru_w_a, lru_b_a, lru_w_x, lru_b_x, lru_lambda, w_a_up, w_b_up, w_out, ffn2_norm, ffn2_w_gate, ffn2_w_up, ffn2_w_down, final_norm):
    depth = w_in.shape[0]
    assert depth == 1, "single-layer trunk"
    l = 0
    row = lambda v: v.reshape(1, -1).astype(F32)
    bf = lambda w: w.astype(BF16)
    p = {
        "mix_norm": row(mix_norm[l]),
        "w_in": bf(w_in[l]),
        "lb_raw": hgrn_lower_bounds.astype(F32),
        "o_norm": row(hgrn_out_norm[l]),
        "conv_w": conv_w[l].astype(F32),
        "conv_b": row(conv_b[l]),
        "w_ax": bf(jnp.concatenate([lru_w_a[l], lru_w_x[l]], axis=-1)),
        "b_a": row(lru_b_a[l]),
        "b_x": row(lru_b_x[l]),
        "lam": row(lru_lambda[l]),
        "w_a_up": bf(w_a_up[l]),
        "w_b_up": bf(w_b_up[l]),
        "w_out": bf(w_out[l]),
    }
    f1 = (row(ffn1_norm[l]), bf(ffn1_w_gate[l]), bf(ffn1_w_up[l]), bf(ffn1_w_down[l]))
    f2 = (row(ffn2_norm[l]), bf(ffn2_w_gate[l]), bf(ffn2_w_up[l]), bf(ffn2_w_down[l]))
    gfin = row(final_norm)

    bp, tp, _ = x_prompt.shape
    bs, ts, _ = x_sample.shape
    assert ts == 1
    xp, xs = _ffn(x_prompt.reshape(bp * tp, D_MODEL), x_sample.reshape(bs, D_MODEL),
                  *f1, gfin, final_norm=False)
    xp, s_p, h_p, c_p = _mix_prompt(xp.reshape(bp, tp, D_MODEL), p)
    xs, s_s, h_s, c_s = _mix_sample(xs, state_hgrn[l], state_lru[l], state_conv[l], p)
    y_p, y_s = _ffn(xp.reshape(bp * tp, D_MODEL), xs, *f2, gfin, final_norm=True)
    y_p = y_p.reshape(bp, tp, D_MODEL)
    y_s = y_s.reshape(bs, ts, D_MODEL)

    return (y_p, y_s, s_p[None], h_p.reshape(bp, D_MODEL)[None], c_p[None],
            s_s[None], h_s[None], c_s[None])
```

```python
import functools

import jax
import jax.numpy as jnp
from jax import lax
from jax.experimental import pallas as pl
from jax.experimental.pallas import tpu as pltpu

D_MODEL = 1024
HEADS = 8
DK = 128
DV = 128
CHUNK = 64
LRU_BLOCKS = 8
LRU_BW = 128
LRU_C = 8.0
CONV_W = 4
D_FF = 2816
EPS = 1e-6
LOG2_E = 1.4426950408889634
N_SPLITS = 8

SUBLANES = 8
LANES = 128
MXU_COLS = 256
VMEM_LIMIT_BYTES = 56 * 1024 * 1024

FFN_ROWS = 512
MIX_ROWS = 256
SAMPLE_STATE_BLOCK = 8

BF16 = jnp.bfloat16
F32 = jnp.float32


def _dot(a, b):
    return jnp.dot(a, b, preferred_element_type=F32)


def _dot_tn(a, b):
    return lax.dot_general(a, b, (((0,), (0,)), ((), ())), preferred_element_type=F32)


def _dot_nt(a, b):
    return lax.dot_general(a, b, (((1,), (1,)), ((), ())), preferred_element_type=F32)


def _sigmoid(x):
    return 1.0 / (1.0 + jnp.exp2(x * (-LOG2_E)))


def _silu(x):
    return x * _sigmoid(x)


def _gelu_tanh(x):
    c = 0.7978845608028654
    return 0.5 * x * (1.0 + jnp.tanh(c * (x + 0.044715 * (x * x * x))))


def _rms(x, g):
    return x * lax.rsqrt(jnp.mean(x * x, axis=-1, keepdims=True) + EPS) * g


def _lower_bound(raw, layer):
    m = jnp.max(raw, axis=0, keepdims=True)
    e = jnp.exp(raw - m)
    den = jnp.sum(e, axis=0, keepdims=True)
    num = jnp.sum(e[0:layer + 1], axis=0, keepdims=True)
    return num / den


def _softplus(x):
    return jnp.maximum(x, 0.0) + jnp.log1p(jnp.exp(-jnp.abs(x)))


def _lru_coeffs(r_pre, i_pre, xc, lam):
    r = _sigmoid(r_pre)
    ig = _sigmoid(i_pre)
    log_a = (-LRU_C) * r * _softplus(-lam)
    a = jnp.exp(log_a)
    th = jnp.tanh(log_a)
    mult = jnp.sqrt(-2.0 * th / (1.0 - th))
    return a, mult * (ig * xc)


def _split3(x):
    p1 = x.astype(BF16)
    r1 = x - p1.astype(F32)
    p2 = r1.astype(BF16)
    r2 = r1 - p2.astype(F32)
    return p1, p2, r2.astype(BF16)


def _ffn_kernel(xa_ref, xb_ref, g_ref, wg_ref, wu_ref, wd_ref, gf_ref, oa_ref, ob_ref, *,
                final_norm, a_tiles):
    def half_step(x_ref, o_ref):
        x = x_ref[...]
        inv_rms = lax.rsqrt(jnp.mean(x * x, axis=-1, keepdims=True) + EPS)
        h = (x * g_ref[...]).astype(BF16)
        gate = _dot(h, wg_ref[...]) * inv_rms
        up = _dot(h, wu_ref[...]) * inv_rms
        act = (_silu(gate) * up).astype(BF16)
        y = x + 0.5 * _dot(act, wd_ref[...])
        if final_norm:
            y = _rms(y, gf_ref[...])
        o_ref[...] = y

    i = pl.program_id(0)

    @pl.when(i < a_tiles)
    def _():
        half_step(xa_ref, oa_ref)

    @pl.when(i == a_tiles)
    def _():
        half_step(xb_ref, ob_ref)


def _resident(shape):
    nd = len(shape)
    return pl.BlockSpec(shape, lambda *_: (0,) * nd, pipeline_mode=pl.Buffered(1))


def _ffn(xa, xb, g, wg, wu, wd, gf, *, final_norm):
    na, nb = xa.shape[0], xb.shape[0]
    assert na % FFN_ROWS == 0 and nb <= FFN_ROWS and nb % SUBLANES == 0
    a_tiles = na // FFN_ROWS
    a_tile = lambda i: (jnp.minimum(i, a_tiles - 1), 0)
    b_whole = lambda i: (0, 0)
    return pl.pallas_call(
        functools.partial(_ffn_kernel, final_norm=final_norm, a_tiles=a_tiles),
        grid=(a_tiles + 1,),
        in_specs=[
            pl.BlockSpec((FFN_ROWS, D_MODEL), a_tile),
            pl.BlockSpec((nb, D_MODEL), b_whole),
            _resident((1, D_MODEL)),
            _resident((D_MODEL, D_FF)),
            _resident((D_MODEL, D_FF)),
            _resident((D_FF, D_MODEL)),
            _resident((1, D_MODEL)),
        ],
        out_specs=(pl.BlockSpec((FFN_ROWS, D_MODEL), a_tile),
                   pl.BlockSpec((nb, D_MODEL), b_whole)),
        out_shape=(jax.ShapeDtypeStruct((na, D_MODEL), F32),
                   jax.ShapeDtypeStruct((nb, D_MODEL), F32)),
        name="ffn_final" if final_norm else "ffn",
        compiler_params=pltpu.CompilerParams(
            dimension_semantics=("arbitrary",), vmem_limit_bytes=VMEM_LIMIT_BYTES),
    )(xa, xb, g, wg, wu, wd, gf)


SCAN_SEG = 4
SCAN_GROUP = SCAN_SEG * SUBLANES
CONV_PAD = SUBLANES
PIECE_ROWS = 2 * SUBLANES
QA_W = 2 * LANES


def _mix_prompt_kernel(x_ref, nmix_ref, win_ref, lbraw_ref, onorm_ref, convw_ref, convb_ref,
                       wax_ref, ba_ref, bx_ref, lam_ref, waup_ref, wbup_ref, wout_ref,
                       xo_ref, s_ref, hl_ref, cv_ref,
                       qa_s, ud_s, vb_s, oa_s, xr_s, h_s):
    rows = MIX_ROWS
    t = pl.program_id(1)

    @pl.when(t == 0)
    def _():
        s_ref[...] = jnp.zeros_like(s_ref)
        hl_ref[...] = jnp.zeros_like(hl_ref)
        xr_s[:, 0:CONV_PAD, :] = jnp.zeros((LRU_BLOCKS, CONV_PAD, LRU_BW), F32)

    x = x_ref[0]
    hb = _rms(x, nmix_ref[...]).astype(BF16)

    def proj(i):
        return _dot(hb, win_ref[:, i * D_MODEL:(i + 1) * D_MODEL])

    early = {}

    ncb = D_MODEL // MXU_COLS

    def issue_proj(i, cb):
        if cb < ncb and (i, cb) not in early:
            c0 = i * D_MODEL + cb * MXU_COLS
            early[(i, cb)] = _dot(hb, win_ref[:, c0:c0 + MXU_COLS])

    def gather(i):
        for cb in range(ncb):
            issue_proj(i, cb)
        return jnp.concatenate([early[(i, cb)] for cb in range(ncb)], axis=1)

    lb = _lower_bound(lbraw_ref[...], 0)
    q_act = _silu(proj(0))
    f = lb + (1.0 - lb) * _sigmoid(proj(1))
    kk = 1.0 - f
    vb_s[...] = proj(2).astype(BF16)

    ri = lax.broadcasted_iota(jnp.int32, (rows, rows), 0)
    ci = lax.broadcasted_iota(jnp.int32, (rows, rows), 1)
    same_chunk = (ri // CHUNK) == (ci // CHUNK)
    tril_blk = jnp.where((ri >= ci) & same_chunk, 1.0, 0.0).astype(BF16)
    logf = jnp.log(f)
    g1 = logf.astype(BF16)
    g2 = (logf - g1.astype(F32)).astype(BF16)
    b_all = _dot(tril_blk, g1) + _dot(tril_blk, g2)

    causal = (lax.broadcasted_iota(jnp.int32, (CHUNK, CHUNK), 0)
              >= lax.broadcasted_iota(jnp.int32, (CHUNK, CHUNK), 1))
    piece_row = lax.broadcasted_iota(jnp.int32, (PIECE_ROWS, D_MODEL), 0)
    ones_pv = jnp.ones((3 * PIECE_ROWS, DV), BF16)
    zeros_pv = jnp.zeros((3 * PIECE_ROWS, DV), BF16)
    zeros_cv = jnp.zeros((CHUNK, DV), BF16)

    nchunk = rows // CHUNK
    last = piece_row == PIECE_ROWS - 1
    zero_p = jnp.zeros((PIECE_ROWS, D_MODEL), BF16)

    for c in range(nchunk):
        issue_proj(3, c)
        rs = slice(c * CHUNK, (c + 1) * CHUNK)
        b = b_all[rs, :]
        bl = b[CHUNK - 1:CHUNK, :]
        qe = (q_act[rs, :] * jnp.exp(b)).astype(BF16)
        ke = (kk[rs, :] * jnp.exp(-b)).astype(BF16)
        kd = (kk[rs, :] * jnp.exp(bl - b)).astype(BF16)
        pieces = jnp.concatenate(
            [jnp.where(last, e, zero_p) for e in _split3(jnp.exp(b[CHUNK - PIECE_ROWS:, :]))], axis=0)
        for h in range(HEADS):
            hs = slice(h * DK, (h + 1) * DK)
            qa_s[rs, h * QA_W:h * QA_W + DK] = qe[:, hs]
            att = jnp.where(causal, _dot_nt(qe[:, hs], ke[:, hs]), 0.0)
            qa_s[rs, h * QA_W + DK:h * QA_W + DK + CHUNK] = att.astype(BF16)
            lhs = jnp.concatenate([kd[:, hs], pieces[:, hs]], axis=0)
            rhs = jnp.concatenate([jnp.concatenate([vb_s[rs, hs], zeros_cv], axis=1),
                                   jnp.concatenate([zeros_pv, ones_pv], axis=1)], axis=0)
            ud_s[c * HEADS + h] = _dot_tn(lhs, rhs)

    for c in range(nchunk):
        rs = slice(c * CHUNK, (c + 1) * CHUNK)
        for h in range(HEADS):
            hs = slice(h * DK, (h + 1) * DK)
            s_old = s_ref[0, h]
            rhs = jnp.concatenate([s_old.astype(BF16), vb_s[rs, hs]], axis=0)
            o = _dot(qa_s[rs, h * QA_W:h * QA_W + DK + CHUNK], rhs)
            oa_s[rs, hs] = o * lax.rsqrt(jnp.mean(o * o, axis=-1, keepdims=True) + EPS)
            ud = ud_s[c * HEADS + h]
            s_ref[0, h] = ud[:, DV:] * s_old + ud[:, 0:DV]

    o_a = (oa_s[...] * onorm_ref[...] * _silu(gather(3))).astype(BF16)
    up_a = {}

    def issue_up_a(cb):
        if cb < ncb and cb not in up_a:
            up_a[cb] = _dot(o_a, waup_ref[:, cb * MXU_COLS:(cb + 1) * MXU_COLS])

    xr = proj(4)
    h_prev = hl_ref[0]
    sub = lax.broadcasted_iota(jnp.int32, (SUBLANES, LANES), 0)
    ngroup = rows // SCAN_GROUP
    tails = []
    h_last = []
    for n in range(LRU_BLOCKS):
        ns = slice(n * LRU_BW, (n + 1) * LRU_BW)
        xr_s[n, CONV_PAD:CONV_PAD + rows, :] = xr[:, ns]
        taps = [jnp.broadcast_to(convw_ref[j:j + 1, ns], (SUBLANES, LRU_BW)) for j in range(CONV_W)]
        bias = jnp.broadcast_to(convb_ref[:, ns], (SUBLANES, LRU_BW))
        xc_parts = []
        for grp in range(ngroup):
            xin = {d: xr_s[n, pl.ds(CONV_PAD + grp * SCAN_GROUP + d, SUBLANES, stride=SCAN_SEG), :]
                   for d in range(1 - CONV_W, SCAN_SEG)}
            for j in range(SCAN_SEG):
                acc = bias
                for tap in range(CONV_W):
                    acc = acc + xin[j + tap - (CONV_W - 1)] * taps[tap]
                xc_parts.append(acc)
        xc = jnp.concatenate(xc_parts, axis=0)
        tail = xr_s[n, CONV_PAD + rows - (CONV_W - 1):CONV_PAD + rows, :]
        tails.append(tail)
        xr_s[n, CONV_PAD - (CONV_W - 1):CONV_PAD, :] = tail
        pre = _dot(xc.astype(BF16), wax_ref[n])
        issue_proj(5 + n % 2, n // 2)
        a, bterm = _lru_coeffs(pre[:, :LRU_BW] + ba_ref[:, ns], pre[:, LRU_BW:] + bx_ref[:, ns],
                               xc, lam_ref[:, ns])
        carry = h_prev[:, ns]
        for grp in range(ngroup):
            def vreg(v, j, grp=grp):
                r0 = grp * SCAN_GROUP + j * SUBLANES
                return v[r0:r0 + SUBLANES, :]

            def strided(j, grp=grp):
                return pl.ds(grp * SCAN_GROUP + j, SUBLANES, stride=SCAN_SEG)
            pa = [vreg(a, 0)]
            hh = [vreg(bterm, 0)]
            for j in range(1, SCAN_SEG):
                aj = vreg(a, j)
                hh.append(aj * hh[-1] + vreg(bterm, j))
                pa.append(aj * pa[-1])
            pi, hi = pa[-1], hh[-1]
            for d in (1, 2, 4):
                keep = sub >= d
                hi = pi * jnp.where(keep, pltpu.roll(hi, d, 0), 0.0) + hi
                pi = pi * jnp.where(keep, pltpu.roll(pi, d, 0), 1.0)
            first = sub >= 1
            seg_in = (jnp.where(first, pltpu.roll(pi, 1, 0), 1.0) * carry
                      + jnp.where(first, pltpu.roll(hi, 1, 0), 0.0))
            carry = pi[SUBLANES - 1:, :] * carry + hi[SUBLANES - 1:, :]
            for j in range(SCAN_SEG):
                h_s[n, strided(j), :] = pa[j] * seg_in + hh[j]
        h_last.append(carry)
        if n % 2 == 0:
            issue_proj(7, n // 2)
        else:
            issue_up_a(n // 2)
    hl_ref[0] = jnp.concatenate(h_last, axis=1)
    cv_ref[0] = jnp.concatenate(tails, axis=1)
    gel = _gelu_tanh(gather(5))
    o_b = jnp.concatenate(
        [h_s[n] * gel[:, n * LRU_BW:(n + 1) * LRU_BW] for n in range(LRU_BLOCKS)], axis=1).astype(BF16)

    for cb in range(ncb):
        issue_up_a(cb)
    m = (_sigmoid(gather(6)) * jnp.concatenate([up_a[cb] for cb in range(ncb)], axis=1)
         + _sigmoid(gather(7)) * _dot(o_b, wbup_ref[...]))
    xo_ref[0] = x + _dot(m.astype(BF16), wout_ref[...])


def _mix_prompt(x, p):
    batch, seq, _ = x.shape
    rows = MIX_ROWS
    assert seq % rows == 0 and rows % CHUNK == 0 and rows >= CONV_W - 1
    row_vec = _resident((1, D_MODEL))
    out_shape = (
        jax.ShapeDtypeStruct((batch, seq, D_MODEL), F32),
        jax.ShapeDtypeStruct((batch, HEADS, DK, DV), F32),
        jax.ShapeDtypeStruct((batch, 1, D_MODEL), F32),
        jax.ShapeDtypeStruct((batch, CONV_W - 1, D_MODEL), F32),
    )
    return pl.pallas_call(
        _mix_prompt_kernel,
        grid=(batch, seq // rows),
        in_specs=[
            pl.BlockSpec((1, rows, D_MODEL), lambda b, t: (b, t, 0)),
            row_vec,
            _resident((D_MODEL, N_SPLITS * D_MODEL)),
            _resident(p["lb_raw"].shape),
            row_vec,
            _resident((CONV_W, D_MODEL)),
            row_vec,
            _resident((LRU_BLOCKS, LRU_BW, 2 * LRU_BW)),
            row_vec, row_vec, row_vec,
            _resident((D_MODEL, D_MODEL)),
            _resident((D_MODEL, D_MODEL)),
            _resident((D_MODEL, D_MODEL)),
        ],
        out_specs=(
            pl.BlockSpec((1, rows, D_MODEL), lambda b, t: (b, t, 0)),
            pl.BlockSpec((1, HEADS, DK, DV), lambda b, t: (b, 0, 0, 0)),
            pl.BlockSpec((1, 1, D_MODEL), lambda b, t: (b, 0, 0)),
            pl.BlockSpec((1, CONV_W - 1, D_MODEL), lambda b, t: (b, 0, 0)),
        ),
        out_shape=out_shape,
        scratch_shapes=[
            pltpu.VMEM((rows, HEADS * QA_W), BF16),
            pltpu.VMEM((rows // CHUNK * HEADS, DK, 2 * DV), F32),
            pltpu.VMEM((rows, D_MODEL), BF16),
            pltpu.VMEM((rows, D_MODEL), F32),
            pltpu.VMEM((LRU_BLOCKS, CONV_PAD + rows, LRU_BW), F32),
            pltpu.VMEM((LRU_BLOCKS, rows, LRU_BW), F32),
        ],
        name="mix_prompt",
        compiler_params=pltpu.CompilerParams(
            dimension_semantics=("arbitrary", "arbitrary"), vmem_limit_bytes=VMEM_LIMIT_BYTES),
    )(x, p["mix_norm"], p["w_in"], p["lb_raw"], p["o_norm"], p["conv_w"], p["conv_b"],
      p["w_ax"], p["b_a"], p["b_x"], p["lam"], p["w_a_up"], p["w_b_up"], p["w_out"])


def _mix_sample_kernel(x_ref, nmix_ref, win_ref, lbraw_ref, onorm_ref, convw_ref, convb_ref,
                       wax_ref, ba_ref, bx_ref, lam_ref, waup_ref, wbup_ref, wout_ref,
                       sin_ref, hlin_ref, cvin_ref,
                       xo_ref, sout_ref, hl_ref, cv_ref,
                       qt_s, ft_s, v_s, oa_s, gsil_s, ga_s, mb_s):
    nseq = x_ref.shape[0]
    blk = SAMPLE_STATE_BLOCK
    j = pl.program_id(0)

    @pl.when(j == 0)
    def _():
        x = x_ref[...]
        hb = _rms(x, nmix_ref[...]).astype(BF16)

        def proj(i):
            return _dot(hb, win_ref[:, i * D_MODEL:(i + 1) * D_MODEL])

        lb = _lower_bound(lbraw_ref[...], 0)
        qt_s[...] = _silu(proj(0)).T
        ft_s[...] = (lb + (1.0 - lb) * _sigmoid(proj(1))).T
        v_s[...] = proj(2)
        gsil_s[...] = onorm_ref[...] * _silu(proj(3))

        xr = proj(4)
        xc = convb_ref[...] + xr * convw_ref[CONV_W - 1:CONV_W, :]
        for i in range(CONV_W - 1):
            xc = xc + cvin_ref[:, i, :] * convw_ref[i:i + 1, :]
        for i in range(CONV_W - 2):
            cv_ref[:, i, :] = cvin_ref[:, i + 1, :]
        cv_ref[:, CONV_W - 2, :] = xr

        xcb = xc.astype(BF16)
        h_parts = []
        for n in range(LRU_BLOCKS):
            ns = slice(n * LRU_BW, (n + 1) * LRU_BW)
            pre = _dot(xcb[:, ns], wax_ref[n])
            a, bterm = _lru_coeffs(pre[:, :LRU_BW] + ba_ref[:, ns], pre[:, LRU_BW:] + bx_ref[:, ns],
                                   xc[:, ns], lam_ref[:, ns])
            h_parts.append(a * hlin_ref[:, ns] + bterm)
        h_new = jnp.concatenate(h_parts, axis=1)
        hl_ref[...] = h_new
        o_b = (h_new * _gelu_tanh(proj(5))).astype(BF16)
        ga_s[...] = _sigmoid(proj(6))
        mb_s[...] = _sigmoid(proj(7)) * _dot(o_b, wbup_ref[...])

    b0 = pl.multiple_of(j * blk, blk)
    shifts = [lax.rem(nseq - b0 + p * blk, nseq) for p in range(3)]

    lane = lax.broadcasted_iota(jnp.int32, (DK, nseq), 1)
    sel_r = lax.broadcasted_iota(jnp.int32, (nseq, blk * DV), 0)
    sel_c = lax.broadcasted_iota(jnp.int32, (nseq, blk * DV), 1)
    selector = jnp.where((sel_r < 3 * blk) & (jnp.bitwise_and(sel_r, blk - 1) == sel_c // DV),
                         1.0, 0.0).astype(BF16)

    def spread(cols_t):
        p1 = cols_t.astype(BF16).astype(F32)
        r1 = cols_t - p1
        p2 = r1.astype(BF16).astype(F32)
        p3 = r1 - p2
        lhs = jnp.where(lane < blk, pltpu.roll(p1, shifts[0], 1),
                        jnp.where(lane < 2 * blk, pltpu.roll(p2, shifts[1], 1),
                                  jnp.where(lane < 3 * blk, pltpu.roll(p3, shifts[2], 1), 0.0)))
        return _dot(lhs.astype(BF16), selector)

    for h in range(HEADS):
        hrows = slice(h * DK, (h + 1) * DK)
        q_all = spread(qt_s[hrows, :])
        f_all = spread(ft_s[hrows, :])
        v_rows = v_s[pl.ds(b0, blk), hrows]
        o_rows = []
        for i in range(blk):
            f_col = f_all[:, i * DV:(i + 1) * DV]
            s_new = f_col * sin_ref[i, h] + (1.0 - f_col) * v_rows[i:i + 1, :]
            sout_ref[i, h] = s_new
            o_rows.append(jnp.sum(q_all[:, i * DV:(i + 1) * DV] * s_new, axis=0, keepdims=True))
        o = jnp.concatenate(o_rows, axis=0)
        oa_s[pl.ds(b0, blk), hrows] = o * lax.rsqrt(
            jnp.mean(o * o, axis=-1, keepdims=True) + EPS)

    @pl.when(j == pl.num_programs(0) - 1)
    def _():
        o_a = (oa_s[...] * gsil_s[...]).astype(BF16)
        m = ga_s[...] * _dot(o_a, waup_ref[...]) + mb_s[...]
        xo_ref[...] = x_ref[...] + _dot(m.astype(BF16), wout_ref[...])


def _mix_sample(x2d, s_in, h_in, c_in, p):
    nseq = x2d.shape[0]
    blk = SAMPLE_STATE_BLOCK
    assert nseq == LANES and nseq % blk == 0
    row_vec = _resident((1, D_MODEL))
    tok = _resident((nseq, D_MODEL))
    conv_state = _resident((nseq, CONV_W - 1, D_MODEL))
    out_shape = (
        jax.ShapeDtypeStruct((nseq, D_MODEL), F32),
        jax.ShapeDtypeStruct((nseq, HEADS, DK, DV), F32),
        jax.ShapeDtypeStruct((nseq, D_MODEL), F32),
        jax.ShapeDtypeStruct((nseq, CONV_W - 1, D_MODEL), F32),
    )
    state_spec = pl.BlockSpec((blk, HEADS, DK, DV), lambda j: (j, 0, 0, 0))
    return pl.pallas_call(
        _mix_sample_kernel,
        grid=(nseq // blk,),
        in_specs=[
            tok, row_vec,
            _resident((D_MODEL, N_SPLITS * D_MODEL)),
            _resident(p["lb_raw"].shape),
            row_vec,
            _resident((CONV_W, D_MODEL)),
            row_vec,
            _resident((LRU_BLOCKS, LRU_BW, 2 * LRU_BW)),
            row_vec, row_vec, row_vec,
            _resident((D_MODEL, D_MODEL)),
            _resident((D_MODEL, D_MODEL)),
            _resident((D_MODEL, D_MODEL)),
            state_spec, tok, conv_state,
        ],
        out_specs=(
            pl.BlockSpec((nseq, D_MODEL), lambda j: (0, 0)),
            state_spec,
            pl.BlockSpec((nseq, D_MODEL), lambda j: (0, 0)),
            pl.BlockSpec((nseq, CONV_W - 1, D_MODEL), lambda j: (0, 0, 0)),
        ),
        out_shape=out_shape,
        scratch_shapes=[
            pltpu.VMEM((D_MODEL, nseq), F32),
            pltpu.VMEM((D_MODEL, nseq), F32),
            pltpu.VMEM((nseq, D_MODEL), F32),
            pltpu.VMEM((nseq, D_MODEL), F32),
            pltpu.VMEM((nseq, D_MODEL), F32),
            pltpu.VMEM((nseq, D_MODEL), F32),
            pltpu.VMEM((nseq, D_MODEL), F32),
        ],
        name="mix_sample",
        compiler_params=pltpu.CompilerParams(
            dimension_semantics=("arbitrary",), vmem_limit_bytes=VMEM_LIMIT_BYTES),
    )(x2d, p["mix_norm"], p["w_in"], p["lb_raw"], p["o_norm"], p["conv_w"], p["conv_b"],
      p["w_ax"], p["b_a"], p["b_x"], p["lam"], p["w_a_up"], p["w_b_up"], p["w_out"],
      s_in, h_in, c_in)


def kernel(x_prompt, x_sample, state_hgrn, state_lru, state_conv, ffn1_norm, ffn1_w_gate, ffn1_w_up, ffn1_w_down, mix_norm, w_in, hgrn_lower_bounds, hgrn_out_norm, conv_w, conv_b, lru_w_a, lru_b_a, lru_w_x, lru_b_x, lru_lambda, w_a_up, w_b_up, w_out, ffn2_norm, ffn2_w_gate, ffn2_w_up, ffn2_w_down, final_norm):
    depth = w_in.shape[0]
    assert depth == 1, "single-layer trunk"
    l = 0
    row = lambda v: v.reshape(1, -1).astype(F32)
    bf = lambda w: w.astype(BF16)
    p = {
        "mix_norm": row(mix_norm[l]),
        "w_in": bf(w_in[l]),
        "lb_raw": hgrn_lower_bounds.astype(F32),
        "o_norm": row(hgrn_out_norm[l]),
        "conv_w": conv_w[l].astype(F32),
        "conv_b": row(conv_b[l]),
        "w_ax": bf(jnp.concatenate([lru_w_a[l], lru_w_x[l]], axis=-1)),
        "b_a": row(lru_b_a[l]),
        "b_x": row(lru_b_x[l]),
        "lam": row(lru_lambda[l]),
        "w_a_up": bf(w_a_up[l]),
        "w_b_up": bf(w_b_up[l]),
        "w_out": bf(w_out[l]),
    }
    f1 = (row(ffn1_norm[l]), bf(ffn1_w_gate[l]), bf(ffn1_w_up[l]), bf(ffn1_w_down[l]))
    f2 = (row(ffn2_norm[l]), bf(ffn2_w_gate[l]), bf(ffn2_w_up[l]), bf(ffn2_w_down[l]))
    gfin = row(final_norm)

    bp, tp, _ = x_prompt.shape
    bs, ts, _ = x_sample.shape
    assert ts == 1
    xp, xs = _ffn(x_prompt.reshape(bp * tp, D_MODEL), x_sample.reshape(bs, D_MODEL),
                  *f1, gfin, final_norm=False)
    xp, s_p, h_p, c_p = _mix_prompt(xp.reshape(bp, tp, D_MODEL), p)
    xs, s_s, h_s, c_s = _mix_sample(xs, state_hgrn[l], state_lru[l], state_conv[l], p)
    y_p, y_s = _ffn(xp.reshape(bp * tp, D_MODEL), xs, *f2, gfin, final_norm=True)
    y_p = y_p.reshape(bp, tp, D_MODEL)
    y_s = y_s.reshape(bs, ts, D_MODEL)

    return (y_p, y_s, s_p[None], h_p.reshape(bp, D_MODEL)[None], c_p[None],
            s_s[None], h_s[None], c_s[None])
```

```python
import functools

import jax
import jax.numpy as jnp
from jax import lax
from jax.experimental import pallas as pl
from jax.experimental.pallas import tpu as pltpu

D_MODEL = 1024
HEADS = 8
DK = 128
DV = 128
CHUNK = 64
LRU_BLOCKS = 8
LRU_BW = 128
LRU_C = 8.0
CONV_W = 4
D_FF = 2816
EPS = 1e-6
LOG2_E = 1.4426950408889634
N_SPLITS = 8

SUBLANES = 8
LANES = 128
MXU_COLS = 256
VMEM_LIMIT_BYTES = 56 * 1024 * 1024

FFN_ROWS = 512
MIX_ROWS = 256
SAMPLE_STATE_BLOCK = 8

BF16 = jnp.bfloat16
F32 = jnp.float32


def _dot(a, b):
    return jnp.dot(a, b, preferred_element_type=F32)


def _dot_tn(a, b):
    return lax.dot_general(a, b, (((0,), (0,)), ((), ())), preferred_element_type=F32)


def _dot_nt(a, b):
    return lax.dot_general(a, b, (((1,), (1,)), ((), ())), preferred_element_type=F32)


def _sigmoid(x):
    return 1.0 / (1.0 + jnp.exp2(x * (-LOG2_E)))


def _silu(x):
    return x * _sigmoid(x)


def _gelu_tanh(x):
    c = 0.7978845608028654
    return 0.5 * x * (1.0 + jnp.tanh(c * (x + 0.044715 * (x * x * x))))


def _rms(x, g):
    return x * lax.rsqrt(jnp.mean(x * x, axis=-1, keepdims=True) + EPS) * g


def _lower_bound(raw, layer):
    m = jnp.max(raw, axis=0, keepdims=True)
    e = jnp.exp(raw - m)
    den = jnp.sum(e, axis=0, keepdims=True)
    num = jnp.sum(e[0:layer + 1], axis=0, keepdims=True)
    return num / den


def _softplus(x):
    return jnp.maximum(x, 0.0) + jnp.log1p(jnp.exp(-jnp.abs(x)))


def _lru_coeffs(r_pre, i_pre, xc, lam):
    r = _sigmoid(r_pre)
    ig = _sigmoid(i_pre)
    log_a = (-LRU_C) * r * _softplus(-lam)
    a = jnp.exp(log_a)
    th = jnp.tanh(log_a)
    mult = jnp.sqrt(-2.0 * th / (1.0 - th))
    return a, mult * (ig * xc)


def _split3(x):
    p1 = x.astype(BF16)
    r1 = x - p1.astype(F32)
    p2 = r1.astype(BF16)
    r2 = r1 - p2.astype(F32)
    return p1, p2, r2.astype(BF16)


def _ffn_kernel(xa_ref, xb_ref, g_ref, wg_ref, wu_ref, wd_ref, gf_ref, oa_ref, ob_ref, *,
                final_norm, a_tiles):
    def half_step(x_ref, o_ref):
        x = x_ref[...]
        inv_rms = lax.rsqrt(jnp.mean(x * x, axis=-1, keepdims=True) + EPS)
        h = (x * g_ref[...]).astype(BF16)
        gate = _dot(h, wg_ref[...]) * inv_rms
        up = _dot(h, wu_ref[...]) * inv_rms
        act = (_silu(gate) * up).astype(BF16)
        y = x + 0.5 * _dot(act, wd_ref[...])
        if final_norm:
            y = _rms(y, gf_ref[...])
        o_ref[...] = y

    i = pl.program_id(0)

    @pl.when(i < a_tiles)
    def _():
        half_step(xa_ref, oa_ref)

    @pl.when(i == a_tiles)
    def _():
        half_step(xb_ref, ob_ref)


def _resident(shape):
    nd = len(shape)
    return pl.BlockSpec(shape, lambda *_: (0,) * nd, pipeline_mode=pl.Buffered(1))


def _ffn(xa, xb, g, wg, wu, wd, gf, *, final_norm):
    na, nb = xa.shape[0], xb.shape[0]
    assert na % FFN_ROWS == 0 and nb <= FFN_ROWS and nb % SUBLANES == 0
    a_tiles = na // FFN_ROWS
    a_tile = lambda i: (jnp.minimum(i, a_tiles - 1), 0)
    b_whole = lambda i: (0, 0)
    return pl.pallas_call(
        functools.partial(_ffn_kernel, final_norm=final_norm, a_tiles=a_tiles),
        grid=(a_tiles + 1,),
        in_specs=[
            pl.BlockSpec((FFN_ROWS, D_MODEL), a_tile),
            pl.BlockSpec((nb, D_MODEL), b_whole),
            _resident((1, D_MODEL)),
            _resident((D_MODEL, D_FF)),
            _resident((D_MODEL, D_FF)),
            _resident((D_FF, D_MODEL)),
            _resident((1, D_MODEL)),
        ],
        out_specs=(pl.BlockSpec((FFN_ROWS, D_MODEL), a_tile),
                   pl.BlockSpec((nb, D_MODEL), b_whole)),
        out_shape=(jax.ShapeDtypeStruct((na, D_MODEL), F32),
                   jax.ShapeDtypeStruct((nb, D_MODEL), F32)),
        name="ffn_final" if final_norm else "ffn",
        compiler_params=pltpu.CompilerParams(
            dimension_semantics=("arbitrary",), vmem_limit_bytes=VMEM_LIMIT_BYTES,
            allow_input_fusion=[False, False, False, True, True, True, False]),
    )(xa, xb, g, wg, wu, wd, gf)


SCAN_SEG = 4
SCAN_GROUP = SCAN_SEG * SUBLANES
CONV_PAD = SUBLANES
PIECE_ROWS = 2 * SUBLANES
QA_W = 2 * LANES


def _mix_prompt_kernel(x_ref, nmix_ref, win_ref, lbraw_ref, onorm_ref, convw_ref, convb_ref,
                       wax_ref, ba_ref, bx_ref, lam_ref, waup_ref, wbup_ref, wout_ref,
                       xo_ref, s_ref, hl_ref, cv_ref,
                       qa_s, ud_s, vb_s, oa_s, xr_s, h_s):
    rows = MIX_ROWS
    t = pl.program_id(1)

    @pl.when(t == 0)
    def _():
        s_ref[...] = jnp.zeros_like(s_ref)
        hl_ref[...] = jnp.zeros_like(hl_ref)
        xr_s[:, 0:CONV_PAD, :] = jnp.zeros((LRU_BLOCKS, CONV_PAD, LRU_BW), F32)

    x = x_ref[0]
    inv_rms = lax.rsqrt(jnp.mean(x * x, axis=-1, keepdims=True) + EPS)
    xg = x * nmix_ref[...]
    q_pre = _dot(xg.astype(BF16), win_ref[:, 0:D_MODEL]) * inv_rms
    hb = (xg * inv_rms).astype(BF16)

    def proj(i):
        return _dot(hb, win_ref[:, i * D_MODEL:(i + 1) * D_MODEL])

    early = {}

    ncb = D_MODEL // MXU_COLS

    def issue_proj(i, cb):
        if cb < ncb and (i, cb) not in early:
            c0 = i * D_MODEL + cb * MXU_COLS
            early[(i, cb)] = _dot(hb, win_ref[:, c0:c0 + MXU_COLS])

    def gather(i):
        for cb in range(ncb):
            issue_proj(i, cb)
        return jnp.concatenate([early[(i, cb)] for cb in range(ncb)], axis=1)

    lb = _lower_bound(lbraw_ref[...], 0)
    q_act = _silu(q_pre)
    f = lb + (1.0 - lb) * _sigmoid(proj(1))
    kk = 1.0 - f
    vb_s[...] = proj(2).astype(BF16)

    ri = lax.broadcasted_iota(jnp.int32, (rows, rows), 0)
    ci = lax.broadcasted_iota(jnp.int32, (rows, rows), 1)
    same_chunk = (ri // CHUNK) == (ci // CHUNK)
    tril_blk = jnp.where((ri >= ci) & same_chunk, 1.0, 0.0).astype(BF16)
    logf = jnp.log(f)
    g1 = logf.astype(BF16)
    g2 = (logf - g1.astype(F32)).astype(BF16)
    b_all = _dot(tril_blk, g1) + _dot(tril_blk, g2)

    causal = (lax.broadcasted_iota(jnp.int32, (CHUNK, CHUNK), 0)
              >= lax.broadcasted_iota(jnp.int32, (CHUNK, CHUNK), 1))
    piece_row = lax.broadcasted_iota(jnp.int32, (PIECE_ROWS, D_MODEL), 0)
    ones_pv = jnp.ones((3 * PIECE_ROWS, DV), BF16)
    zeros_pv = jnp.zeros((3 * PIECE_ROWS, DV), BF16)
    zeros_cv = jnp.zeros((CHUNK, DV), BF16)

    nchunk = rows // CHUNK
    last = piece_row == PIECE_ROWS - 1
    zero_p = jnp.zeros((PIECE_ROWS, D_MODEL), BF16)

    for c in range(nchunk):
        issue_proj(3, c)
        rs = slice(c * CHUNK, (c + 1) * CHUNK)
        b = b_all[rs, :]
        bl = b[CHUNK - 1:CHUNK, :]
        qe = (q_act[rs, :] * jnp.exp(b)).astype(BF16)
        ke = (kk[rs, :] * jnp.exp(-b)).astype(BF16)
        kd = (kk[rs, :] * jnp.exp(bl - b)).astype(BF16)
        pieces = jnp.concatenate(
            [jnp.where(last, e, zero_p) for e in _split3(jnp.exp(b[CHUNK - PIECE_ROWS:, :]))], axis=0)
        for h in range(HEADS):
            hs = slice(h * DK, (h + 1) * DK)
            qa_s[rs, h * QA_W:h * QA_W + DK] = qe[:, hs]
            att = jnp.where(causal, _dot_nt(qe[:, hs], ke[:, hs]), 0.0)
            qa_s[rs, h * QA_W + DK:h * QA_W + DK + CHUNK] = att.astype(BF16)
            lhs = jnp.concatenate([kd[:, hs], pieces[:, hs]], axis=0)
            rhs = jnp.concatenate([jnp.concatenate([vb_s[rs, hs], zeros_cv], axis=1),
                                   jnp.concatenate([zeros_pv, ones_pv], axis=1)], axis=0)
            ud_s[c * HEADS + h] = _dot_tn(lhs, rhs)

    for c in range(nchunk):
        rs = slice(c * CHUNK, (c + 1) * CHUNK)
        for h in range(HEADS):
            hs = slice(h * DK, (h + 1) * DK)
            s_old = s_ref[0, h]
            rhs = jnp.concatenate([s_old.astype(BF16), vb_s[rs, hs]], axis=0)
            o = _dot(qa_s[rs, h * QA_W:h * QA_W + DK + CHUNK], rhs)
            oa_s[rs, hs] = o * lax.rsqrt(jnp.mean(o * o, axis=-1, keepdims=True) + EPS)
            ud = ud_s[c * HEADS + h]
            s_ref[0, h] = ud[:, DV:] * s_old + ud[:, 0:DV]

    o_a = (oa_s[...] * onorm_ref[...] * _silu(gather(3))).astype(BF16)
    up_a = {}

    def issue_up_a(cb):
        if cb < ncb and cb not in up_a:
            up_a[cb] = _dot(o_a, waup_ref[:, cb * MXU_COLS:(cb + 1) * MXU_COLS])

    xr = proj(4)
    h_prev = hl_ref[0]
    sub = lax.broadcasted_iota(jnp.int32, (SUBLANES, LANES), 0)
    ngroup = rows // SCAN_GROUP
    tails = []
    h_last = []
    for n in range(LRU_BLOCKS):
        ns = slice(n * LRU_BW, (n + 1) * LRU_BW)
        xr_s[n, CONV_PAD:CONV_PAD + rows, :] = xr[:, ns]
        taps = [jnp.broadcast_to(convw_ref[j:j + 1, ns], (SUBLANES, LRU_BW)) for j in range(CONV_W)]
        bias = jnp.broadcast_to(convb_ref[:, ns], (SUBLANES, LRU_BW))
        xc_parts = []
        for grp in range(ngroup):
            xin = {d: xr_s[n, pl.ds(CONV_PAD + grp * SCAN_GROUP + d, SUBLANES, stride=SCAN_SEG), :]
                   for d in range(1 - CONV_W, SCAN_SEG)}
            for j in range(SCAN_SEG):
                acc = bias
                for tap in range(CONV_W):
                    acc = acc + xin[j + tap - (CONV_W - 1)] * taps[tap]
                xc_parts.append(acc)
        xc = jnp.concatenate(xc_parts, axis=0)
        tail = xr_s[n, CONV_PAD + rows - (CONV_W - 1):CONV_PAD + rows, :]
        tails.append(tail)
        xr_s[n, CONV_PAD - (CONV_W - 1):CONV_PAD, :] = tail
        pre = _dot(xc.astype(BF16), wax_ref[n])
        issue_proj(5 + n % 2, n // 2)
        a, bterm = _lru_coeffs(pre[:, :LRU_BW] + ba_ref[:, ns], pre[:, LRU_BW:] + bx_ref[:, ns],
                               xc, lam_ref[:, ns])
        carry = h_prev[:, ns]
        for grp in range(ngroup):
            def vreg(v, j, grp=grp):
                r0 = grp * SCAN_GROUP + j * SUBLANES
                return v[r0:r0 + SUBLANES, :]

            def strided(j, grp=grp):
                return pl.ds(grp * SCAN_GROUP + j, SUBLANES, stride=SCAN_SEG)
            pa = [vreg(a, 0)]
            hh = [vreg(bterm, 0)]
            for j in range(1, SCAN_SEG):
                aj = vreg(a, j)
                hh.append(aj * hh[-1] + vreg(bterm, j))
                pa.append(aj * pa[-1])
            pi, hi = pa[-1], hh[-1]
            for d in (1, 2, 4):
                keep = sub >= d
                hi = pi * jnp.where(keep, pltpu.roll(hi, d, 0), 0.0) + hi
                pi = pi * jnp.where(keep, pltpu.roll(pi, d, 0), 1.0)
            first = sub >= 1
            seg_in = (jnp.where(first, pltpu.roll(pi, 1, 0), 1.0) * carry
                      + jnp.where(first, pltpu.roll(hi, 1, 0), 0.0))
            carry = pi[SUBLANES - 1:, :] * carry + hi[SUBLANES - 1:, :]
            for j in range(SCAN_SEG):
                h_s[n, strided(j), :] = pa[j] * seg_in + hh[j]
        h_last.append(carry)
        if n % 2 == 0:
            issue_proj(7, n // 2)
        else:
            issue_up_a(n // 2)
    hl_ref[0] = jnp.concatenate(h_last, axis=1)
    cv_ref[0] = jnp.concatenate(tails, axis=1)
    gel = _gelu_tanh(gather(5))
    o_b = jnp.concatenate(
        [h_s[n] * gel[:, n * LRU_BW:(n + 1) * LRU_BW] for n in range(LRU_BLOCKS)], axis=1).astype(BF16)

    for cb in range(ncb):
        issue_up_a(cb)
    m = (_sigmoid(gather(6)) * jnp.concatenate([up_a[cb] for cb in range(ncb)], axis=1)
         + _sigmoid(gather(7)) * _dot(o_b, wbup_ref[...]))
    xo_ref[0] = x + _dot(m.astype(BF16), wout_ref[...])


def _mix_prompt(x, p):
    batch, seq, _ = x.shape
    rows = MIX_ROWS
    assert seq % rows == 0 and rows % CHUNK == 0 and rows >= CONV_W - 1
    row_vec = _resident((1, D_MODEL))
    out_shape = (
        jax.ShapeDtypeStruct((batch, seq, D_MODEL), F32),
        jax.ShapeDtypeStruct((batch, HEADS, DK, DV), F32),
        jax.ShapeDtypeStruct((batch, 1, D_MODEL), F32),
        jax.ShapeDtypeStruct((batch, CONV_W - 1, D_MODEL), F32),
    )
    return pl.pallas_call(
        _mix_prompt_kernel,
        grid=(batch, seq // rows),
        in_specs=[
            pl.BlockSpec((1, rows, D_MODEL), lambda b, t: (b, t, 0)),
            row_vec,
            _resident((D_MODEL, N_SPLITS * D_MODEL)),
            _resident(p["lb_raw"].shape),
            row_vec,
            _resident((CONV_W, D_MODEL)),
            row_vec,
            _resident((LRU_BLOCKS, LRU_BW, 2 * LRU_BW)),
            row_vec, row_vec, row_vec,
            _resident((D_MODEL, D_MODEL)),
            _resident((D_MODEL, D_MODEL)),
            _resident((D_MODEL, D_MODEL)),
        ],
        out_specs=(
            pl.BlockSpec((1, rows, D_MODEL), lambda b, t: (b, t, 0)),
            pl.BlockSpec((1, HEADS, DK, DV), lambda b, t: (b, 0, 0, 0)),
            pl.BlockSpec((1, 1, D_MODEL), lambda b, t: (b, 0, 0)),
            pl.BlockSpec((1, CONV_W - 1, D_MODEL), lambda b, t: (b, 0, 0)),
        ),
        out_shape=out_shape,
        scratch_shapes=[
            pltpu.VMEM((rows, HEADS * QA_W), BF16),
            pltpu.VMEM((rows // CHUNK * HEADS, DK, 2 * DV), F32),
            pltpu.VMEM((rows, D_MODEL), BF16),
            pltpu.VMEM((rows, D_MODEL), F32),
            pltpu.VMEM((LRU_BLOCKS, CONV_PAD + rows, LRU_BW), F32),
            pltpu.VMEM((LRU_BLOCKS, rows, LRU_BW), F32),
        ],
        name="mix_prompt",
        compiler_params=pltpu.CompilerParams(
            dimension_semantics=("arbitrary", "arbitrary"), vmem_limit_bytes=VMEM_LIMIT_BYTES),
    )(x, p["mix_norm"], p["w_in"], p["lb_raw"], p["o_norm"], p["conv_w"], p["conv_b"],
      p["w_ax"], p["b_a"], p["b_x"], p["lam"], p["w_a_up"], p["w_b_up"], p["w_out"])


def _mix_sample_kernel(x_ref, nmix_ref, win_ref, lbraw_ref, onorm_ref, convw_ref, convb_ref,
                       wax_ref, ba_ref, bx_ref, lam_ref, waup_ref, wbup_ref, wout_ref,
                       sin_ref, hlin_ref, cvin_ref,
                       xo_ref, sout_ref, hl_ref, cv_ref,
                       qt_s, ft_s, v_s, oa_s, gsil_s, ga_s, mb_s):
    nseq = x_ref.shape[0]
    blk = SAMPLE_STATE_BLOCK
    j = pl.program_id(0)

    @pl.when(j == 0)
    def _():
        x = x_ref[...]
        hb = _rms(x, nmix_ref[...]).astype(BF16)

        def proj(i):
            return _dot(hb, win_ref[:, i * D_MODEL:(i + 1) * D_MODEL])

        lb = _lower_bound(lbraw_ref[...], 0)
        qt_s[...] = _silu(proj(0)).T
        ft_s[...] = (lb + (1.0 - lb) * _sigmoid(proj(1))).T
        v_s[...] = proj(2)
        gsil_s[...] = onorm_ref[...] * _silu(proj(3))

        xr = proj(4)
        xc = convb_ref[...] + xr * convw_ref[CONV_W - 1:CONV_W, :]
        for i in range(CONV_W - 1):
            xc = xc + cvin_ref[:, i, :] * convw_ref[i:i + 1, :]
        for i in range(CONV_W - 2):
            cv_ref[:, i, :] = cvin_ref[:, i + 1, :]
        cv_ref[:, CONV_W - 2, :] = xr

        xcb = xc.astype(BF16)
        h_parts = []
        for n in range(LRU_BLOCKS):
            ns = slice(n * LRU_BW, (n + 1) * LRU_BW)
            pre = _dot(xcb[:, ns], wax_ref[n])
            a, bterm = _lru_coeffs(pre[:, :LRU_BW] + ba_ref[:, ns], pre[:, LRU_BW:] + bx_ref[:, ns],
                                   xc[:, ns], lam_ref[:, ns])
            h_parts.append(a * hlin_ref[:, ns] + bterm)
        h_new = jnp.concatenate(h_parts, axis=1)
        hl_ref[...] = h_new
        o_b = (h_new * _gelu_tanh(proj(5))).astype(BF16)
        ga_s[...] = _sigmoid(proj(6))
        mb_s[...] = _sigmoid(proj(7)) * _dot(o_b, wbup_ref[...])

    b0 = pl.multiple_of(j * blk, blk)
    shifts = [lax.rem(nseq - b0 + p * blk, nseq) for p in range(3)]

    lane = lax.broadcasted_iota(jnp.int32, (DK, nseq), 1)
    sel_r = lax.broadcasted_iota(jnp.int32, (nseq, blk * DV), 0)
    sel_c = lax.broadcasted_iota(jnp.int32, (nseq, blk * DV), 1)
    selector = jnp.where((sel_r < 3 * blk) & (jnp.bitwise_and(sel_r, blk - 1) == sel_c // DV),
                         1.0, 0.0).astype(BF16)

    def spread(cols_t):
        p1 = cols_t.astype(BF16).astype(F32)
        r1 = cols_t - p1
        p2 = r1.astype(BF16).astype(F32)
        p3 = r1 - p2
        lhs = jnp.where(lane < blk, pltpu.roll(p1, shifts[0], 1),
                        jnp.where(lane < 2 * blk, pltpu.roll(p2, shifts[1], 1),
                                  jnp.where(lane < 3 * blk, pltpu.roll(p3, shifts[2], 1), 0.0)))
        return _dot(lhs.astype(BF16), selector)

    for h in range(HEADS):
        hrows = slice(h * DK, (h + 1) * DK)
        q_all = spread(qt_s[hrows, :])
        f_all = spread(ft_s[hrows, :])
        v_rows = v_s[pl.ds(b0, blk), hrows]
        o_rows = []
        for i in range(blk):
            f_col = f_all[:, i * DV:(i + 1) * DV]
            s_new = f_col * sin_ref[i, h] + (1.0 - f_col) * v_rows[i:i + 1, :]
            sout_ref[i, h] = s_new
            o_rows.append(jnp.sum(q_all[:, i * DV:(i + 1) * DV] * s_new, axis=0, keepdims=True))
        o = jnp.concatenate(o_rows, axis=0)
        oa_s[pl.ds(b0, blk), hrows] = o * lax.rsqrt(
            jnp.mean(o * o, axis=-1, keepdims=True) + EPS)

    @pl.when(j == pl.num_programs(0) - 1)
    def _():
        o_a = (oa_s[...] * gsil_s[...]).astype(BF16)
        m = ga_s[...] * _dot(o_a, waup_ref[...]) + mb_s[...]
        xo_ref[...] = x_ref[...] + _dot(m.astype(BF16), wout_ref[...])


def _mix_sample(x2d, s_in, h_in, c_in, p):
    nseq = x2d.shape[0]
    blk = SAMPLE_STATE_BLOCK
    assert nseq == LANES and nseq % blk == 0
    row_vec = _resident((1, D_MODEL))
    tok = _resident((nseq, D_MODEL))
    conv_state = _resident((nseq, CONV_W - 1, D_MODEL))
    out_shape = (
        jax.ShapeDtypeStruct((nseq, D_MODEL), F32),
        jax.ShapeDtypeStruct((nseq, HEADS, DK, DV), F32),
        jax.ShapeDtypeStruct((nseq, D_MODEL), F32),
        jax.ShapeDtypeStruct((nseq, CONV_W - 1, D_MODEL), F32),
    )
    state_spec = pl.BlockSpec((blk, HEADS, DK, DV), lambda j: (j, 0, 0, 0))
    return pl.pallas_call(
        _mix_sample_kernel,
        grid=(nseq // blk,),
        in_specs=[
            tok, row_vec,
            _resident((D_MODEL, N_SPLITS * D_MODEL)),
            _resident(p["lb_raw"].shape),
            row_vec,
            _resident((CONV_W, D_MODEL)),
            row_vec,
            _resident((LRU_BLOCKS, LRU_BW, 2 * LRU_BW)),
            row_vec, row_vec, row_vec,
            _resident((D_MODEL, D_MODEL)),
            _resident((D_MODEL, D_MODEL)),
            _resident((D_MODEL, D_MODEL)),
            state_spec, tok, conv_state,
        ],
        out_specs=(
            pl.BlockSpec((nseq, D_MODEL), lambda j: (0, 0)),
            state_spec,
            pl.BlockSpec((nseq, D_MODEL), lambda j: (0, 0)),
            pl.BlockSpec((nseq, CONV_W - 1, D_MODEL), lambda j: (0, 0, 0)),
        ),
        out_shape=out_shape,
        scratch_shapes=[
            pltpu.VMEM((D_MODEL, nseq), F32),
            pltpu.VMEM((D_MODEL, nseq), F32),
            pltpu.VMEM((nseq, D_MODEL), F32),
            pltpu.VMEM((nseq, D_MODEL), F32),
            pltpu.VMEM((nseq, D_MODEL), F32),
            pltpu.VMEM((nseq, D_MODEL), F32),
            pltpu.VMEM((nseq, D_MODEL), F32),
        ],
        name="mix_sample",
        compiler_params=pltpu.CompilerParams(
            dimension_semantics=("arbitrary",), vmem_limit_bytes=VMEM_LIMIT_BYTES),
    )(x2d, p["mix_norm"], p["w_in"], p["lb_raw"], p["o_norm"], p["conv_w"], p["conv_b"],
      p["w_ax"], p["b_a"], p["b_x"], p["lam"], p["w_a_up"], p["w_b_up"], p["w_out"],
      s_in, h_in, c_in)


def kernel(x_prompt, x_sample, state_hgrn, state_lru, state_conv, ffn1_norm, ffn1_w_gate, ffn1_w_up, ffn1_w_down, mix_norm, w_in, hgrn_lower_bounds, hgrn_out_norm, conv_w, conv_b, lru_w_a, lru_b_a, lru_w_x, lru_b_x, lru_lambda, w_a_up, w_b_up, w_out, ffn2_norm, ffn2_w_gate, ffn2_w_up, ffn2_w_down, final_norm):
    depth = w_in.shape[0]
    assert depth == 1, "single-layer trunk"
    l = 0
    row = lambda v: v.reshape(1, -1).astype(F32)
    bf = lambda w: w.astype(BF16)
    p = {
        "mix_norm": row(mix_norm[l]),
        "w_in": bf(w_in[l]),
        "lb_raw": hgrn_lower_bounds.astype(F32),
        "o_norm": row(hgrn_out_norm[l]),
        "conv_w": conv_w[l].astype(F32),
        "conv_b": row(conv_b[l]),
        "w_ax": bf(jnp.concatenate([lru_w_a[l], lru_w_x[l]], axis=-1)),
        "b_a": row(lru_b_a[l]),
        "b_x": row(lru_b_x[l]),
        "lam": row(lru_lambda[l]),
        "w_a_up": bf(w_a_up[l]),
        "w_b_up": bf(w_b_up[l]),
        "w_out": bf(w_out[l]),
    }
    f1 = (row(ffn1_norm[l]), bf(ffn1_w_gate[l]), bf(ffn1_w_up[l]), bf(ffn1_w_down[l]))
    f2 = (row(ffn2_norm[l]), bf(ffn2_w_gate[l]), bf(ffn2_w_up[l]), bf(ffn2_w_down[l]))
    gfin = row(final_norm)

    bp, tp, _ = x_prompt.shape
    bs, ts, _ = x_sample.shape
    assert ts == 1
    xp, xs = _ffn(x_prompt.reshape(bp * tp, D_MODEL), x_sample.reshape(bs, D_MODEL),
                  *f1, gfin, final_norm=False)
    xp, s_p, h_p, c_p = _mix_prompt(xp.reshape(bp, tp, D_MODEL), p)
    xs, s_s, h_s, c_s = _mix_sample(xs, state_hgrn[l], state_lru[l], state_conv[l], p)
    y_p, y_s = _ffn(xp.reshape(bp * tp, D_MODEL), xs, *f2, gfin, final_norm=True)
    y_p = y_p.reshape(bp, tp, D_MODEL)
    y_s = y_s.reshape(bs, ts, D_MODEL)

    return (y_p, y_s, s_p[None], h_p.reshape(bp, D_MODEL)[None], c_p[None],
            s_s[None], h_s[None], c_s[None])
```

```python
import functools

import jax
import jax.numpy as jnp
from jax import lax
from jax.experimental import pallas as pl
from jax.experimental.pallas import tpu as pltpu

D_MODEL = 1024
HEADS = 8
DK = 128
DV = 128
CHUNK = 64
LRU_BLOCKS = 8
LRU_BW = 128
LRU_C = 8.0
CONV_W = 4
D_FF = 2816
EPS = 1e-6
LOG2_E = 1.4426950408889634
N_SPLITS = 8

SUBLANES = 8
LANES = 128
MXU_COLS = 256
VMEM_LIMIT_BYTES = 56 * 1024 * 1024

FFN_ROWS = 512
MIX_ROWS = 256
SAMPLE_STATE_BLOCK = 8

BF16 = jnp.bfloat16
F32 = jnp.float32


def _dot(a, b):
    return jnp.dot(a, b, preferred_element_type=F32)


def _dot_tn(a, b):
    return lax.dot_general(a, b, (((0,), (0,)), ((), ())), preferred_element_type=F32)


def _dot_nt(a, b):
    return lax.dot_general(a, b, (((1,), (1,)), ((), ())), preferred_element_type=F32)


def _sigmoid(x):
    return 1.0 / (1.0 + jnp.exp2(x * (-LOG2_E)))


def _silu(x):
    return x * _sigmoid(x)


def _gelu_tanh(x):
    c = 0.7978845608028654
    return 0.5 * x * (1.0 + jnp.tanh(c * (x + 0.044715 * (x * x * x))))


def _rms(x, g):
    return x * lax.rsqrt(jnp.mean(x * x, axis=-1, keepdims=True) + EPS) * g


def _lower_bound(raw, layer):
    m = jnp.max(raw, axis=0, keepdims=True)
    e = jnp.exp(raw - m)
    den = jnp.sum(e, axis=0, keepdims=True)
    num = jnp.sum(e[0:layer + 1], axis=0, keepdims=True)
    return num / den


def _softplus(x):
    return jnp.maximum(x, 0.0) + jnp.log1p(jnp.exp(-jnp.abs(x)))


def _lru_coeffs(r_pre, i_pre, xc, lam):
    r = _sigmoid(r_pre)
    ig = _sigmoid(i_pre)
    log_a = (-LRU_C) * r * _softplus(-lam)
    a = jnp.exp(log_a)
    th = jnp.tanh(log_a)
    mult = jnp.sqrt(-2.0 * th / (1.0 - th))
    return a, mult * (ig * xc)


def _split3(x):
    p1 = x.astype(BF16)
    r1 = x - p1.astype(F32)
    p2 = r1.astype(BF16)
    r2 = r1 - p2.astype(F32)
    return p1, p2, r2.astype(BF16)


def _cast_rows(src_refs, dst_refs):
    for src, dst in zip(src_refs, dst_refs):
        dst[...] = src[...].astype(BF16)


def _cast_specs(weights, steps, index):
    specs, shapes = [], []
    for w in weights:
        block = w.shape[0] // steps
        assert w.ndim == 2 and w.shape[0] % steps == 0 and block % (2 * SUBLANES) == 0
        specs.append(pl.BlockSpec((block, w.shape[1]), index))
        shapes.append(jax.ShapeDtypeStruct(w.shape, BF16))
    return specs, shapes


def _ffn_kernel(xa_ref, xb_ref, g_ref, wg_ref, wu_ref, wd_ref, gf_ref, *rest, final_norm, a_tiles):
    n_cast = (len(rest) - 2) // 2
    oa_ref, ob_ref = rest[n_cast:n_cast + 2]
    _cast_rows(rest[:n_cast], rest[n_cast + 2:])

    def half_step(x_ref, o_ref):
        x = x_ref[...]
        inv_rms = lax.rsqrt(jnp.mean(x * x, axis=-1, keepdims=True) + EPS)
        h = (x * g_ref[...]).astype(BF16)
        gate = _dot(h, wg_ref[...]) * inv_rms
        up = _dot(h, wu_ref[...]) * inv_rms
        act = (_silu(gate) * up).astype(BF16)
        y = x + 0.5 * _dot(act, wd_ref[...])
        if final_norm:
            y = _rms(y, gf_ref[...])
        o_ref[...] = y

    i = pl.program_id(0)

    @pl.when(i < a_tiles)
    def _():
        half_step(xa_ref, oa_ref)

    @pl.when(i == a_tiles)
    def _():
        half_step(xb_ref, ob_ref)


def _resident(shape):
    nd = len(shape)
    return pl.BlockSpec(shape, lambda *_: (0,) * nd, pipeline_mode=pl.Buffered(1))


def _ffn(xa, xb, g, wg, wu, wd, gf, *, final_norm, cast=()):
    na, nb = xa.shape[0], xb.shape[0]
    assert na % FFN_ROWS == 0 and nb <= FFN_ROWS and nb % SUBLANES == 0
    a_tiles = na // FFN_ROWS
    a_tile = lambda i: (jnp.minimum(i, a_tiles - 1), 0)
    b_whole = lambda i: (0, 0)
    cast_specs, cast_shapes = _cast_specs(cast, a_tiles, a_tile)
    return pl.pallas_call(
        functools.partial(_ffn_kernel, final_norm=final_norm, a_tiles=a_tiles),
        grid=(a_tiles + 1,),
        in_specs=[
            pl.BlockSpec((FFN_ROWS, D_MODEL), a_tile),
            pl.BlockSpec((nb, D_MODEL), b_whole),
            _resident((1, D_MODEL)),
            _resident((D_MODEL, D_FF)),
            _resident((D_MODEL, D_FF)),
            _resident((D_FF, D_MODEL)),
            _resident((1, D_MODEL)),
            *cast_specs,
        ],
        out_specs=(pl.BlockSpec((FFN_ROWS, D_MODEL), a_tile),
                   pl.BlockSpec((nb, D_MODEL), b_whole),
                   *cast_specs),
        out_shape=(jax.ShapeDtypeStruct((na, D_MODEL), F32),
                   jax.ShapeDtypeStruct((nb, D_MODEL), F32),
                   *cast_shapes),
        name="ffn_final" if final_norm else "ffn",
        compiler_params=pltpu.CompilerParams(
            dimension_semantics=("arbitrary",), vmem_limit_bytes=VMEM_LIMIT_BYTES),
    )(xa, xb, g, wg, wu, wd, gf, *cast)


SCAN_SEG = 4
SCAN_GROUP = SCAN_SEG * SUBLANES
CONV_PAD = SUBLANES
PIECE_ROWS = 2 * SUBLANES
QA_W = 2 * LANES


MIX_INPUTS = 14
MIX_OUTPUTS = 4
MIX_SCRATCH = 6


def _mix_prompt_kernel(*refs):
    n_cast = (len(refs) - MIX_INPUTS - MIX_OUTPUTS - MIX_SCRATCH) // 2
    (x_ref, nmix_ref, win_ref, lbraw_ref, onorm_ref, convw_ref, convb_ref,
     wax_ref, ba_ref, bx_ref, lam_ref, waup_ref, wbup_ref, wout_ref) = refs[:MIX_INPUTS]
    outs = refs[MIX_INPUTS + n_cast:]
    xo_ref, s_ref, hl_ref, cv_ref = outs[:MIX_OUTPUTS]
    qa_s, ud_s, vb_s, oa_s, xr_s, h_s = outs[MIX_OUTPUTS + n_cast:]
    _cast_rows(refs[MIX_INPUTS:MIX_INPUTS + n_cast], outs[MIX_OUTPUTS:MIX_OUTPUTS + n_cast])

    rows = MIX_ROWS
    t = pl.program_id(1)

    @pl.when(t == 0)
    def _():
        s_ref[...] = jnp.zeros_like(s_ref)
        hl_ref[...] = jnp.zeros_like(hl_ref)
        xr_s[:, 0:CONV_PAD, :] = jnp.zeros((LRU_BLOCKS, CONV_PAD, LRU_BW), F32)

    x = x_ref[0]
    hb = _rms(x, nmix_ref[...]).astype(BF16)

    def proj(i):
        return _dot(hb, win_ref[:, i * D_MODEL:(i + 1) * D_MODEL])

    early = {}

    ncb = D_MODEL // MXU_COLS

    def issue_proj(i, cb):
        if cb < ncb and (i, cb) not in early:
            c0 = i * D_MODEL + cb * MXU_COLS
            early[(i, cb)] = _dot(hb, win_ref[:, c0:c0 + MXU_COLS])

    def gather(i):
        for cb in range(ncb):
            issue_proj(i, cb)
        return jnp.concatenate([early[(i, cb)] for cb in range(ncb)], axis=1)

    lb = _lower_bound(lbraw_ref[...], 0)
    q_act = _silu(proj(0))
    f = lb + (1.0 - lb) * _sigmoid(proj(1))
    kk = 1.0 - f
    vb_s[...] = proj(2).astype(BF16)

    ri = lax.broadcasted_iota(jnp.int32, (rows, rows), 0)
    ci = lax.broadcasted_iota(jnp.int32, (rows, rows), 1)
    same_chunk = (ri // CHUNK) == (ci // CHUNK)
    tril_blk = jnp.where((ri >= ci) & same_chunk, 1.0, 0.0).astype(BF16)
    logf = jnp.log(f)
    g1 = logf.astype(BF16)
    g2 = (logf - g1.astype(F32)).astype(BF16)
    b_all = _dot(tril_blk, g1) + _dot(tril_blk, g2)

    causal = (lax.broadcasted_iota(jnp.int32, (CHUNK, CHUNK), 0)
              >= lax.broadcasted_iota(jnp.int32, (CHUNK, CHUNK), 1))
    piece_row = lax.broadcasted_iota(jnp.int32, (PIECE_ROWS, D_MODEL), 0)
    ones_pv = jnp.ones((3 * PIECE_ROWS, DV), BF16)
    zeros_pv = jnp.zeros((3 * PIECE_ROWS, DV), BF16)
    zeros_cv = jnp.zeros((CHUNK, DV), BF16)

    nchunk = rows // CHUNK
    last = piece_row == PIECE_ROWS - 1
    zero_p = jnp.zeros((PIECE_ROWS, D_MODEL), BF16)

    for c in range(nchunk):
        issue_proj(3, c)
        rs = slice(c * CHUNK, (c + 1) * CHUNK)
        b = b_all[rs, :]
        bl = b[CHUNK - 1:CHUNK, :]
        qe = (q_act[rs, :] * jnp.exp(b)).astype(BF16)
        ke = (kk[rs, :] * jnp.exp(-b)).astype(BF16)
        kd = (kk[rs, :] * jnp.exp(bl - b)).astype(BF16)
        pieces = jnp.concatenate(
            [jnp.where(last, e, zero_p) for e in _split3(jnp.exp(b[CHUNK - PIECE_ROWS:, :]))], axis=0)
        for h in range(HEADS):
            hs = slice(h * DK, (h + 1) * DK)
            qa_s[rs, h * QA_W:h * QA_W + DK] = qe[:, hs]
            att = jnp.where(causal, _dot_nt(qe[:, hs], ke[:, hs]), 0.0)
            qa_s[rs, h * QA_W + DK:h * QA_W + DK + CHUNK] = att.astype(BF16)
            lhs = jnp.concatenate([kd[:, hs], pieces[:, hs]], axis=0)
            rhs = jnp.concatenate([jnp.concatenate([vb_s[rs, hs], zeros_cv], axis=1),
                                   jnp.concatenate([zeros_pv, ones_pv], axis=1)], axis=0)
            ud_s[c * HEADS + h] = _dot_tn(lhs, rhs)

    for c in range(nchunk):
        rs = slice(c * CHUNK, (c + 1) * CHUNK)
        for h in range(HEADS):
            hs = slice(h * DK, (h + 1) * DK)
            s_old = s_ref[0, h]
            rhs = jnp.concatenate([s_old.astype(BF16), vb_s[rs, hs]], axis=0)
            o = _dot(qa_s[rs, h * QA_W:h * QA_W + DK + CHUNK], rhs)
            oa_s[rs, hs] = o * lax.rsqrt(jnp.mean(o * o, axis=-1, keepdims=True) + EPS)
            ud = ud_s[c * HEADS + h]
            s_ref[0, h] = ud[:, DV:] * s_old + ud[:, 0:DV]

    o_a = (oa_s[...] * onorm_ref[...] * _silu(gather(3))).astype(BF16)
    up_a = {}

    def issue_up_a(cb):
        if cb < ncb and cb not in up_a:
            up_a[cb] = _dot(o_a, waup_ref[:, cb * MXU_COLS:(cb + 1) * MXU_COLS])

    xr = proj(4)
    h_prev = hl_ref[0]
    sub = lax.broadcasted_iota(jnp.int32, (SUBLANES, LANES), 0)
    ngroup = rows // SCAN_GROUP
    tails = []
    h_last = []
    for n in range(LRU_BLOCKS):
        ns = slice(n * LRU_BW, (n + 1) * LRU_BW)
        xr_s[n, CONV_PAD:CONV_PAD + rows, :] = xr[:, ns]
        taps = [jnp.broadcast_to(convw_ref[j:j + 1, ns], (SUBLANES, LRU_BW)) for j in range(CONV_W)]
        bias = jnp.broadcast_to(convb_ref[:, ns], (SUBLANES, LRU_BW))
        xc_parts = []
        for grp in range(ngroup):
            xin = {d: xr_s[n, pl.ds(CONV_PAD + grp * SCAN_GROUP + d, SUBLANES, stride=SCAN_SEG), :]
                   for d in range(1 - CONV_W, SCAN_SEG)}
            for j in range(SCAN_SEG):
                acc = bias
                for tap in range(CONV_W):
                    acc = acc + xin[j + tap - (CONV_W - 1)] * taps[tap]
                xc_parts.append(acc)
        xc = jnp.concatenate(xc_parts, axis=0)
        tail = xr_s[n, CONV_PAD + rows - (CONV_W - 1):CONV_PAD + rows, :]
        tails.append(tail)
        xr_s[n, CONV_PAD - (CONV_W - 1):CONV_PAD, :] = tail
        pre = _dot(xc.astype(BF16), wax_ref[n])
        issue_proj(5 + n % 2, n // 2)
        a, bterm = _lru_coeffs(pre[:, :LRU_BW] + ba_ref[:, ns], pre[:, LRU_BW:] + bx_ref[:, ns],
                               xc, lam_ref[:, ns])
        carry = h_prev[:, ns]
        for grp in range(ngroup):
            def vreg(v, j, grp=grp):
                r0 = grp * SCAN_GROUP + j * SUBLANES
                return v[r0:r0 + SUBLANES, :]

            def strided(j, grp=grp):
                return pl.ds(grp * SCAN_GROUP + j, SUBLANES, stride=SCAN_SEG)
            pa = [vreg(a, 0)]
            hh = [vreg(bterm, 0)]
            for j in range(1, SCAN_SEG):
                aj = vreg(a, j)
                hh.append(aj * hh[-1] + vreg(bterm, j))
                pa.append(aj * pa[-1])
            pi, hi = pa[-1], hh[-1]
            for d in (1, 2, 4):
                keep = sub >= d
                hi = pi * jnp.where(keep, pltpu.roll(hi, d, 0), 0.0) + hi
                pi = pi * jnp.where(keep, pltpu.roll(pi, d, 0), 1.0)
            first = sub >= 1
            seg_in = (jnp.where(first, pltpu.roll(pi, 1, 0), 1.0) * carry
                      + jnp.where(first, pltpu.roll(hi, 1, 0), 0.0))
            carry = pi[SUBLANES - 1:, :] * carry + hi[SUBLANES - 1:, :]
            for j in range(SCAN_SEG):
                h_s[n, strided(j), :] = pa[j] * seg_in + hh[j]
        h_last.append(carry)
        if n % 2 == 0:
            issue_proj(7, n // 2)
        else:
            issue_up_a(n // 2)
    hl_ref[0] = jnp.concatenate(h_last, axis=1)
    cv_ref[0] = jnp.concatenate(tails, axis=1)
    gel = _gelu_tanh(gather(5))
    o_b = jnp.concatenate(
        [h_s[n] * gel[:, n * LRU_BW:(n + 1) * LRU_BW] for n in range(LRU_BLOCKS)], axis=1).astype(BF16)

    for cb in range(ncb):
        issue_up_a(cb)
    m = (_sigmoid(gather(6)) * jnp.concatenate([up_a[cb] for cb in range(ncb)], axis=1)
         + _sigmoid(gather(7)) * _dot(o_b, wbup_ref[...]))
    xo_ref[0] = x + _dot(m.astype(BF16), wout_ref[...])


def _mix_prompt(x, p, cast=()):
    batch, seq, _ = x.shape
    rows = MIX_ROWS
    assert seq % rows == 0 and rows % CHUNK == 0 and rows >= CONV_W - 1
    tiles = seq // rows
    row_vec = _resident((1, D_MODEL))
    cast_specs, cast_shapes = _cast_specs(cast, batch * tiles, lambda b, t: (b * tiles + t, 0))
    out_shape = (
        jax.ShapeDtypeStruct((batch, seq, D_MODEL), F32),
        jax.ShapeDtypeStruct((batch, HEADS, DK, DV), F32),
        jax.ShapeDtypeStruct((batch, 1, D_MODEL), F32),
        jax.ShapeDtypeStruct((batch, CONV_W - 1, D_MODEL), F32),
        *cast_shapes,
    )
    return pl.pallas_call(
        _mix_prompt_kernel,
        grid=(batch, seq // rows),
        in_specs=[
            pl.BlockSpec((1, rows, D_MODEL), lambda b, t: (b, t, 0)),
            row_vec,
            _resident((D_MODEL, N_SPLITS * D_MODEL)),
            _resident(p["lb_raw"].shape),
            row_vec,
            _resident((CONV_W, D_MODEL)),
            row_vec,
            _resident((LRU_BLOCKS, LRU_BW, 2 * LRU_BW)),
            row_vec, row_vec, row_vec,
            _resident((D_MODEL, D_MODEL)),
            _resident((D_MODEL, D_MODEL)),
            _resident((D_MODEL, D_MODEL)),
            *cast_specs,
        ],
        out_specs=(
            pl.BlockSpec((1, rows, D_MODEL), lambda b, t: (b, t, 0)),
            pl.BlockSpec((1, HEADS, DK, DV), lambda b, t: (b, 0, 0, 0)),
            pl.BlockSpec((1, 1, D_MODEL), lambda b, t: (b, 0, 0)),
            pl.BlockSpec((1, CONV_W - 1, D_MODEL), lambda b, t: (b, 0, 0)),
            *cast_specs,
        ),
        out_shape=out_shape,
        scratch_shapes=[
            pltpu.VMEM((rows, HEADS * QA_W), BF16),
            pltpu.VMEM((rows // CHUNK * HEADS, DK, 2 * DV), F32),
            pltpu.VMEM((rows, D_MODEL), BF16),
            pltpu.VMEM((rows, D_MODEL), F32),
            pltpu.VMEM((LRU_BLOCKS, CONV_PAD + rows, LRU_BW), F32),
            pltpu.VMEM((LRU_BLOCKS, rows, LRU_BW), F32),
        ],
        name="mix_prompt",
        compiler_params=pltpu.CompilerParams(
            dimension_semantics=("arbitrary", "arbitrary"), vmem_limit_bytes=VMEM_LIMIT_BYTES),
    )(x, p["mix_norm"], p["w_in"], p["lb_raw"], p["o_norm"], p["conv_w"], p["conv_b"],
      p["w_ax"], p["b_a"], p["b_x"], p["lam"], p["w_a_up"], p["w_b_up"], p["w_out"], *cast)


def _mix_sample_kernel(x_ref, nmix_ref, win_ref, lbraw_ref, onorm_ref, convw_ref, convb_ref,
                       wax_ref, ba_ref, bx_ref, lam_ref, waup_ref, wbup_ref, wout_ref,
                       sin_ref, hlin_ref, cvin_ref,
                       xo_ref, sout_ref, hl_ref, cv_ref,
                       qt_s, ft_s, v_s, oa_s, gsil_s, ga_s, mb_s):
    nseq = x_ref.shape[0]
    blk = SAMPLE_STATE_BLOCK
    j = pl.program_id(0)

    @pl.when(j == 0)
    def _():
        x = x_ref[...]
        hb = _rms(x, nmix_ref[...]).astype(BF16)

        def proj(i):
            return _dot(hb, win_ref[:, i * D_MODEL:(i + 1) * D_MODEL])

        lb = _lower_bound(lbraw_ref[...], 0)
        qt_s[...] = _silu(proj(0)).T
        ft_s[...] = (lb + (1.0 - lb) * _sigmoid(proj(1))).T
        v_s[...] = proj(2)
        gsil_s[...] = onorm_ref[...] * _silu(proj(3))

        xr = proj(4)
        xc = convb_ref[...] + xr * convw_ref[CONV_W - 1:CONV_W, :]
        for i in range(CONV_W - 1):
            xc = xc + cvin_ref[:, i, :] * convw_ref[i:i + 1, :]
        for i in range(CONV_W - 2):
            cv_ref[:, i, :] = cvin_ref[:, i + 1, :]
        cv_ref[:, CONV_W - 2, :] = xr

        xcb = xc.astype(BF16)
        h_parts = []
        for n in range(LRU_BLOCKS):
            ns = slice(n * LRU_BW, (n + 1) * LRU_BW)
            pre = _dot(xcb[:, ns], wax_ref[n])
            a, bterm = _lru_coeffs(pre[:, :LRU_BW] + ba_ref[:, ns], pre[:, LRU_BW:] + bx_ref[:, ns],
                                   xc[:, ns], lam_ref[:, ns])
            h_parts.append(a * hlin_ref[:, ns] + bterm)
        h_new = jnp.concatenate(h_parts, axis=1)
        hl_ref[...] = h_new
        o_b = (h_new * _gelu_tanh(proj(5))).astype(BF16)
        ga_s[...] = _sigmoid(proj(6))
        mb_s[...] = _sigmoid(proj(7)) * _dot(o_b, wbup_ref[...])

    b0 = pl.multiple_of(j * blk, blk)
    shifts = [lax.rem(nseq - b0 + p * blk, nseq) for p in range(3)]

    lane = lax.broadcasted_iota(jnp.int32, (DK, nseq), 1)
    sel_r = lax.broadcasted_iota(jnp.int32, (nseq, blk * DV), 0)
    sel_c = lax.broadcasted_iota(jnp.int32, (nseq, blk * DV), 1)
    selector = jnp.where((sel_r < 3 * blk) & (jnp.bitwise_and(sel_r, blk - 1) == sel_c // DV),
                         1.0, 0.0).astype(BF16)

    def spread(cols_t):
        p1 = cols_t.astype(BF16).astype(F32)
        r1 = cols_t - p1
        p2 = r1.astype(BF16).astype(F32)
        p3 = r1 - p2
        lhs = jnp.where(lane < blk, pltpu.roll(p1, shifts[0], 1),
                        jnp.where(lane < 2 * blk, pltpu.roll(p2, shifts[1], 1),
                                  jnp.where(lane < 3 * blk, pltpu.roll(p3, shifts[2], 1), 0.0)))
        return _dot(lhs.astype(BF16), selector)

    for h in range(HEADS):
        hrows = slice(h * DK, (h + 1) * DK)
        q_all = spread(qt_s[hrows, :])
        f_all = spread(ft_s[hrows, :])
        v_rows = v_s[pl.ds(b0, blk), hrows]
        o_rows = []
        for i in range(blk):
            f_col = f_all[:, i * DV:(i + 1) * DV]
            s_new = f_col * sin_ref[i, h] + (1.0 - f_col) * v_rows[i:i + 1, :]
            sout_ref[i, h] = s_new
            o_rows.append(jnp.sum(q_all[:, i * DV:(i + 1) * DV] * s_new, axis=0, keepdims=True))
        o = jnp.concatenate(o_rows, axis=0)
        oa_s[pl.ds(b0, blk), hrows] = o * lax.rsqrt(
            jnp.mean(o * o, axis=-1, keepdims=True) + EPS)

    @pl.when(j == pl.num_programs(0) - 1)
    def _():
        o_a = (oa_s[...] * gsil_s[...]).astype(BF16)
        m = ga_s[...] * _dot(o_a, waup_ref[...]) + mb_s[...]
        xo_ref[...] = x_ref[...] + _dot(m.astype(BF16), wout_ref[...])


def _mix_sample(x2d, s_in, h_in, c_in, p):
    nseq = x2d.shape[0]
    blk = SAMPLE_STATE_BLOCK
    assert nseq == LANES and nseq % blk == 0
    row_vec = _resident((1, D_MODEL))
    tok = _resident((nseq, D_MODEL))
    conv_state = _resident((nseq, CONV_W - 1, D_MODEL))
    out_shape = (
        jax.ShapeDtypeStruct((nseq, D_MODEL), F32),
        jax.ShapeDtypeStruct((nseq, HEADS, DK, DV), F32),
        jax.ShapeDtypeStruct((nseq, D_MODEL), F32),
        jax.ShapeDtypeStruct((nseq, CONV_W - 1, D_MODEL), F32),
    )
    state_spec = pl.BlockSpec((blk, HEADS, DK, DV), lambda j: (j, 0, 0, 0))
    return pl.pallas_call(
        _mix_sample_kernel,
        grid=(nseq // blk,),
        in_specs=[
            tok, row_vec,
            _resident((D_MODEL, N_SPLITS * D_MODEL)),
            _resident(p["lb_raw"].shape),
            row_vec,
            _resident((CONV_W, D_MODEL)),
            row_vec,
            _resident((LRU_BLOCKS, LRU_BW, 2 * LRU_BW)),
            row_vec, row_vec, row_vec,
            _resident((D_MODEL, D_MODEL)),
            _resident((D_MODEL, D_MODEL)),
            _resident((D_MODEL, D_MODEL)),
            state_spec, tok, conv_state,
        ],
        out_specs=(
            pl.BlockSpec((nseq, D_MODEL), lambda j: (0, 0)),
            state_spec,
            pl.BlockSpec((nseq, D_MODEL), lambda j: (0, 0)),
            pl.BlockSpec((nseq, CONV_W - 1, D_MODEL), lambda j: (0, 0, 0)),
        ),
        out_shape=out_shape,
        scratch_shapes=[
            pltpu.VMEM((D_MODEL, nseq), F32),
            pltpu.VMEM((D_MODEL, nseq), F32),
            pltpu.VMEM((nseq, D_MODEL), F32),
            pltpu.VMEM((nseq, D_MODEL), F32),
            pltpu.VMEM((nseq, D_MODEL), F32),
            pltpu.VMEM((nseq, D_MODEL), F32),
            pltpu.VMEM((nseq, D_MODEL), F32),
        ],
        name="mix_sample",
        compiler_params=pltpu.CompilerParams(
            dimension_semantics=("arbitrary",), vmem_limit_bytes=VMEM_LIMIT_BYTES),
    )(x2d, p["mix_norm"], p["w_in"], p["lb_raw"], p["o_norm"], p["conv_w"], p["conv_b"],
      p["w_ax"], p["b_a"], p["b_x"], p["lam"], p["w_a_up"], p["w_b_up"], p["w_out"],
      s_in, h_in, c_in)


def kernel(x_prompt, x_sample, state_hgrn, state_lru, state_conv, ffn1_norm, ffn1_w_gate, ffn1_w_up, ffn1_w_down, mix_norm, w_in, hgrn_lower_bounds, hgrn_out_norm, conv_w, conv_b, lru_w_a, lru_b_a, lru_w_x, lru_b_x, lru_lambda, w_a_up, w_b_up, w_out, ffn2_norm, ffn2_w_gate, ffn2_w_up, ffn2_w_down, final_norm):
    depth = w_in.shape[0]
    assert depth == 1, "single-layer trunk"
    l = 0
    row = lambda v: v.reshape(1, -1).astype(F32)
    bf = lambda w: w.astype(BF16)
    p = {
        "mix_norm": row(mix_norm[l]),
        "lb_raw": hgrn_lower_bounds.astype(F32),
        "o_norm": row(hgrn_out_norm[l]),
        "conv_w": conv_w[l].astype(F32),
        "conv_b": row(conv_b[l]),
        "b_a": row(lru_b_a[l]),
        "b_x": row(lru_b_x[l]),
        "lam": row(lru_lambda[l]),
    }
    f1 = (row(ffn1_norm[l]), bf(ffn1_w_gate[l]), bf(ffn1_w_up[l]), bf(ffn1_w_down[l]))
    gfin = row(final_norm)

    bp, tp, _ = x_prompt.shape
    bs, ts, _ = x_sample.shape
    assert ts == 1
    w_ax = jnp.concatenate([lru_w_a[l], lru_w_x[l]], axis=-1)
    xp, xs, p["w_in"], p["w_a_up"], p["w_b_up"], p["w_out"], w_ax = _ffn(
        x_prompt.reshape(bp * tp, D_MODEL), x_sample.reshape(bs, D_MODEL), *f1, gfin,
        final_norm=False,
        cast=(w_in[l], w_a_up[l], w_b_up[l], w_out[l], w_ax.reshape(LRU_BLOCKS * LRU_BW, -1)))
    p["w_ax"] = w_ax.reshape(LRU_BLOCKS, LRU_BW, 2 * LRU_BW)
    xp, s_p, h_p, c_p, wg2, wu2, wd2 = _mix_prompt(
        xp.reshape(bp, tp, D_MODEL), p,
        cast=(ffn2_w_gate[l], ffn2_w_up[l], ffn2_w_down[l].reshape(D_MODEL, D_FF)))
    f2 = (row(ffn2_norm[l]), wg2, wu2, wd2.reshape(D_FF, D_MODEL))
    xs, s_s, h_s, c_s = _mix_sample(xs, state_hgrn[l], state_lru[l], state_conv[l], p)
    y_p, y_s = _ffn(xp.reshape(bp * tp, D_MODEL), xs, *f2, gfin, final_norm=True)
    y_p = y_p.reshape(bp, tp, D_MODEL)
    y_s = y_s.reshape(bs, ts, D_MODEL)

    return (y_p, y_s, s_p[None], h_p.reshape(bp, D_MODEL)[None], c_p[None],
            s_s[None], h_s[None], c_s[None])
```

```python
import functools

import jax
import jax.numpy as jnp
from jax import lax
from jax.experimental import pallas as pl
from jax.experimental.pallas import tpu as pltpu

D_MODEL = 1024
HEADS = 8
DK = 128
DV = 128
CHUNK = 64
LRU_BLOCKS = 8
LRU_BW = 128
LRU_C = 8.0
CONV_W = 4
D_FF = 2816
EPS = 1e-6
LOG2_E = 1.4426950408889634
N_SPLITS = 8

SUBLANES = 8
LANES = 128
MXU_COLS = 256
VMEM_LIMIT_BYTES = 56 * 1024 * 1024

FFN_ROWS = 512
MIX_ROWS = 256
SAMPLE_STATE_BLOCK = 8

BF16 = jnp.bfloat16
F32 = jnp.float32


def _dot(a, b):
    return jnp.dot(a, b, preferred_element_type=F32)


def _dot_tn(a, b):
    return lax.dot_general(a, b, (((0,), (0,)), ((), ())), preferred_element_type=F32)


def _dot_nt(a, b):
    return lax.dot_general(a, b, (((1,), (1,)), ((), ())), preferred_element_type=F32)


def _sigmoid(x):
    return 1.0 / (1.0 + jnp.exp2(x * (-LOG2_E)))


def _silu(x):
    return x * _sigmoid(x)


def _gelu_tanh(x):
    c = 0.7978845608028654
    return 0.5 * x * (1.0 + jnp.tanh(c * (x + 0.044715 * (x * x * x))))


def _rms(x, g):
    return x * lax.rsqrt(jnp.mean(x * x, axis=-1, keepdims=True) + EPS) * g


def _lower_bound(raw, layer):
    m = jnp.max(raw, axis=0, keepdims=True)
    e = jnp.exp(raw - m)
    den = jnp.sum(e, axis=0, keepdims=True)
    num = jnp.sum(e[0:layer + 1], axis=0, keepdims=True)
    return num / den


def _softplus(x):
    return jnp.maximum(x, 0.0) + jnp.log1p(jnp.exp(-jnp.abs(x)))


def _lru_coeffs(r_pre, i_pre, xc, lam):
    r = _sigmoid(r_pre)
    ig = _sigmoid(i_pre)
    log_a = (-LRU_C) * r * _softplus(-lam)
    a = jnp.exp(log_a)
    th = jnp.tanh(log_a)
    mult = jnp.sqrt(-2.0 * th / (1.0 - th))
    return a, mult * (ig * xc)


def _split3(x):
    p1 = x.astype(BF16)
    r1 = x - p1.astype(F32)
    p2 = r1.astype(BF16)
    r2 = r1 - p2.astype(F32)
    return p1, p2, r2.astype(BF16)


def _cast_rows(src_refs, dst_refs):
    for src, dst in zip(src_refs, dst_refs):
        dst[...] = src[...].astype(BF16)


def _cast_specs(weights, steps, step_index):
    specs, shapes = [], []
    for w in weights:
        assert w.ndim == 2
        count = max(c for c in range(1, steps + 1)
                    if w.shape[0] % c == 0 and (w.shape[0] // c) % (2 * SUBLANES) == 0)
        index = functools.partial(
            lambda *ids, last: (jnp.minimum(step_index(*ids), last), 0), last=count - 1)
        specs.append(pl.BlockSpec((w.shape[0] // count, w.shape[1]), index))
        shapes.append(jax.ShapeDtypeStruct(w.shape, BF16))
    return specs, shapes


def _ffn_kernel(xa_ref, xb_ref, g_ref, wg_ref, wu_ref, wd_ref, gf_ref, *rest, final_norm, a_tiles):
    n_cast = (len(rest) - 2) // 2
    oa_ref, ob_ref = rest[n_cast:n_cast + 2]
    _cast_rows(rest[:n_cast], rest[n_cast + 2:])

    def half_step(x_ref, o_ref):
        x = x_ref[...]
        inv_rms = lax.rsqrt(jnp.mean(x * x, axis=-1, keepdims=True) + EPS)
        h = (x * g_ref[...]).astype(BF16)
        gate = _dot(h, wg_ref[...]) * inv_rms
        up = _dot(h, wu_ref[...]) * inv_rms
        act = (_silu(gate) * up).astype(BF16)
        y = x + 0.5 * _dot(act, wd_ref[...])
        if final_norm:
            y = _rms(y, gf_ref[...])
        o_ref[...] = y

    i = pl.program_id(0)

    @pl.when(i < a_tiles)
    def _():
        half_step(xa_ref, oa_ref)

    @pl.when(i == a_tiles)
    def _():
        half_step(xb_ref, ob_ref)


def _resident(shape):
    nd = len(shape)
    return pl.BlockSpec(shape, lambda *_: (0,) * nd, pipeline_mode=pl.Buffered(1))


def _ffn(xa, xb, g, wg, wu, wd, gf, *, final_norm, cast=()):
    na, nb = xa.shape[0], xb.shape[0]
    assert na % FFN_ROWS == 0 and nb <= FFN_ROWS and nb % SUBLANES == 0
    a_tiles = na // FFN_ROWS
    a_tile = lambda i: (jnp.minimum(i, a_tiles - 1), 0)
    b_whole = lambda i: (0, 0)
    cast_specs, cast_shapes = _cast_specs(cast, a_tiles, lambda i: i)
    return pl.pallas_call(
        functools.partial(_ffn_kernel, final_norm=final_norm, a_tiles=a_tiles),
        grid=(a_tiles + 1,),
        in_specs=[
            pl.BlockSpec((FFN_ROWS, D_MODEL), a_tile),
            pl.BlockSpec((nb, D_MODEL), b_whole),
            _resident((1, D_MODEL)),
            _resident((D_MODEL, D_FF)),
            _resident((D_MODEL, D_FF)),
            _resident((D_FF, D_MODEL)),
            _resident((1, D_MODEL)),
            *cast_specs,
        ],
        out_specs=(pl.BlockSpec((FFN_ROWS, D_MODEL), a_tile),
                   pl.BlockSpec((nb, D_MODEL), b_whole),
                   *cast_specs),
        out_shape=(jax.ShapeDtypeStruct((na, D_MODEL), F32),
                   jax.ShapeDtypeStruct((nb, D_MODEL), F32),
                   *cast_shapes),
        name="ffn_final" if final_norm else "ffn",
        compiler_params=pltpu.CompilerParams(
            dimension_semantics=("arbitrary",), vmem_limit_bytes=VMEM_LIMIT_BYTES),
    )(xa, xb, g, wg, wu, wd, gf, *cast)


SCAN_SEG = 4
SCAN_GROUP = SCAN_SEG * SUBLANES
CONV_PAD = SUBLANES
PIECE_ROWS = 2 * SUBLANES
QA_W = 2 * LANES


MIX_INPUTS = 14
MIX_OUTPUTS = 4
MIX_SCRATCH = 6


def _mix_prompt_kernel(*refs):
    n_cast = (len(refs) - MIX_INPUTS - MIX_OUTPUTS - MIX_SCRATCH) // 2
    (x_ref, nmix_ref, win_ref, lbraw_ref, onorm_ref, convw_ref, convb_ref,
     wax_ref, ba_ref, bx_ref, lam_ref, waup_ref, wbup_ref, wout_ref) = refs[:MIX_INPUTS]
    outs = refs[MIX_INPUTS + n_cast:]
    xo_ref, s_ref, hl_ref, cv_ref = outs[:MIX_OUTPUTS]
    qa_s, ud_s, vb_s, oa_s, xr_s, h_s = outs[MIX_OUTPUTS + n_cast:]
    _cast_rows(refs[MIX_INPUTS:MIX_INPUTS + n_cast], outs[MIX_OUTPUTS:MIX_OUTPUTS + n_cast])

    rows = MIX_ROWS
    t = pl.program_id(1)

    @pl.when(t == 0)
    def _():
        s_ref[...] = jnp.zeros_like(s_ref)
        hl_ref[...] = jnp.zeros_like(hl_ref)
        xr_s[:, 0:CONV_PAD, :] = jnp.zeros((LRU_BLOCKS, CONV_PAD, LRU_BW), F32)

    x = x_ref[0]
    hb = _rms(x, nmix_ref[...]).astype(BF16)

    def proj(i):
        return _dot(hb, win_ref[:, i * D_MODEL:(i + 1) * D_MODEL])

    early = {}

    ncb = D_MODEL // MXU_COLS

    def issue_proj(i, cb):
        if cb < ncb and (i, cb) not in early:
            c0 = i * D_MODEL + cb * MXU_COLS
            early[(i, cb)] = _dot(hb, win_ref[:, c0:c0 + MXU_COLS])

    def gather(i):
        for cb in range(ncb):
            issue_proj(i, cb)
        return jnp.concatenate([early[(i, cb)] for cb in range(ncb)], axis=1)

    lb = _lower_bound(lbraw_ref[...], 0)
    q_act = _silu(proj(0))
    f = lb + (1.0 - lb) * _sigmoid(proj(1))
    kk = 1.0 - f
    vb_s[...] = proj(2).astype(BF16)

    ri = lax.broadcasted_iota(jnp.int32, (rows, rows), 0)
    ci = lax.broadcasted_iota(jnp.int32, (rows, rows), 1)
    same_chunk = (ri // CHUNK) == (ci // CHUNK)
    tril_blk = jnp.where((ri >= ci) & same_chunk, 1.0, 0.0).astype(BF16)
    logf = jnp.log(f)
    g1 = logf.astype(BF16)
    g2 = (logf - g1.astype(F32)).astype(BF16)
    b_all = _dot(tril_blk, g1) + _dot(tril_blk, g2)

    causal = (lax.broadcasted_iota(jnp.int32, (CHUNK, CHUNK), 0)
              >= lax.broadcasted_iota(jnp.int32, (CHUNK, CHUNK), 1))
    piece_row = lax.broadcasted_iota(jnp.int32, (PIECE_ROWS, D_MODEL), 0)
    ones_pv = jnp.ones((3 * PIECE_ROWS, DV), BF16)
    zeros_pv = jnp.zeros((3 * PIECE_ROWS, DV), BF16)
    zeros_cv = jnp.zeros((CHUNK, DV), BF16)

    nchunk = rows // CHUNK
    last = piece_row == PIECE_ROWS - 1
    zero_p = jnp.zeros((PIECE_ROWS, D_MODEL), BF16)

    for c in range(nchunk):
        issue_proj(3, c)
        rs = slice(c * CHUNK, (c + 1) * CHUNK)
        b = b_all[rs, :]
        bl = b[CHUNK - 1:CHUNK, :]
        qe = (q_act[rs, :] * jnp.exp(b)).astype(BF16)
        ke = (kk[rs, :] * jnp.exp(-b)).astype(BF16)
        kd = (kk[rs, :] * jnp.exp(bl - b)).astype(BF16)
        pieces = jnp.concatenate(
            [jnp.where(last, e, zero_p) for e in _split3(jnp.exp(b[CHUNK - PIECE_ROWS:, :]))], axis=0)
        for h in range(HEADS):
            hs = slice(h * DK, (h + 1) * DK)
            qa_s[rs, h * QA_W:h * QA_W + DK] = qe[:, hs]
            att = jnp.where(causal, _dot_nt(qe[:, hs], ke[:, hs]), 0.0)
            qa_s[rs, h * QA_W + DK:h * QA_W + DK + CHUNK] = att.astype(BF16)
            lhs = jnp.concatenate([kd[:, hs], pieces[:, hs]], axis=0)
            rhs = jnp.concatenate([jnp.concatenate([vb_s[rs, hs], zeros_cv], axis=1),
                                   jnp.concatenate([zeros_pv, ones_pv], axis=1)], axis=0)
            ud_s[c * HEADS + h] = _dot_tn(lhs, rhs)

    for c in range(nchunk):
        rs = slice(c * CHUNK, (c + 1) * CHUNK)
        for h in range(HEADS):
            hs = slice(h * DK, (h + 1) * DK)
            s_old = s_ref[0, h]
            rhs = jnp.concatenate([s_old.astype(BF16), vb_s[rs, hs]], axis=0)
            o = _dot(qa_s[rs, h * QA_W:h * QA_W + DK + CHUNK], rhs)
            oa_s[rs, hs] = o * lax.rsqrt(jnp.mean(o * o, axis=-1, keepdims=True) + EPS)
            ud = ud_s[c * HEADS + h]
            s_ref[0, h] = ud[:, DV:] * s_old + ud[:, 0:DV]

    o_a = (oa_s[...] * onorm_ref[...] * _silu(gather(3))).astype(BF16)
    up_a = {}

    def issue_up_a(cb):
        if cb < ncb and cb not in up_a:
            up_a[cb] = _dot(o_a, waup_ref[:, cb * MXU_COLS:(cb + 1) * MXU_COLS])

    xr = proj(4)
    h_prev = hl_ref[0]
    sub = lax.broadcasted_iota(jnp.int32, (SUBLANES, LANES), 0)
    ngroup = rows // SCAN_GROUP
    tails = []
    h_last = []
    for n in range(LRU_BLOCKS):
        ns = slice(n * LRU_BW, (n + 1) * LRU_BW)
        xr_s[n, CONV_PAD:CONV_PAD + rows, :] = xr[:, ns]
        taps = [jnp.broadcast_to(convw_ref[j:j + 1, ns], (SUBLANES, LRU_BW)) for j in range(CONV_W)]
        bias = jnp.broadcast_to(convb_ref[:, ns], (SUBLANES, LRU_BW))
        xc_parts = []
        for grp in range(ngroup):
            xin = {d: xr_s[n, pl.ds(CONV_PAD + grp * SCAN_GROUP + d, SUBLANES, stride=SCAN_SEG), :]
                   for d in range(1 - CONV_W, SCAN_SEG)}
            for j in range(SCAN_SEG):
                acc = bias
                for tap in range(CONV_W):
                    acc = acc + xin[j + tap - (CONV_W - 1)] * taps[tap]
                xc_parts.append(acc)
        xc = jnp.concatenate(xc_parts, axis=0)
        tail = xr_s[n, CONV_PAD + rows - (CONV_W - 1):CONV_PAD + rows, :]
        tails.append(tail)
        xr_s[n, CONV_PAD - (CONV_W - 1):CONV_PAD, :] = tail
        pre = _dot(xc.astype(BF16), wax_ref[n])
        issue_proj(5 + n % 2, n // 2)
        a, bterm = _lru_coeffs(pre[:, :LRU_BW] + ba_ref[:, ns], pre[:, LRU_BW:] + bx_ref[:, ns],
                               xc, lam_ref[:, ns])
        carry = h_prev[:, ns]
        for grp in range(ngroup):
            def vreg(v, j, grp=grp):
                r0 = grp * SCAN_GROUP + j * SUBLANES
                return v[r0:r0 + SUBLANES, :]

            def strided(j, grp=grp):
                return pl.ds(grp * SCAN_GROUP + j, SUBLANES, stride=SCAN_SEG)
            pa = [vreg(a, 0)]
            hh = [vreg(bterm, 0)]
            for j in range(1, SCAN_SEG):
                aj = vreg(a, j)
                hh.append(aj * hh[-1] + vreg(bterm, j))
                pa.append(aj * pa[-1])
            pi, hi = pa[-1], hh[-1]
            for d in (1, 2, 4):
                keep = sub >= d
                hi = pi * jnp.where(keep, pltpu.roll(hi, d, 0), 0.0) + hi
                pi = pi * jnp.where(keep, pltpu.roll(pi, d, 0), 1.0)
            first = sub >= 1
            seg_in = (jnp.where(first, pltpu.roll(pi, 1, 0), 1.0) * carry
                      + jnp.where(first, pltpu.roll(hi, 1, 0), 0.0))
            carry = pi[SUBLANES - 1:, :] * carry + hi[SUBLANES - 1:, :]
            for j in range(SCAN_SEG):
                h_s[n, strided(j), :] = pa[j] * seg_in + hh[j]
        h_last.append(carry)
        if n % 2 == 0:
            issue_proj(7, n // 2)
        else:
            issue_up_a(n // 2)
    hl_ref[0] = jnp.concatenate(h_last, axis=1)
    cv_ref[0] = jnp.concatenate(tails, axis=1)
    gel = _gelu_tanh(gather(5))
    o_b = jnp.concatenate(
        [h_s[n] * gel[:, n * LRU_BW:(n + 1) * LRU_BW] for n in range(LRU_BLOCKS)], axis=1).astype(BF16)

    for cb in range(ncb):
        issue_up_a(cb)
    m = (_sigmoid(gather(6)) * jnp.concatenate([up_a[cb] for cb in range(ncb)], axis=1)
         + _sigmoid(gather(7)) * _dot(o_b, wbup_ref[...]))
    xo_ref[0] = x + _dot(m.astype(BF16), wout_ref[...])


def _mix_prompt(x, p, cast=()):
    batch, seq, _ = x.shape
    rows = MIX_ROWS
    assert seq % rows == 0 and rows % CHUNK == 0 and rows >= CONV_W - 1
    tiles = seq // rows
    row_vec = _resident((1, D_MODEL))
    cast_specs, cast_shapes = _cast_specs(cast, batch * tiles, lambda b, t: b * tiles + t)
    out_shape = (
        jax.ShapeDtypeStruct((batch, seq, D_MODEL), F32),
        jax.ShapeDtypeStruct((batch, HEADS, DK, DV), F32),
        jax.ShapeDtypeStruct((batch, 1, D_MODEL), F32),
        jax.ShapeDtypeStruct((batch, CONV_W - 1, D_MODEL), F32),
        *cast_shapes,
    )
    return pl.pallas_call(
        _mix_prompt_kernel,
        grid=(batch, seq // rows),
        in_specs=[
            pl.BlockSpec((1, rows, D_MODEL), lambda b, t: (b, t, 0)),
            row_vec,
            _resident((D_MODEL, N_SPLITS * D_MODEL)),
            _resident(p["lb_raw"].shape),
            row_vec,
            _resident((CONV_W, D_MODEL)),
            row_vec,
            _resident((LRU_BLOCKS, LRU_BW, 2 * LRU_BW)),
            row_vec, row_vec, row_vec,
            _resident((D_MODEL, D_MODEL)),
            _resident((D_MODEL, D_MODEL)),
            _resident((D_MODEL, D_MODEL)),
            *cast_specs,
        ],
        out_specs=(
            pl.BlockSpec((1, rows, D_MODEL), lambda b, t: (b, t, 0)),
            pl.BlockSpec((1, HEADS, DK, DV), lambda b, t: (b, 0, 0, 0)),
            pl.BlockSpec((1, 1, D_MODEL), lambda b, t: (b, 0, 0)),
            pl.BlockSpec((1, CONV_W - 1, D_MODEL), lambda b, t: (b, 0, 0)),
            *cast_specs,
        ),
        out_shape=out_shape,
        scratch_shapes=[
            pltpu.VMEM((rows, HEADS * QA_W), BF16),
            pltpu.VMEM((rows // CHUNK * HEADS, DK, 2 * DV), F32),
            pltpu.VMEM((rows, D_MODEL), BF16),
            pltpu.VMEM((rows, D_MODEL), F32),
            pltpu.VMEM((LRU_BLOCKS, CONV_PAD + rows, LRU_BW), F32),
            pltpu.VMEM((LRU_BLOCKS, rows, LRU_BW), F32),
        ],
        name="mix_prompt",
        compiler_params=pltpu.CompilerParams(
            dimension_semantics=("arbitrary", "arbitrary"), vmem_limit_bytes=VMEM_LIMIT_BYTES),
    )(x, p["mix_norm"], p["w_in"], p["lb_raw"], p["o_norm"], p["conv_w"], p["conv_b"],
      p["w_ax"], p["b_a"], p["b_x"], p["lam"], p["w_a_up"], p["w_b_up"], p["w_out"], *cast)


def _mix_sample_kernel(x_ref, nmix_ref, win_ref, lbraw_ref, onorm_ref, convw_ref, convb_ref,
                       wax_ref, ba_ref, bx_ref, lam_ref, waup_ref, wbup_ref, wout_ref,
                       sin_ref, hlin_ref, cvin_ref,
                       xo_ref, sout_ref, hl_ref, cv_ref,
                       qt_s, ft_s, v_s, oa_s, gsil_s, ga_s, mb_s):
    nseq = x_ref.shape[0]
    blk = SAMPLE_STATE_BLOCK
    j = pl.program_id(0)

    @pl.when(j == 0)
    def _():
        x = x_ref[...]
        hb = _rms(x, nmix_ref[...]).astype(BF16)

        def proj(i):
            return _dot(hb, win_ref[:, i * D_MODEL:(i + 1) * D_MODEL])

        lb = _lower_bound(lbraw_ref[...], 0)
        qt_s[...] = _silu(proj(0)).T
        ft_s[...] = (lb + (1.0 - lb) * _sigmoid(proj(1))).T
        v_s[...] = proj(2)
        gsil_s[...] = onorm_ref[...] * _silu(proj(3))

        xr = proj(4)
        xc = convb_ref[...] + xr * convw_ref[CONV_W - 1:CONV_W, :]
        for i in range(CONV_W - 1):
            xc = xc + cvin_ref[:, i, :] * convw_ref[i:i + 1, :]
        for i in range(CONV_W - 2):
            cv_ref[:, i, :] = cvin_ref[:, i + 1, :]
        cv_ref[:, CONV_W - 2, :] = xr

        xcb = xc.astype(BF16)
        h_parts = []
        for n in range(LRU_BLOCKS):
            ns = slice(n * LRU_BW, (n + 1) * LRU_BW)
            pre = _dot(xcb[:, ns], wax_ref[n])
            a, bterm = _lru_coeffs(pre[:, :LRU_BW] + ba_ref[:, ns], pre[:, LRU_BW:] + bx_ref[:, ns],
                                   xc[:, ns], lam_ref[:, ns])
            h_parts.append(a * hlin_ref[:, ns] + bterm)
        h_new = jnp.concatenate(h_parts, axis=1)
        hl_ref[...] = h_new
        o_b = (h_new * _gelu_tanh(proj(5))).astype(BF16)
        ga_s[...] = _sigmoid(proj(6))
        mb_s[...] = _sigmoid(proj(7)) * _dot(o_b, wbup_ref[...])

    b0 = pl.multiple_of(j * blk, blk)
    shifts = [lax.rem(nseq - b0 + p * blk, nseq) for p in range(3)]

    lane = lax.broadcasted_iota(jnp.int32, (DK, nseq), 1)
    sel_r = lax.broadcasted_iota(jnp.int32, (nseq, blk * DV), 0)
    sel_c = lax.broadcasted_iota(jnp.int32, (nseq, blk * DV), 1)
    selector = jnp.where((sel_r < 3 * blk) & (jnp.bitwise_and(sel_r, blk - 1) == sel_c // DV),
                         1.0, 0.0).astype(BF16)

    def spread(cols_t):
        p1 = cols_t.astype(BF16).astype(F32)
        r1 = cols_t - p1
        p2 = r1.astype(BF16).astype(F32)
        p3 = r1 - p2
        lhs = jnp.where(lane < blk, pltpu.roll(p1, shifts[0], 1),
                        jnp.where(lane < 2 * blk, pltpu.roll(p2, shifts[1], 1),
                                  jnp.where(lane < 3 * blk, pltpu.roll(p3, shifts[2], 1), 0.0)))
        return _dot(lhs.astype(BF16), selector)

    for h in range(HEADS):
        hrows = slice(h * DK, (h + 1) * DK)
        q_all = spread(qt_s[hrows, :])
        f_all = spread(ft_s[hrows, :])
        v_rows = v_s[pl.ds(b0, blk), hrows]
        o_rows = []
        for i in range(blk):
            f_col = f_all[:, i * DV:(i + 1) * DV]
            s_new = f_col * sin_ref[i, h] + (1.0 - f_col) * v_rows[i:i + 1, :]
            sout_ref[i, h] = s_new
            o_rows.append(jnp.sum(q_all[:, i * DV:(i + 1) * DV] * s_new, axis=0, keepdims=True))
        o = jnp.concatenate(o_rows, axis=0)
        oa_s[pl.ds(b0, blk), hrows] = o * lax.rsqrt(
            jnp.mean(o * o, axis=-1, keepdims=True) + EPS)

    @pl.when(j == pl.num_programs(0) - 1)
    def _():
        o_a = (oa_s[...] * gsil_s[...]).astype(BF16)
        m = ga_s[...] * _dot(o_a, waup_ref[...]) + mb_s[...]
        xo_ref[...] = x_ref[...] + _dot(m.astype(BF16), wout_ref[...])


def _mix_sample(x2d, s_in, h_in, c_in, p):
    nseq = x2d.shape[0]
    blk = SAMPLE_STATE_BLOCK
    assert nseq == LANES and nseq % blk == 0
    row_vec = _resident((1, D_MODEL))
    tok = _resident((nseq, D_MODEL))
    conv_state = _resident((nseq, CONV_W - 1, D_MODEL))
    out_shape = (
        jax.ShapeDtypeStruct((nseq, D_MODEL), F32),
        jax.ShapeDtypeStruct((nseq, HEADS, DK, DV), F32),
        jax.ShapeDtypeStruct((nseq, D_MODEL), F32),
        jax.ShapeDtypeStruct((nseq, CONV_W - 1, D_MODEL), F32),
    )
    state_spec = pl.BlockSpec((blk, HEADS, DK, DV), lambda j: (j, 0, 0, 0))
    return pl.pallas_call(
        _mix_sample_kernel,
        grid=(nseq // blk,),
        in_specs=[
            tok, row_vec,
            _resident((D_MODEL, N_SPLITS * D_MODEL)),
            _resident(p["lb_raw"].shape),
            row_vec,
            _resident((CONV_W, D_MODEL)),
            row_vec,
            _resident((LRU_BLOCKS, LRU_BW, 2 * LRU_BW)),
            row_vec, row_vec, row_vec,
            _resident((D_MODEL, D_MODEL)),
            _resident((D_MODEL, D_MODEL)),
            _resident((D_MODEL, D_MODEL)),
            state_spec, tok, conv_state,
        ],
        out_specs=(
            pl.BlockSpec((nseq, D_MODEL), lambda j: (0, 0)),
            state_spec,
            pl.BlockSpec((nseq, D_MODEL), lambda j: (0, 0)),
            pl.BlockSpec((nseq, CONV_W - 1, D_MODEL), lambda j: (0, 0, 0)),
        ),
        out_shape=out_shape,
        scratch_shapes=[
            pltpu.VMEM((D_MODEL, nseq), F32),
            pltpu.VMEM((D_MODEL, nseq), F32),
            pltpu.VMEM((nseq, D_MODEL), F32),
            pltpu.VMEM((nseq, D_MODEL), F32),
            pltpu.VMEM((nseq, D_MODEL), F32),
            pltpu.VMEM((nseq, D_MODEL), F32),
            pltpu.VMEM((nseq, D_MODEL), F32),
        ],
        name="mix_sample",
        compiler_params=pltpu.CompilerParams(
            dimension_semantics=("arbitrary",), vmem_limit_bytes=VMEM_LIMIT_BYTES),
    )(x2d, p["mix_norm"], p["w_in"], p["lb_raw"], p["o_norm"], p["conv_w"], p["conv_b"],
      p["w_ax"], p["b_a"], p["b_x"], p["lam"], p["w_a_up"], p["w_b_up"], p["w_out"],
      s_in, h_in, c_in)


def kernel(x_prompt, x_sample, state_hgrn, state_lru, state_conv, ffn1_norm, ffn1_w_gate, ffn1_w_up, ffn1_w_down, mix_norm, w_in, hgrn_lower_bounds, hgrn_out_norm, conv_w, conv_b, lru_w_a, lru_b_a, lru_w_x, lru_b_x, lru_lambda, w_a_up, w_b_up, w_out, ffn2_norm, ffn2_w_gate, ffn2_w_up, ffn2_w_down, final_norm):
    depth = w_in.shape[0]
    assert depth == 1, "single-layer trunk"
    l = 0
    row = lambda v: v.reshape(1, -1).astype(F32)
    bf = lambda w: w.astype(BF16)
    p = {
        "mix_norm": row(mix_norm[l]),
        "lb_raw": hgrn_lower_bounds.astype(F32),
        "o_norm": row(hgrn_out_norm[l]),
        "conv_w": conv_w[l].astype(F32),
        "conv_b": row(conv_b[l]),
        "b_a": row(lru_b_a[l]),
        "b_x": row(lru_b_x[l]),
        "lam": row(lru_lambda[l]),
    }
    f1 = (row(ffn1_norm[l]), bf(ffn1_w_gate[l]), bf(ffn1_w_up[l]), bf(ffn1_w_down[l]))
    gfin = row(final_norm)

    bp, tp, _ = x_prompt.shape
    bs, ts, _ = x_sample.shape
    assert ts == 1
    flat = lambda w: w.reshape(-1, w.shape[-1])
    w_ax = jnp.concatenate([lru_w_a[l], lru_w_x[l]], axis=-1)
    xp, xs, p["w_in"], p["w_a_up"], p["w_b_up"], p["w_out"], w_ax = _ffn(
        x_prompt.reshape(bp * tp, D_MODEL), x_sample.reshape(bs, D_MODEL), *f1, gfin,
        final_norm=False, cast=(flat(w_in), flat(w_a_up), flat(w_b_up), flat(w_out), flat(w_ax)))
    p["w_ax"] = w_ax.reshape(LRU_BLOCKS, LRU_BW, 2 * LRU_BW)
    xp, s_p, h_p, c_p, wg2, wu2, wd2 = _mix_prompt(
        xp.reshape(bp, tp, D_MODEL), p,
        cast=(flat(ffn2_w_gate), flat(ffn2_w_up), flat(ffn2_w_down)))
    f2 = (row(ffn2_norm[l]), wg2, wu2, wd2)
    xs, s_s, h_s, c_s = _mix_sample(xs, state_hgrn[l], state_lru[l], state_conv[l], p)
    y_p, y_s = _ffn(xp.reshape(bp * tp, D_MODEL), xs, *f2, gfin, final_norm=True)
    y_p = y_p.reshape(bp, tp, D_MODEL)
    y_s = y_s.reshape(bs, ts, D_MODEL)

    return (y_p, y_s, s_p[None], h_p.reshape(bp, D_MODEL)[None], c_p[None],
            s_s[None], h_s[None], c_s[None])
```

```python
import functools

import jax
import jax.numpy as jnp
from jax import lax
from jax.experimental import pallas as pl
from jax.experimental.pallas import tpu as pltpu

D_MODEL = 1024
HEADS = 8
DK = 128
DV = 128
CHUNK = 64
LRU_BLOCKS = 8
LRU_BW = 128
LRU_C = 8.0
CONV_W = 4
D_FF = 2816
EPS = 1e-6
LOG2_E = 1.4426950408889634
N_SPLITS = 8

SUBLANES = 8
LANES = 128
MXU_COLS = 256
VMEM_LIMIT_BYTES = 56 * 1024 * 1024

FFN_ROWS = 1024
MIX_ROWS = 256
SAMPLE_STATE_BLOCK = 8

BF16 = jnp.bfloat16
F32 = jnp.float32


def _dot(a, b):
    return jnp.dot(a, b, preferred_element_type=F32)


def _dot_tn(a, b):
    return lax.dot_general(a, b, (((0,), (0,)), ((), ())), preferred_element_type=F32)


def _dot_nt(a, b):
    return lax.dot_general(a, b, (((1,), (1,)), ((), ())), preferred_element_type=F32)


def _sigmoid(x):
    return 1.0 / (1.0 + jnp.exp2(x * (-LOG2_E)))


def _silu(x):
    return x * _sigmoid(x)


def _gelu_tanh(x):
    c = 0.7978845608028654
    return 0.5 * x * (1.0 + jnp.tanh(c * (x + 0.044715 * (x * x * x))))


def _rms(x, g):
    return x * lax.rsqrt(jnp.mean(x * x, axis=-1, keepdims=True) + EPS) * g


def _lower_bound(raw, layer):
    m = jnp.max(raw, axis=0, keepdims=True)
    e = jnp.exp(raw - m)
    den = jnp.sum(e, axis=0, keepdims=True)
    num = jnp.sum(e[0:layer + 1], axis=0, keepdims=True)
    return num / den


def _softplus(x):
    return jnp.maximum(x, 0.0) + jnp.log1p(jnp.exp(-jnp.abs(x)))


def _lru_coeffs(r_pre, i_pre, xc, lam):
    r = _sigmoid(r_pre)
    ig = _sigmoid(i_pre)
    log_a = (-LRU_C) * r * _softplus(-lam)
    a = jnp.exp(log_a)
    th = jnp.tanh(log_a)
    mult = jnp.sqrt(-2.0 * th / (1.0 - th))
    return a, mult * (ig * xc)


def _split3(x):
    p1 = x.astype(BF16)
    r1 = x - p1.astype(F32)
    p2 = r1.astype(BF16)
    r2 = r1 - p2.astype(F32)
    return p1, p2, r2.astype(BF16)


def _cast_rows(src_refs, dst_refs):
    for src, dst in zip(src_refs, dst_refs):
        dst[...] = src[...].astype(BF16)


def _cast_specs(weights, steps, step_index):
    specs, shapes = [], []
    for w in weights:
        assert w.ndim == 2
        count = max(c for c in range(1, steps + 1)
                    if w.shape[0] % c == 0 and (w.shape[0] // c) % (2 * SUBLANES) == 0)
        index = functools.partial(
            lambda *ids, last: (jnp.minimum(step_index(*ids), last), 0), last=count - 1)
        specs.append(pl.BlockSpec((w.shape[0] // count, w.shape[1]), index))
        shapes.append(jax.ShapeDtypeStruct(w.shape, BF16))
    return specs, shapes


def _ffn_kernel(xa_ref, xb_ref, g_ref, wg_ref, wu_ref, wd_ref, gf_ref, *rest, final_norm, a_tiles):
    n_cast = (len(rest) - 2) // 2
    oa_ref, ob_ref = rest[n_cast:n_cast + 2]
    _cast_rows(rest[:n_cast], rest[n_cast + 2:])

    def half_step(x_ref, o_ref):
        x = x_ref[...]
        inv_rms = lax.rsqrt(jnp.mean(x * x, axis=-1, keepdims=True) + EPS)
        h = (x * g_ref[...]).astype(BF16)
        gate = _dot(h, wg_ref[...]) * inv_rms
        up = _dot(h, wu_ref[...]) * inv_rms
        act = (_silu(gate) * up).astype(BF16)
        y = x + 0.5 * _dot(act, wd_ref[...])
        if final_norm:
            y = _rms(y, gf_ref[...])
        o_ref[...] = y

    i = pl.program_id(0)

    @pl.when(i < a_tiles)
    def _():
        half_step(xa_ref, oa_ref)

    @pl.when(i == a_tiles)
    def _():
        half_step(xb_ref, ob_ref)


def _resident(shape):
    nd = len(shape)
    return pl.BlockSpec(shape, lambda *_: (0,) * nd, pipeline_mode=pl.Buffered(1))


def _ffn(xa, xb, g, wg, wu, wd, gf, *, final_norm, cast=()):
    na, nb = xa.shape[0], xb.shape[0]
    assert na % FFN_ROWS == 0 and nb <= FFN_ROWS and nb % SUBLANES == 0
    a_tiles = na // FFN_ROWS
    a_tile = lambda i: (jnp.minimum(i, a_tiles - 1), 0)
    b_whole = lambda i: (0, 0)
    cast_specs, cast_shapes = _cast_specs(cast, a_tiles, lambda i: i)
    return pl.pallas_call(
        functools.partial(_ffn_kernel, final_norm=final_norm, a_tiles=a_tiles),
        grid=(a_tiles + 1,),
        in_specs=[
            pl.BlockSpec((FFN_ROWS, D_MODEL), a_tile),
            pl.BlockSpec((nb, D_MODEL), b_whole),
            _resident((1, D_MODEL)),
            _resident((D_MODEL, D_FF)),
            _resident((D_MODEL, D_FF)),
            _resident((D_FF, D_MODEL)),
            _resident((1, D_MODEL)),
            *cast_specs,
        ],
        out_specs=(pl.BlockSpec((FFN_ROWS, D_MODEL), a_tile),
                   pl.BlockSpec((nb, D_MODEL), b_whole),
                   *cast_specs),
        out_shape=(jax.ShapeDtypeStruct((na, D_MODEL), F32),
                   jax.ShapeDtypeStruct((nb, D_MODEL), F32),
                   *cast_shapes),
        name="ffn_final" if final_norm else "ffn",
        compiler_params=pltpu.CompilerParams(
            dimension_semantics=("arbitrary",), vmem_limit_bytes=VMEM_LIMIT_BYTES),
    )(xa, xb, g, wg, wu, wd, gf, *cast)


SCAN_SEG = 4
SCAN_GROUP = SCAN_SEG * SUBLANES
CONV_PAD = SUBLANES
PIECE_ROWS = 2 * SUBLANES
QA_W = 2 * LANES


MIX_INPUTS = 14
MIX_OUTPUTS = 4
MIX_SCRATCH = 6


def _mix_prompt_kernel(*refs):
    n_cast = (len(refs) - MIX_INPUTS - MIX_OUTPUTS - MIX_SCRATCH) // 2
    (x_ref, nmix_ref, win_ref, lbraw_ref, onorm_ref, convw_ref, convb_ref,
     wax_ref, ba_ref, bx_ref, lam_ref, waup_ref, wbup_ref, wout_ref) = refs[:MIX_INPUTS]
    outs = refs[MIX_INPUTS + n_cast:]
    xo_ref, s_ref, hl_ref, cv_ref = outs[:MIX_OUTPUTS]
    qa_s, ud_s, vb_s, oa_s, xr_s, h_s = outs[MIX_OUTPUTS + n_cast:]
    _cast_rows(refs[MIX_INPUTS:MIX_INPUTS + n_cast], outs[MIX_OUTPUTS:MIX_OUTPUTS + n_cast])

    rows = MIX_ROWS
    t = pl.program_id(1)

    @pl.when(t == 0)
    def _():
        s_ref[...] = jnp.zeros_like(s_ref)
        hl_ref[...] = jnp.zeros_like(hl_ref)
        xr_s[:, 0:CONV_PAD, :] = jnp.zeros((LRU_BLOCKS, CONV_PAD, LRU_BW), F32)

    x = x_ref[0]
    hb = _rms(x, nmix_ref[...]).astype(BF16)

    def proj(i):
        return _dot(hb, win_ref[:, i * D_MODEL:(i + 1) * D_MODEL])

    early = {}

    ncb = D_MODEL // MXU_COLS

    def issue_proj(i, cb):
        if cb < ncb and (i, cb) not in early:
            c0 = i * D_MODEL + cb * MXU_COLS
            early[(i, cb)] = _dot(hb, win_ref[:, c0:c0 + MXU_COLS])

    def gather(i):
        for cb in range(ncb):
            issue_proj(i, cb)
        return jnp.concatenate([early[(i, cb)] for cb in range(ncb)], axis=1)

    lb = _lower_bound(lbraw_ref[...], 0)
    q_act = _silu(proj(0))
    f = lb + (1.0 - lb) * _sigmoid(proj(1))
    kk = 1.0 - f
    vb_s[...] = proj(2).astype(BF16)

    ri = lax.broadcasted_iota(jnp.int32, (rows, rows), 0)
    ci = lax.broadcasted_iota(jnp.int32, (rows, rows), 1)
    same_chunk = (ri // CHUNK) == (ci // CHUNK)
    tril_blk = jnp.where((ri >= ci) & same_chunk, 1.0, 0.0).astype(BF16)
    logf = jnp.log(f)
    g1 = logf.astype(BF16)
    g2 = (logf - g1.astype(F32)).astype(BF16)
    b_all = _dot(tril_blk, g1) + _dot(tril_blk, g2)

    causal = (lax.broadcasted_iota(jnp.int32, (CHUNK, CHUNK), 0)
              >= lax.broadcasted_iota(jnp.int32, (CHUNK, CHUNK), 1))
    piece_row = lax.broadcasted_iota(jnp.int32, (PIECE_ROWS, D_MODEL), 0)
    ones_pv = jnp.ones((3 * PIECE_ROWS, DV), BF16)
    zeros_pv = jnp.zeros((3 * PIECE_ROWS, DV), BF16)
    zeros_cv = jnp.zeros((CHUNK, DV), BF16)

    nchunk = rows // CHUNK
    last = piece_row == PIECE_ROWS - 1
    zero_p = jnp.zeros((PIECE_ROWS, D_MODEL), BF16)

    for c in range(nchunk):
        issue_proj(3, c)
        rs = slice(c * CHUNK, (c + 1) * CHUNK)
        b = b_all[rs, :]
        bl = b[CHUNK - 1:CHUNK, :]
        qe = (q_act[rs, :] * jnp.exp(b)).astype(BF16)
        ke = (kk[rs, :] * jnp.exp(-b)).astype(BF16)
        kd = (kk[rs, :] * jnp.exp(bl - b)).astype(BF16)
        pieces = jnp.concatenate(
            [jnp.where(last, e, zero_p) for e in _split3(jnp.exp(b[CHUNK - PIECE_ROWS:, :]))], axis=0)
        for h in range(HEADS):
            hs = slice(h * DK, (h + 1) * DK)
            qa_s[rs, h * QA_W:h * QA_W + DK] = qe[:, hs]
            att = jnp.where(causal, _dot_nt(qe[:, hs], ke[:, hs]), 0.0)
            qa_s[rs, h * QA_W + DK:h * QA_W + DK + CHUNK] = att.astype(BF16)
            lhs = jnp.concatenate([kd[:, hs], pieces[:, hs]], axis=0)
            rhs = jnp.concatenate([jnp.concatenate([vb_s[rs, hs], zeros_cv], axis=1),
                                   jnp.concatenate([zeros_pv, ones_pv], axis=1)], axis=0)
            ud_s[c * HEADS + h] = _dot_tn(lhs, rhs)

    for c in range(nchunk):
        rs = slice(c * CHUNK, (c + 1) * CHUNK)
        for h in range(HEADS):
            hs = slice(h * DK, (h + 1) * DK)
            s_old = s_ref[0, h]
            rhs = jnp.concatenate([s_old.astype(BF16), vb_s[rs, hs]], axis=0)
            o = _dot(qa_s[rs, h * QA_W:h * QA_W + DK + CHUNK], rhs)
            oa_s[rs, hs] = o * lax.rsqrt(jnp.mean(o * o, axis=-1, keepdims=True) + EPS)
            ud = ud_s[c * HEADS + h]
            s_ref[0, h] = ud[:, DV:] * s_old + ud[:, 0:DV]

    o_a = (oa_s[...] * onorm_ref[...] * _silu(gather(3))).astype(BF16)
    up_a = {}

    def issue_up_a(cb):
        if cb < ncb and cb not in up_a:
            up_a[cb] = _dot(o_a, waup_ref[:, cb * MXU_COLS:(cb + 1) * MXU_COLS])

    xr = proj(4)
    h_prev = hl_ref[0]
    sub = lax.broadcasted_iota(jnp.int32, (SUBLANES, LANES), 0)
    ngroup = rows // SCAN_GROUP
    tails = []
    h_last = []
    for n in range(LRU_BLOCKS):
        ns = slice(n * LRU_BW, (n + 1) * LRU_BW)
        xr_s[n, CONV_PAD:CONV_PAD + rows, :] = xr[:, ns]
        taps = [jnp.broadcast_to(convw_ref[j:j + 1, ns], (SUBLANES, LRU_BW)) for j in range(CONV_W)]
        bias = jnp.broadcast_to(convb_ref[:, ns], (SUBLANES, LRU_BW))
        xc_parts = []
        for grp in range(ngroup):
            xin = {d: xr_s[n, pl.ds(CONV_PAD + grp * SCAN_GROUP + d, SUBLANES, stride=SCAN_SEG), :]
                   for d in range(1 - CONV_W, SCAN_SEG)}
            for j in range(SCAN_SEG):
                acc = bias
                for tap in range(CONV_W):
                    acc = acc + xin[j + tap - (CONV_W - 1)] * taps[tap]
                xc_parts.append(acc)
        xc = jnp.concatenate(xc_parts, axis=0)
        tail = xr_s[n, CONV_PAD + rows - (CONV_W - 1):CONV_PAD + rows, :]
        tails.append(tail)
        xr_s[n, CONV_PAD - (CONV_W - 1):CONV_PAD, :] = tail
        pre = _dot(xc.astype(BF16), wax_ref[n])
        issue_proj(5 + n % 2, n // 2)
        a, bterm = _lru_coeffs(pre[:, :LRU_BW] + ba_ref[:, ns], pre[:, LRU_BW:] + bx_ref[:, ns],
                               xc, lam_ref[:, ns])
        carry = h_prev[:, ns]
        for grp in range(ngroup):
            def vreg(v, j, grp=grp):
                r0 = grp * SCAN_GROUP + j * SUBLANES
                return v[r0:r0 + SUBLANES, :]

            def strided(j, grp=grp):
                return pl.ds(grp * SCAN_GROUP + j, SUBLANES, stride=SCAN_SEG)
            pa = [vreg(a, 0)]
            hh = [vreg(bterm, 0)]
            for j in range(1, SCAN_SEG):
                aj = vreg(a, j)
                hh.append(aj * hh[-1] + vreg(bterm, j))
                pa.append(aj * pa[-1])
            pi, hi = pa[-1], hh[-1]
            for d in (1, 2, 4):
                keep = sub >= d
                hi = pi * jnp.where(keep, pltpu.roll(hi, d, 0), 0.0) + hi
                pi = pi * jnp.where(keep, pltpu.roll(pi, d, 0), 1.0)
            first = sub >= 1
            seg_in = (jnp.where(first, pltpu.roll(pi, 1, 0), 1.0) * carry
                      + jnp.where(first, pltpu.roll(hi, 1, 0), 0.0))
            carry = pi[SUBLANES - 1:, :] * carry + hi[SUBLANES - 1:, :]
            for j in range(SCAN_SEG):
                h_s[n, strided(j), :] = pa[j] * seg_in + hh[j]
        h_last.append(carry)
        if n % 2 == 0:
            issue_proj(7, n // 2)
        else:
            issue_up_a(n // 2)
    hl_ref[0] = jnp.concatenate(h_last, axis=1)
    cv_ref[0] = jnp.concatenate(tails, axis=1)
    gel = _gelu_tanh(gather(5))
    o_b = jnp.concatenate(
        [h_s[n] * gel[:, n * LRU_BW:(n + 1) * LRU_BW] for n in range(LRU_BLOCKS)], axis=1).astype(BF16)

    for cb in range(ncb):
        issue_up_a(cb)
    m = (_sigmoid(gather(6)) * jnp.concatenate([up_a[cb] for cb in range(ncb)], axis=1)
         + _sigmoid(gather(7)) * _dot(o_b, wbup_ref[...]))
    xo_ref[0] = x + _dot(m.astype(BF16), wout_ref[...])


def _mix_prompt(x, p, cast=()):
    batch, seq, _ = x.shape
    rows = MIX_ROWS
    assert seq % rows == 0 and rows % CHUNK == 0 and rows >= CONV_W - 1
    tiles = seq // rows
    row_vec = _resident((1, D_MODEL))
    cast_specs, cast_shapes = _cast_specs(cast, batch * tiles, lambda b, t: b * tiles + t)
    out_shape = (
        jax.ShapeDtypeStruct((batch, seq, D_MODEL), F32),
        jax.ShapeDtypeStruct((batch, HEADS, DK, DV), F32),
        jax.ShapeDtypeStruct((batch, 1, D_MODEL), F32),
        jax.ShapeDtypeStruct((batch, CONV_W - 1, D_MODEL), F32),
        *cast_shapes,
    )
    return pl.pallas_call(
        _mix_prompt_kernel,
        grid=(batch, seq // rows),
        in_specs=[
            pl.BlockSpec((1, rows, D_MODEL), lambda b, t: (b, t, 0)),
            row_vec,
            _resident((D_MODEL, N_SPLITS * D_MODEL)),
            _resident(p["lb_raw"].shape),
            row_vec,
            _resident((CONV_W, D_MODEL)),
            row_vec,
            _resident((LRU_BLOCKS, LRU_BW, 2 * LRU_BW)),
            row_vec, row_vec, row_vec,
            _resident((D_MODEL, D_MODEL)),
            _resident((D_MODEL, D_MODEL)),
            _resident((D_MODEL, D_MODEL)),
            *cast_specs,
        ],
        out_specs=(
            pl.BlockSpec((1, rows, D_MODEL), lambda b, t: (b, t, 0)),
            pl.BlockSpec((1, HEADS, DK, DV), lambda b, t: (b, 0, 0, 0)),
            pl.BlockSpec((1, 1, D_MODEL), lambda b, t: (b, 0, 0)),
            pl.BlockSpec((1, CONV_W - 1, D_MODEL), lambda b, t: (b, 0, 0)),
            *cast_specs,
        ),
        out_shape=out_shape,
        scratch_shapes=[
            pltpu.VMEM((rows, HEADS * QA_W), BF16),
            pltpu.VMEM((rows // CHUNK * HEADS, DK, 2 * DV), F32),
            pltpu.VMEM((rows, D_MODEL), BF16),
            pltpu.VMEM((rows, D_MODEL), F32),
            pltpu.VMEM((LRU_BLOCKS, CONV_PAD + rows, LRU_BW), F32),
            pltpu.VMEM((LRU_BLOCKS, rows, LRU_BW), F32),
        ],
        name="mix_prompt",
        compiler_params=pltpu.CompilerParams(
            dimension_semantics=("arbitrary", "arbitrary"), vmem_limit_bytes=VMEM_LIMIT_BYTES),
    )(x, p["mix_norm"], p["w_in"], p["lb_raw"], p["o_norm"], p["conv_w"], p["conv_b"],
      p["w_ax"], p["b_a"], p["b_x"], p["lam"], p["w_a_up"], p["w_b_up"], p["w_out"], *cast)


def _mix_sample_kernel(x_ref, nmix_ref, win_ref, lbraw_ref, onorm_ref, convw_ref, convb_ref,
                       wax_ref, ba_ref, bx_ref, lam_ref, waup_ref, wbup_ref, wout_ref,
                       sin_ref, hlin_ref, cvin_ref,
                       xo_ref, sout_ref, hl_ref, cv_ref,
                       qt_s, ft_s, v_s, oa_s, gsil_s, ga_s, mb_s):
    nseq = x_ref.shape[0]
    blk = SAMPLE_STATE_BLOCK
    j = pl.program_id(0)

    @pl.when(j == 0)
    def _():
        x = x_ref[...]
        hb = _rms(x, nmix_ref[...]).astype(BF16)

        def proj(i):
            return _dot(hb, win_ref[:, i * D_MODEL:(i + 1) * D_MODEL])

        lb = _lower_bound(lbraw_ref[...], 0)
        qt_s[...] = _silu(proj(0)).T
        ft_s[...] = (lb + (1.0 - lb) * _sigmoid(proj(1))).T
        v_s[...] = proj(2)
        gsil_s[...] = onorm_ref[...] * _silu(proj(3))

        xr = proj(4)
        xc = convb_ref[...] + xr * convw_ref[CONV_W - 1:CONV_W, :]
        for i in range(CONV_W - 1):
            xc = xc + cvin_ref[:, i, :] * convw_ref[i:i + 1, :]
        for i in range(CONV_W - 2):
            cv_ref[:, i, :] = cvin_ref[:, i + 1, :]
        cv_ref[:, CONV_W - 2, :] = xr

        xcb = xc.astype(BF16)
        h_parts = []
        for n in range(LRU_BLOCKS):
            ns = slice(n * LRU_BW, (n + 1) * LRU_BW)
            pre = _dot(xcb[:, ns], wax_ref[n])
            a, bterm = _lru_coeffs(pre[:, :LRU_BW] + ba_ref[:, ns], pre[:, LRU_BW:] + bx_ref[:, ns],
                                   xc[:, ns], lam_ref[:, ns])
            h_parts.append(a * hlin_ref[:, ns] + bterm)
        h_new = jnp.concatenate(h_parts, axis=1)
        hl_ref[...] = h_new
        o_b = (h_new * _gelu_tanh(proj(5))).astype(BF16)
        ga_s[...] = _sigmoid(proj(6))
        mb_s[...] = _sigmoid(proj(7)) * _dot(o_b, wbup_ref[...])

    b0 = pl.multiple_of(j * blk, blk)
    shifts = [lax.rem(nseq - b0 + p * blk, nseq) for p in range(3)]

    lane = lax.broadcasted_iota(jnp.int32, (DK, nseq), 1)
    sel_r = lax.broadcasted_iota(jnp.int32, (nseq, blk * DV), 0)
    sel_c = lax.broadcasted_iota(jnp.int32, (nseq, blk * DV), 1)
    selector = jnp.where((sel_r < 3 * blk) & (jnp.bitwise_and(sel_r, blk - 1) == sel_c // DV),
                         1.0, 0.0).astype(BF16)

    def spread(cols_t):
        p1 = cols_t.astype(BF16).astype(F32)
        r1 = cols_t - p1
        p2 = r1.astype(BF16).astype(F32)
        p3 = r1 - p2
        lhs = jnp.where(lane < blk, pltpu.roll(p1, shifts[0], 1),
                        jnp.where(lane < 2 * blk, pltpu.roll(p2, shifts[1], 1),
                                  jnp.where(lane < 3 * blk, pltpu.roll(p3, shifts[2], 1), 0.0)))
        return _dot(lhs.astype(BF16), selector)

    for h in range(HEADS):
        hrows = slice(h * DK, (h + 1) * DK)
        q_all = spread(qt_s[hrows, :])
        f_all = spread(ft_s[hrows, :])
        v_rows = v_s[pl.ds(b0, blk), hrows]
        o_rows = []
        for i in range(blk):
            f_col = f_all[:, i * DV:(i + 1) * DV]
            s_new = f_col * sin_ref[i, h] + (1.0 - f_col) * v_rows[i:i + 1, :]
            sout_ref[i, h] = s_new
            o_rows.append(jnp.sum(q_all[:, i * DV:(i + 1) * DV] * s_new, axis=0, keepdims=True))
        o = jnp.concatenate(o_rows, axis=0)
        oa_s[pl.ds(b0, blk), hrows] = o * lax.rsqrt(
            jnp.mean(o * o, axis=-1, keepdims=True) + EPS)

    @pl.when(j == pl.num_programs(0) - 1)
    def _():
        o_a = (oa_s[...] * gsil_s[...]).astype(BF16)
        m = ga_s[...] * _dot(o_a, waup_ref[...]) + mb_s[...]
        xo_ref[...] = x_ref[...] + _dot(m.astype(BF16), wout_ref[...])


def _mix_sample(x2d, s_in, h_in, c_in, p):
    nseq = x2d.shape[0]
    blk = SAMPLE_STATE_BLOCK
    assert nseq == LANES and nseq % blk == 0
    row_vec = _resident((1, D_MODEL))
    tok = _resident((nseq, D_MODEL))
    conv_state = _resident((nseq, CONV_W - 1, D_MODEL))
    out_shape = (
        jax.ShapeDtypeStruct((nseq, D_MODEL), F32),
        jax.ShapeDtypeStruct((nseq, HEADS, DK, DV), F32),
        jax.ShapeDtypeStruct((nseq, D_MODEL), F32),
        jax.ShapeDtypeStruct((nseq, CONV_W - 1, D_MODEL), F32),
    )
    state_spec = pl.BlockSpec((blk, HEADS, DK, DV), lambda j: (j, 0, 0, 0))
    return pl.pallas_call(
        _mix_sample_kernel,
        grid=(nseq // blk,),
        in_specs=[
            tok, row_vec,
            _resident((D_MODEL, N_SPLITS * D_MODEL)),
            _resident(p["lb_raw"].shape),
            row_vec,
            _resident((CONV_W, D_MODEL)),
            row_vec,
            _resident((LRU_BLOCKS, LRU_BW, 2 * LRU_BW)),
            row_vec, row_vec, row_vec,
            _resident((D_MODEL, D_MODEL)),
            _resident((D_MODEL, D_MODEL)),
            _resident((D_MODEL, D_MODEL)),
            state_spec, tok, conv_state,
        ],
        out_specs=(
            pl.BlockSpec((nseq, D_MODEL), lambda j: (0, 0)),
            state_spec,
            pl.BlockSpec((nseq, D_MODEL), lambda j: (0, 0)),
            pl.BlockSpec((nseq, CONV_W - 1, D_MODEL), lambda j: (0, 0, 0)),
        ),
        out_shape=out_shape,
        scratch_shapes=[
            pltpu.VMEM((D_MODEL, nseq), F32),
            pltpu.VMEM((D_MODEL, nseq), F32),
            pltpu.VMEM((nseq, D_MODEL), F32),
            pltpu.VMEM((nseq, D_MODEL), F32),
            pltpu.VMEM((nseq, D_MODEL), F32),
            pltpu.VMEM((nseq, D_MODEL), F32),
            pltpu.VMEM((nseq, D_MODEL), F32),
        ],
        name="mix_sample",
        compiler_params=pltpu.CompilerParams(
            dimension_semantics=("arbitrary",), vmem_limit_bytes=VMEM_LIMIT_BYTES),
    )(x2d, p["mix_norm"], p["w_in"], p["lb_raw"], p["o_norm"], p["conv_w"], p["conv_b"],
      p["w_ax"], p["b_a"], p["b_x"], p["lam"], p["w_a_up"], p["w_b_up"], p["w_out"],
      s_in, h_in, c_in)


def kernel(x_prompt, x_sample, state_hgrn, state_lru, state_conv, ffn1_norm, ffn1_w_gate, ffn1_w_up, ffn1_w_down, mix_norm, w_in, hgrn_lower_bounds, hgrn_out_norm, conv_w, conv_b, lru_w_a, lru_b_a, lru_w_x, lru_b_x, lru_lambda, w_a_up, w_b_up, w_out, ffn2_norm, ffn2_w_gate, ffn2_w_up, ffn2_w_down, final_norm):
    depth = w_in.shape[0]
    assert depth == 1, "single-layer trunk"
    l = 0
    row = lambda v: v.reshape(1, -1).astype(F32)
    bf = lambda w: w.astype(BF16)
    p = {
        "mix_norm": row(mix_norm[l]),
        "lb_raw": hgrn_lower_bounds.astype(F32),
        "o_norm": row(hgrn_out_norm[l]),
        "conv_w": conv_w[l].astype(F32),
        "conv_b": row(conv_b[l]),
        "b_a": row(lru_b_a[l]),
        "b_x": row(lru_b_x[l]),
        "lam": row(lru_lambda[l]),
    }
    f1 = (row(ffn1_norm[l]), bf(ffn1_w_gate[l]), bf(ffn1_w_up[l]), bf(ffn1_w_down[l]))
    gfin = row(final_norm)

    bp, tp, _ = x_prompt.shape
    bs, ts, _ = x_sample.shape
    assert ts == 1
    flat = lambda w: w.reshape(-1, w.shape[-1])
    w_ax = jnp.concatenate([lru_w_a[l], lru_w_x[l]], axis=-1)
    xp, xs, p["w_in"], p["w_a_up"], p["w_b_up"], p["w_out"], w_ax = _ffn(
        x_prompt.reshape(bp * tp, D_MODEL), x_sample.reshape(bs, D_MODEL), *f1, gfin,
        final_norm=False, cast=(flat(w_in), flat(w_a_up), flat(w_b_up), flat(w_out), flat(w_ax)))
    p["w_ax"] = w_ax.reshape(LRU_BLOCKS, LRU_BW, 2 * LRU_BW)
    xp, s_p, h_p, c_p, wg2, wu2, wd2 = _mix_prompt(
        xp.reshape(bp, tp, D_MODEL), p,
        cast=(flat(ffn2_w_gate), flat(ffn2_w_up), flat(ffn2_w_down)))
    f2 = (row(ffn2_norm[l]), wg2, wu2, wd2)
    xs, s_s, h_s, c_s = _mix_sample(xs, state_hgrn[l], state_lru[l], state_conv[l], p)
    y_p, y_s = _ffn(xp.reshape(bp * tp, D_MODEL), xs, *f2, gfin, final_norm=True)
    y_p = y_p.reshape(bp, tp, D_MODEL)
    y_s = y_s.reshape(bs, ts, D_MODEL)

    return (y_p, y_s, s_p[None], h_p.reshape(bp, D_MODEL)[None], c_p[None],
            s_s[None], h_s[None], c_s[None])
```

```python
import functools

import jax
import jax.numpy as jnp
from jax import lax
from jax.experimental import pallas as pl
from jax.experimental.pallas import tpu as pltpu

D_MODEL = 1024
HEADS = 8
DK = 128
DV = 128
CHUNK = 64
LRU_BLOCKS = 8
LRU_BW = 128
LRU_C = 8.0
CONV_W = 4
D_FF = 2816
EPS = 1e-6
LOG2_E = 1.4426950408889634
N_SPLITS = 8

SUBLANES = 8
LANES = 128
MXU_COLS = 256
VMEM_LIMIT_BYTES = 56 * 1024 * 1024

FFN_ROWS = 1024
MIX_ROWS = 256
SAMPLE_STATE_BLOCK = 8

BF16 = jnp.bfloat16
F32 = jnp.float32


def _dot(a, b):
    return jnp.dot(a, b, preferred_element_type=F32)


def _dot_tn(a, b):
    return lax.dot_general(a, b, (((0,), (0,)), ((), ())), preferred_element_type=F32)


def _dot_nt(a, b):
    return lax.dot_general(a, b, (((1,), (1,)), ((), ())), preferred_element_type=F32)


def _sigmoid(x):
    return 1.0 / (1.0 + jnp.exp2(x * (-LOG2_E)))


def _silu(x):
    return x * _sigmoid(x)


def _gelu_tanh(x):
    c = 0.7978845608028654
    return 0.5 * x * (1.0 + jnp.tanh(c * (x + 0.044715 * (x * x * x))))


def _rms(x, g):
    return x * lax.rsqrt(jnp.mean(x * x, axis=-1, keepdims=True) + EPS) * g


def _lower_bound(raw, layer):
    m = jnp.max(raw, axis=0, keepdims=True)
    e = jnp.exp(raw - m)
    den = jnp.sum(e, axis=0, keepdims=True)
    num = jnp.sum(e[0:layer + 1], axis=0, keepdims=True)
    return num / den


def _softplus(x):
    return jnp.maximum(x, 0.0) + jnp.log1p(jnp.exp(-jnp.abs(x)))


def _lru_coeffs(r_pre, i_pre, xc, lam):
    r = _sigmoid(r_pre)
    ig = _sigmoid(i_pre)
    log_a = (-LRU_C) * r * _softplus(-lam)
    a = jnp.exp(log_a)
    th = jnp.tanh(log_a)
    mult = jnp.sqrt(-2.0 * th / (1.0 - th))
    return a, mult * (ig * xc)


def _split3(x):
    p1 = x.astype(BF16)
    r1 = x - p1.astype(F32)
    p2 = r1.astype(BF16)
    r2 = r1 - p2.astype(F32)
    return p1, p2, r2.astype(BF16)


def _cast_rows(src_refs, dst_refs):
    for src, dst in zip(src_refs, dst_refs):
        dst[...] = src[...].astype(BF16)


def _cast_specs(weights, steps, step_index):
    specs, shapes = [], []
    for w in weights:
        assert w.ndim == 2
        count = max(c for c in range(1, steps + 1)
                    if w.shape[0] % c == 0 and (w.shape[0] // c) % (2 * SUBLANES) == 0)
        index = functools.partial(
            lambda *ids, last: (jnp.minimum(step_index(*ids), last), 0), last=count - 1)
        specs.append(pl.BlockSpec((w.shape[0] // count, w.shape[1]), index))
        shapes.append(jax.ShapeDtypeStruct(w.shape, BF16))
    return specs, shapes


def _ffn_kernel(xa_ref, xb_ref, g_ref, wg_ref, wu_ref, wd_ref, gf_ref, *rest, final_norm, a_tiles):
    n_cast = (len(rest) - 2) // 2
    oa_ref, ob_ref = rest[n_cast:n_cast + 2]
    _cast_rows(rest[:n_cast], rest[n_cast + 2:])

    def half_step(x_ref, o_ref):
        x = x_ref[...]
        inv_rms = lax.rsqrt(jnp.mean(x * x, axis=-1, keepdims=True) + EPS)
        h = (x * g_ref[...]).astype(BF16)
        gate = _dot(h, wg_ref[...]) * inv_rms
        up = _dot(h, wu_ref[...]) * inv_rms
        act = (_silu(gate) * up).astype(BF16)
        y = x + 0.5 * _dot(act, wd_ref[...])
        if final_norm:
            y = _rms(y, gf_ref[...])
        o_ref[...] = y

    i = pl.program_id(0)

    @pl.when(i < a_tiles)
    def _():
        half_step(xa_ref, oa_ref)

    @pl.when(i == a_tiles)
    def _():
        half_step(xb_ref, ob_ref)


def _resident(shape):
    nd = len(shape)
    return pl.BlockSpec(shape, lambda *_: (0,) * nd, pipeline_mode=pl.Buffered(1))


def _ffn(xa, xb, g, wg, wu, wd, gf, *, final_norm, cast=()):
    na, nb = xa.shape[0], xb.shape[0]
    assert na % FFN_ROWS == 0 and nb <= FFN_ROWS and nb % SUBLANES == 0
    a_tiles = na // FFN_ROWS
    a_tile = lambda i: (jnp.minimum(i, a_tiles - 1), 0)
    b_whole = lambda i: (0, 0)
    cast_specs, cast_shapes = _cast_specs(cast, a_tiles, lambda i: i)
    return pl.pallas_call(
        functools.partial(_ffn_kernel, final_norm=final_norm, a_tiles=a_tiles),
        grid=(a_tiles + 1,),
        in_specs=[
            pl.BlockSpec((FFN_ROWS, D_MODEL), a_tile),
            pl.BlockSpec((nb, D_MODEL), b_whole),
            _resident((1, D_MODEL)),
            _resident((D_MODEL, D_FF)),
            _resident((D_MODEL, D_FF)),
            _resident((D_FF, D_MODEL)),
            _resident((1, D_MODEL)),
            *cast_specs,
        ],
        out_specs=(pl.BlockSpec((FFN_ROWS, D_MODEL), a_tile),
                   pl.BlockSpec((nb, D_MODEL), b_whole),
                   *cast_specs),
        out_shape=(jax.ShapeDtypeStruct((na, D_MODEL), F32),
                   jax.ShapeDtypeStruct((nb, D_MODEL), F32),
                   *cast_shapes),
        name="ffn_final" if final_norm else "ffn",
        compiler_params=pltpu.CompilerParams(
            dimension_semantics=("arbitrary",), vmem_limit_bytes=VMEM_LIMIT_BYTES),
    )(xa, xb, g, wg, wu, wd, gf, *cast)


SCAN_SEG = 4
SCAN_GROUP = SCAN_SEG * SUBLANES
CONV_PAD = SUBLANES
PIECE_ROWS = 2 * SUBLANES
QA_W = 2 * LANES


MIX_INPUTS = 14
MIX_OUTPUTS = 4
MIX_SCRATCH = 7


def _mix_prompt_kernel(*refs):
    n_cast = (len(refs) - MIX_INPUTS - MIX_OUTPUTS - MIX_SCRATCH) // 2
    (x_ref, nmix_ref, win_ref, lbraw_ref, onorm_ref, convw_ref, convb_ref,
     wax_ref, ba_ref, bx_ref, lam_ref, waup_ref, wbup_ref, wout_ref) = refs[:MIX_INPUTS]
    outs = refs[MIX_INPUTS + n_cast:]
    xo_ref, s_ref, hl_ref, cv_ref = outs[:MIX_OUTPUTS]
    qa_s, ud_s, vb_s, oa_s, xr_s, h_s, win_s = outs[MIX_OUTPUTS + n_cast:]
    _cast_rows(refs[MIX_INPUTS:MIX_INPUTS + n_cast], outs[MIX_OUTPUTS:MIX_OUTPUTS + n_cast])

    rows = MIX_ROWS
    t = pl.program_id(1)

    @pl.when(t == 0)
    def _():
        s_ref[...] = jnp.zeros_like(s_ref)
        hl_ref[...] = jnp.zeros_like(hl_ref)
        xr_s[:, 0:CONV_PAD, :] = jnp.zeros((LRU_BLOCKS, CONV_PAD, LRU_BW), F32)

    @pl.when((pl.program_id(0) == 0) & (t == 0))
    def _():
        win_s[...] = win_ref[...]

    x = x_ref[0]
    hb = _rms(x, nmix_ref[...]).astype(BF16)

    def proj(i):
        return _dot(hb, win_s[:, i * D_MODEL:(i + 1) * D_MODEL])

    early = {}

    ncb = D_MODEL // MXU_COLS

    def issue_proj(i, cb):
        if cb < ncb and (i, cb) not in early:
            c0 = i * D_MODEL + cb * MXU_COLS
            early[(i, cb)] = _dot(hb, win_s[:, c0:c0 + MXU_COLS])

    def gather(i):
        for cb in range(ncb):
            issue_proj(i, cb)
        return jnp.concatenate([early[(i, cb)] for cb in range(ncb)], axis=1)

    lb = _lower_bound(lbraw_ref[...], 0)
    q_act = _silu(proj(0))
    f = lb + (1.0 - lb) * _sigmoid(proj(1))
    kk = 1.0 - f
    vb_s[...] = proj(2).astype(BF16)

    ri = lax.broadcasted_iota(jnp.int32, (rows, rows), 0)
    ci = lax.broadcasted_iota(jnp.int32, (rows, rows), 1)
    same_chunk = (ri // CHUNK) == (ci // CHUNK)
    tril_blk = jnp.where((ri >= ci) & same_chunk, 1.0, 0.0).astype(BF16)
    logf = jnp.log(f)
    g1 = logf.astype(BF16)
    g2 = (logf - g1.astype(F32)).astype(BF16)
    b_all = _dot(tril_blk, g1) + _dot(tril_blk, g2)

    causal = (lax.broadcasted_iota(jnp.int32, (CHUNK, CHUNK), 0)
              >= lax.broadcasted_iota(jnp.int32, (CHUNK, CHUNK), 1))
    piece_row = lax.broadcasted_iota(jnp.int32, (PIECE_ROWS, D_MODEL), 0)
    ones_pv = jnp.ones((3 * PIECE_ROWS, DV), BF16)
    zeros_pv = jnp.zeros((3 * PIECE_ROWS, DV), BF16)
    zeros_cv = jnp.zeros((CHUNK, DV), BF16)

    nchunk = rows // CHUNK
    last = piece_row == PIECE_ROWS - 1
    zero_p = jnp.zeros((PIECE_ROWS, D_MODEL), BF16)

    for c in range(nchunk):
        issue_proj(3, c)
        rs = slice(c * CHUNK, (c + 1) * CHUNK)
        b = b_all[rs, :]
        bl = b[CHUNK - 1:CHUNK, :]
        qe = (q_act[rs, :] * jnp.exp(b)).astype(BF16)
        ke = (kk[rs, :] * jnp.exp(-b)).astype(BF16)
        kd = (kk[rs, :] * jnp.exp(bl - b)).astype(BF16)
        pieces = jnp.concatenate(
            [jnp.where(last, e, zero_p) for e in _split3(jnp.exp(b[CHUNK - PIECE_ROWS:, :]))], axis=0)
        for h in range(HEADS):
            hs = slice(h * DK, (h + 1) * DK)
            qa_s[rs, h * QA_W:h * QA_W + DK] = qe[:, hs]
            att = jnp.where(causal, _dot_nt(qe[:, hs], ke[:, hs]), 0.0)
            qa_s[rs, h * QA_W + DK:h * QA_W + DK + CHUNK] = att.astype(BF16)
            lhs = jnp.concatenate([kd[:, hs], pieces[:, hs]], axis=0)
            rhs = jnp.concatenate([jnp.concatenate([vb_s[rs, hs], zeros_cv], axis=1),
                                   jnp.concatenate([zeros_pv, ones_pv], axis=1)], axis=0)
            ud_s[c * HEADS + h] = _dot_tn(lhs, rhs)

    for c in range(nchunk):
        rs = slice(c * CHUNK, (c + 1) * CHUNK)
        for h in range(HEADS):
            hs = slice(h * DK, (h + 1) * DK)
            s_old = s_ref[0, h]
            rhs = jnp.concatenate([s_old.astype(BF16), vb_s[rs, hs]], axis=0)
            o = _dot(qa_s[rs, h * QA_W:h * QA_W + DK + CHUNK], rhs)
            oa_s[rs, hs] = o * lax.rsqrt(jnp.mean(o * o, axis=-1, keepdims=True) + EPS)
            ud = ud_s[c * HEADS + h]
            s_ref[0, h] = ud[:, DV:] * s_old + ud[:, 0:DV]

    o_a = (oa_s[...] * onorm_ref[...] * _silu(gather(3))).astype(BF16)
    up_a = {}

    def issue_up_a(cb):
        if cb < ncb and cb not in up_a:
            up_a[cb] = _dot(o_a, waup_ref[:, cb * MXU_COLS:(cb + 1) * MXU_COLS])

    xr = proj(4)
    h_prev = hl_ref[0]
    sub = lax.broadcasted_iota(jnp.int32, (SUBLANES, LANES), 0)
    ngroup = rows // SCAN_GROUP
    tails = []
    h_last = []
    for n in range(LRU_BLOCKS):
        ns = slice(n * LRU_BW, (n + 1) * LRU_BW)
        xr_s[n, CONV_PAD:CONV_PAD + rows, :] = xr[:, ns]
        taps = [jnp.broadcast_to(convw_ref[j:j + 1, ns], (SUBLANES, LRU_BW)) for j in range(CONV_W)]
        bias = jnp.broadcast_to(convb_ref[:, ns], (SUBLANES, LRU_BW))
        xc_parts = []
        for grp in range(ngroup):
            xin = {d: xr_s[n, pl.ds(CONV_PAD + grp * SCAN_GROUP + d, SUBLANES, stride=SCAN_SEG), :]
                   for d in range(1 - CONV_W, SCAN_SEG)}
            for j in range(SCAN_SEG):
                acc = bias
                for tap in range(CONV_W):
                    acc = acc + xin[j + tap - (CONV_W - 1)] * taps[tap]
                xc_parts.append(acc)
        xc = jnp.concatenate(xc_parts, axis=0)
        tail = xr_s[n, CONV_PAD + rows - (CONV_W - 1):CONV_PAD + rows, :]
        tails.append(tail)
        xr_s[n, CONV_PAD - (CONV_W - 1):CONV_PAD, :] = tail
        pre = _dot(xc.astype(BF16), wax_ref[n])
        issue_proj(5 + n % 2, n // 2)
        a, bterm = _lru_coeffs(pre[:, :LRU_BW] + ba_ref[:, ns], pre[:, LRU_BW:] + bx_ref[:, ns],
                               xc, lam_ref[:, ns])
        carry = h_prev[:, ns]
        for grp in range(ngroup):
            def vreg(v, j, grp=grp):
                r0 = grp * SCAN_GROUP + j * SUBLANES
                return v[r0:r0 + SUBLANES, :]

            def strided(j, grp=grp):
                return pl.ds(grp * SCAN_GROUP + j, SUBLANES, stride=SCAN_SEG)
            pa = [vreg(a, 0)]
            hh = [vreg(bterm, 0)]
            for j in range(1, SCAN_SEG):
                aj = vreg(a, j)
                hh.append(aj * hh[-1] + vreg(bterm, j))
                pa.append(aj * pa[-1])
            pi, hi = pa[-1], hh[-1]
            for d in (1, 2, 4):
                keep = sub >= d
                hi = pi * jnp.where(keep, pltpu.roll(hi, d, 0), 0.0) + hi
                pi = pi * jnp.where(keep, pltpu.roll(pi, d, 0), 1.0)
            first = sub >= 1
            seg_in = (jnp.where(first, pltpu.roll(pi, 1, 0), 1.0) * carry
                      + jnp.where(first, pltpu.roll(hi, 1, 0), 0.0))
            carry = pi[SUBLANES - 1:, :] * carry + hi[SUBLANES - 1:, :]
            for j in range(SCAN_SEG):
                h_s[n, strided(j), :] = pa[j] * seg_in + hh[j]
        h_last.append(carry)
        if n % 2 == 0:
            issue_proj(7, n // 2)
        else:
            issue_up_a(n // 2)
    hl_ref[0] = jnp.concatenate(h_last, axis=1)
    cv_ref[0] = jnp.concatenate(tails, axis=1)
    gel = _gelu_tanh(gather(5))
    o_b = jnp.concatenate(
        [h_s[n] * gel[:, n * LRU_BW:(n + 1) * LRU_BW] for n in range(LRU_BLOCKS)], axis=1).astype(BF16)

    for cb in range(ncb):
        issue_up_a(cb)
    m = (_sigmoid(gather(6)) * jnp.concatenate([up_a[cb] for cb in range(ncb)], axis=1)
         + _sigmoid(gather(7)) * _dot(o_b, wbup_ref[...]))
    xo_ref[0] = x + _dot(m.astype(BF16), wout_ref[...])


def _mix_prompt(x, p, cast=()):
    batch, seq, _ = x.shape
    rows = MIX_ROWS
    assert seq % rows == 0 and rows % CHUNK == 0 and rows >= CONV_W - 1
    tiles = seq // rows
    row_vec = _resident((1, D_MODEL))
    cast_specs, cast_shapes = _cast_specs(cast, batch * tiles, lambda b, t: b * tiles + t)
    out_shape = (
        jax.ShapeDtypeStruct((batch, seq, D_MODEL), F32),
        jax.ShapeDtypeStruct((batch, HEADS, DK, DV), F32),
        jax.ShapeDtypeStruct((batch, 1, D_MODEL), F32),
        jax.ShapeDtypeStruct((batch, CONV_W - 1, D_MODEL), F32),
        *cast_shapes,
    )
    return pl.pallas_call(
        _mix_prompt_kernel,
        grid=(batch, seq // rows),
        in_specs=[
            pl.BlockSpec((1, rows, D_MODEL), lambda b, t: (b, t, 0)),
            row_vec,
            _resident((D_MODEL, N_SPLITS * D_MODEL)),
            _resident(p["lb_raw"].shape),
            row_vec,
            _resident((CONV_W, D_MODEL)),
            row_vec,
            _resident((LRU_BLOCKS, LRU_BW, 2 * LRU_BW)),
            row_vec, row_vec, row_vec,
            _resident((D_MODEL, D_MODEL)),
            _resident((D_MODEL, D_MODEL)),
            _resident((D_MODEL, D_MODEL)),
            *cast_specs,
        ],
        out_specs=(
            pl.BlockSpec((1, rows, D_MODEL), lambda b, t: (b, t, 0)),
            pl.BlockSpec((1, HEADS, DK, DV), lambda b, t: (b, 0, 0, 0)),
            pl.BlockSpec((1, 1, D_MODEL), lambda b, t: (b, 0, 0)),
            pl.BlockSpec((1, CONV_W - 1, D_MODEL), lambda b, t: (b, 0, 0)),
            *cast_specs,
        ),
        out_shape=out_shape,
        scratch_shapes=[
            pltpu.VMEM((rows, HEADS * QA_W), BF16),
            pltpu.VMEM((rows // CHUNK * HEADS, DK, 2 * DV), F32),
            pltpu.VMEM((rows, D_MODEL), BF16),
            pltpu.VMEM((rows, D_MODEL), F32),
            pltpu.VMEM((LRU_BLOCKS, CONV_PAD + rows, LRU_BW), F32),
            pltpu.VMEM((LRU_BLOCKS, rows, LRU_BW), F32),
            pltpu.VMEM((D_MODEL, N_SPLITS * D_MODEL), BF16),
        ],
        name="mix_prompt",
        compiler_params=pltpu.CompilerParams(
            dimension_semantics=("arbitrary", "arbitrary"), vmem_limit_bytes=VMEM_LIMIT_BYTES),
    )(x, p["mix_norm"], p["w_in"], p["lb_raw"], p["o_norm"], p["conv_w"], p["conv_b"],
      p["w_ax"], p["b_a"], p["b_x"], p["lam"], p["w_a_up"], p["w_b_up"], p["w_out"], *cast)


def _mix_sample_kernel(x_ref, nmix_ref, win_ref, lbraw_ref, onorm_ref, convw_ref, convb_ref,
                       wax_ref, ba_ref, bx_ref, lam_ref, waup_ref, wbup_ref, wout_ref,
                       sin_ref, hlin_ref, cvin_ref,
                       xo_ref, sout_ref, hl_ref, cv_ref,
                       qt_s, ft_s, v_s, oa_s, gsil_s, ga_s, mb_s):
    nseq = x_ref.shape[0]
    blk = SAMPLE_STATE_BLOCK
    j = pl.program_id(0)

    @pl.when(j == 0)
    def _():
        x = x_ref[...]
        hb = _rms(x, nmix_ref[...]).astype(BF16)

        def proj(i):
            return _dot(hb, win_ref[:, i * D_MODEL:(i + 1) * D_MODEL])

        lb = _lower_bound(lbraw_ref[...], 0)
        qt_s[...] = _silu(proj(0)).T
        ft_s[...] = (lb + (1.0 - lb) * _sigmoid(proj(1))).T
        v_s[...] = proj(2)
        gsil_s[...] = onorm_ref[...] * _silu(proj(3))

        xr = proj(4)
        xc = convb_ref[...] + xr * convw_ref[CONV_W - 1:CONV_W, :]
        for i in range(CONV_W - 1):
            xc = xc + cvin_ref[:, i, :] * convw_ref[i:i + 1, :]
        for i in range(CONV_W - 2):
            cv_ref[:, i, :] = cvin_ref[:, i + 1, :]
        cv_ref[:, CONV_W - 2, :] = xr

        xcb = xc.astype(BF16)
        h_parts = []
        for n in range(LRU_BLOCKS):
            ns = slice(n * LRU_BW, (n + 1) * LRU_BW)
            pre = _dot(xcb[:, ns], wax_ref[n])
            a, bterm = _lru_coeffs(pre[:, :LRU_BW] + ba_ref[:, ns], pre[:, LRU_BW:] + bx_ref[:, ns],
                                   xc[:, ns], lam_ref[:, ns])
            h_parts.append(a * hlin_ref[:, ns] + bterm)
        h_new = jnp.concatenate(h_parts, axis=1)
        hl_ref[...] = h_new
        o_b = (h_new * _gelu_tanh(proj(5))).astype(BF16)
        ga_s[...] = _sigmoid(proj(6))
        mb_s[...] = _sigmoid(proj(7)) * _dot(o_b, wbup_ref[...])

    b0 = pl.multiple_of(j * blk, blk)
    shifts = [lax.rem(nseq - b0 + p * blk, nseq) for p in range(3)]

    lane = lax.broadcasted_iota(jnp.int32, (DK, nseq), 1)
    sel_r = lax.broadcasted_iota(jnp.int32, (nseq, blk * DV), 0)
    sel_c = lax.broadcasted_iota(jnp.int32, (nseq, blk * DV), 1)
    selector = jnp.where((sel_r < 3 * blk) & (jnp.bitwise_and(sel_r, blk - 1) == sel_c // DV),
                         1.0, 0.0).astype(BF16)

    def spread(cols_t):
        p1 = cols_t.astype(BF16).astype(F32)
        r1 = cols_t - p1
        p2 = r1.astype(BF16).astype(F32)
        p3 = r1 - p2
        lhs = jnp.where(lane < blk, pltpu.roll(p1, shifts[0], 1),
                        jnp.where(lane < 2 * blk, pltpu.roll(p2, shifts[1], 1),
                                  jnp.where(lane < 3 * blk, pltpu.roll(p3, shifts[2], 1), 0.0)))
        return _dot(lhs.astype(BF16), selector)

    for h in range(HEADS):
        hrows = slice(h * DK, (h + 1) * DK)
        q_all = spread(qt_s[hrows, :])
        f_all = spread(ft_s[hrows, :])
        v_rows = v_s[pl.ds(b0, blk), hrows]
        o_rows = []
        for i in range(blk):
            f_col = f_all[:, i * DV:(i + 1) * DV]
            s_new = f_col * sin_ref[i, h] + (1.0 - f_col) * v_rows[i:i + 1, :]
            sout_ref[i, h] = s_new
            o_rows.append(jnp.sum(q_all[:, i * DV:(i + 1) * DV] * s_new, axis=0, keepdims=True))
        o = jnp.concatenate(o_rows, axis=0)
        oa_s[pl.ds(b0, blk), hrows] = o * lax.rsqrt(
            jnp.mean(o * o, axis=-1, keepdims=True) + EPS)

    @pl.when(j == pl.num_programs(0) - 1)
    def _():
        o_a = (oa_s[...] * gsil_s[...]).astype(BF16)
        m = ga_s[...] * _dot(o_a, waup_ref[...]) + mb_s[...]
        xo_ref[...] = x_ref[...] + _dot(m.astype(BF16), wout_ref[...])


def _mix_sample(x2d, s_in, h_in, c_in, p):
    nseq = x2d.shape[0]
    blk = SAMPLE_STATE_BLOCK
    assert nseq == LANES and nseq % blk == 0
    row_vec = _resident((1, D_MODEL))
    tok = _resident((nseq, D_MODEL))
    conv_state = _resident((nseq, CONV_W - 1, D_MODEL))
    out_shape = (
        jax.ShapeDtypeStruct((nseq, D_MODEL), F32),
        jax.ShapeDtypeStruct((nseq, HEADS, DK, DV), F32),
        jax.ShapeDtypeStruct((nseq, D_MODEL), F32),
        jax.ShapeDtypeStruct((nseq, CONV_W - 1, D_MODEL), F32),
    )
    state_spec = pl.BlockSpec((blk, HEADS, DK, DV), lambda j: (j, 0, 0, 0))
    return pl.pallas_call(
        _mix_sample_kernel,
        grid=(nseq // blk,),
        in_specs=[
            tok, row_vec,
            _resident((D_MODEL, N_SPLITS * D_MODEL)),
            _resident(p["lb_raw"].shape),
            row_vec,
            _resident((CONV_W, D_MODEL)),
            row_vec,
            _resident((LRU_BLOCKS, LRU_BW, 2 * LRU_BW)),
            row_vec, row_vec, row_vec,
            _resident((D_MODEL, D_MODEL)),
            _resident((D_MODEL, D_MODEL)),
            _resident((D_MODEL, D_MODEL)),
            state_spec, tok, conv_state,
        ],
        out_specs=(
            pl.BlockSpec((nseq, D_MODEL), lambda j: (0, 0)),
            state_spec,
            pl.BlockSpec((nseq, D_MODEL), lambda j: (0, 0)),
            pl.BlockSpec((nseq, CONV_W - 1, D_MODEL), lambda j: (0, 0, 0)),
        ),
        out_shape=out_shape,
        scratch_shapes=[
            pltpu.VMEM((D_MODEL, nseq), F32),
            pltpu.VMEM((D_MODEL, nseq), F32),
            pltpu.VMEM((nseq, D_MODEL), F32),
            pltpu.VMEM((nseq, D_MODEL), F32),
            pltpu.VMEM((nseq, D_MODEL), F32),
            pltpu.VMEM((nseq, D_MODEL), F32),
            pltpu.VMEM((nseq, D_MODEL), F32),
        ],
        name="mix_sample",
        compiler_params=pltpu.CompilerParams(
            dimension_semantics=("arbitrary",), vmem_limit_bytes=VMEM_LIMIT_BYTES),
    )(x2d, p["mix_norm"], p["w_in"], p["lb_raw"], p["o_norm"], p["conv_w"], p["conv_b"],
      p["w_ax"], p["b_a"], p["b_x"], p["lam"], p["w_a_up"], p["w_b_up"], p["w_out"],
      s_in, h_in, c_in)


def kernel(x_prompt, x_sample, state_hgrn, state_lru, state_conv, ffn1_norm, ffn1_w_gate, ffn1_w_up, ffn1_w_down, mix_norm, w_in, hgrn_lower_bounds, hgrn_out_norm, conv_w, conv_b, lru_w_a, lru_b_a, lru_w_x, lru_b_x, lru_lambda, w_a_up, w_b_up, w_out, ffn2_norm, ffn2_w_gate, ffn2_w_up, ffn2_w_down, final_norm):
    depth = w_in.shape[0]
    assert depth == 1, "single-layer trunk"
    l = 0
    row = lambda v: v.reshape(1, -1).astype(F32)
    bf = lambda w: w.astype(BF16)
    p = {
        "mix_norm": row(mix_norm[l]),
        "lb_raw": hgrn_lower_bounds.astype(F32),
        "o_norm": row(hgrn_out_norm[l]),
        "conv_w": conv_w[l].astype(F32),
        "conv_b": row(conv_b[l]),
        "b_a": row(lru_b_a[l]),
        "b_x": row(lru_b_x[l]),
        "lam": row(lru_lambda[l]),
    }
    f1 = (row(ffn1_norm[l]), bf(ffn1_w_gate[l]), bf(ffn1_w_up[l]), bf(ffn1_w_down[l]))
    gfin = row(final_norm)

    bp, tp, _ = x_prompt.shape
    bs, ts, _ = x_sample.shape
    assert ts == 1
    flat = lambda w: w.reshape(-1, w.shape[-1])
    w_ax = jnp.concatenate([lru_w_a[l], lru_w_x[l]], axis=-1)
    xp, xs, p["w_in"], p["w_a_up"], p["w_b_up"], p["w_out"], w_ax = _ffn(
        x_prompt.reshape(bp * tp, D_MODEL), x_sample.reshape(bs, D_MODEL), *f1, gfin,
        final_norm=False, cast=(flat(w_in), flat(w_a_up), flat(w_b_up), flat(w_out), flat(w_ax)))
    p["w_ax"] = w_ax.reshape(LRU_BLOCKS, LRU_BW, 2 * LRU_BW)
    xp, s_p, h_p, c_p, wg2, wu2, wd2 = _mix_prompt(
        xp.reshape(bp, tp, D_MODEL), p,
        cast=(flat(ffn2_w_gate), flat(ffn2_w_up), flat(ffn2_w_down)))
    f2 = (row(ffn2_norm[l]), wg2, wu2, wd2)
    xs, s_s, h_s, c_s = _mix_sample(xs, state_hgrn[l], state_lru[l], state_conv[l], p)
    y_p, y_s = _ffn(xp.reshape(bp * tp, D_MODEL), xs, *f2, gfin, final_norm=True)
    y_p = y_p.reshape(bp, tp, D_MODEL)
    y_s = y_s.reshape(bs, ts, D_MODEL)

    return (y_p, y_s, s_p[None], h_p.reshape(bp, D_MODEL)[None], c_p[None],
            s_s[None], h_s[None], c_s[None])
```

```python
import functools

import jax
import jax.numpy as jnp
from jax import lax
from jax.experimental import pallas as pl
from jax.experimental.pallas import tpu as pltpu

D_MODEL = 1024
HEADS = 8
DK = 128
DV = 128
CHUNK = 64
LRU_BLOCKS = 8
LRU_BW = 128
LRU_C = 8.0
CONV_W = 4
D_FF = 2816
EPS = 1e-6
LOG2_E = 1.4426950408889634
N_SPLITS = 8

SUBLANES = 8
LANES = 128
MXU_COLS = 256
VMEM_LIMIT_BYTES = 56 * 1024 * 1024

FFN_ROWS = 1024
MIX_ROWS = 256
SAMPLE_STATE_BLOCK = 8

BF16 = jnp.bfloat16
F32 = jnp.float32


def _dot(a, b):
    return jnp.dot(a, b, preferred_element_type=F32)


def _dot_tn(a, b):
    return lax.dot_general(a, b, (((0,), (0,)), ((), ())), preferred_element_type=F32)


def _dot_nt(a, b):
    return lax.dot_general(a, b, (((1,), (1,)), ((), ())), preferred_element_type=F32)


def _sigmoid(x):
    return 1.0 / (1.0 + jnp.exp2(x * (-LOG2_E)))


def _silu(x):
    return x * _sigmoid(x)


def _gelu_tanh(x):
    c = 0.7978845608028654
    return 0.5 * x * (1.0 + jnp.tanh(c * (x + 0.044715 * (x * x * x))))


def _rms(x, g):
    return x * lax.rsqrt(jnp.mean(x * x, axis=-1, keepdims=True) + EPS) * g


def _lower_bound(raw, layer):
    m = jnp.max(raw, axis=0, keepdims=True)
    e = jnp.exp(raw - m)
    den = jnp.sum(e, axis=0, keepdims=True)
    num = jnp.sum(e[0:layer + 1], axis=0, keepdims=True)
    return num / den


def _softplus(x):
    return jnp.maximum(x, 0.0) + jnp.log1p(jnp.exp(-jnp.abs(x)))


def _lru_coeffs(r_pre, i_pre, xc, lam):
    r = _sigmoid(r_pre)
    ig = _sigmoid(i_pre)
    log_a = (-LRU_C) * r * _softplus(-lam)
    a = jnp.exp(log_a)
    th = jnp.tanh(log_a)
    mult = jnp.sqrt(-2.0 * th / (1.0 - th))
    return a, mult * (ig * xc)


def _split3(x):
    p1 = x.astype(BF16)
    r1 = x - p1.astype(F32)
    p2 = r1.astype(BF16)
    r2 = r1 - p2.astype(F32)
    return p1, p2, r2.astype(BF16)


def _cast_rows(src_refs, dst_refs):
    for src, dst in zip(src_refs, dst_refs):
        dst[...] = src[...].astype(BF16)


def _cast_specs(weights, steps, step_index):
    specs, shapes = [], []
    for w in weights:
        assert w.ndim == 2
        count = max(c for c in range(1, steps + 1)
                    if w.shape[0] % c == 0 and (w.shape[0] // c) % (2 * SUBLANES) == 0)
        index = functools.partial(
            lambda *ids, last: (jnp.minimum(step_index(*ids), last), 0), last=count - 1)
        specs.append(pl.BlockSpec((w.shape[0] // count, w.shape[1]), index))
        shapes.append(jax.ShapeDtypeStruct(w.shape, BF16))
    return specs, shapes


def _ffn_kernel(xa_ref, xb_ref, g_ref, wg_ref, wu_ref, wd_ref, gf_ref, *rest, final_norm, a_tiles):
    n_cast = (len(rest) - 2) // 2
    oa_ref, ob_ref = rest[n_cast:n_cast + 2]
    _cast_rows(rest[:n_cast], rest[n_cast + 2:])

    def half_step(x_ref, o_ref):
        x = x_ref[...]
        inv_rms = lax.rsqrt(jnp.mean(x * x, axis=-1, keepdims=True) + EPS)
        h = (x * g_ref[...]).astype(BF16)
        gate = _dot(h, wg_ref[...]) * inv_rms
        up = _dot(h, wu_ref[...]) * inv_rms
        act = (_silu(gate) * up).astype(BF16)
        y = x + 0.5 * _dot(act, wd_ref[...])
        if final_norm:
            y = _rms(y, gf_ref[...])
        o_ref[...] = y

    i = pl.program_id(0)

    @pl.when(i < a_tiles)
    def _():
        half_step(xa_ref, oa_ref)

    @pl.when(i == a_tiles)
    def _():
        half_step(xb_ref, ob_ref)


def _resident(shape):
    nd = len(shape)
    return pl.BlockSpec(shape, lambda *_: (0,) * nd, pipeline_mode=pl.Buffered(1))


def _ffn(xa, xb, g, wg, wu, wd, gf, *, final_norm, cast=()):
    na, nb = xa.shape[0], xb.shape[0]
    assert na % FFN_ROWS == 0 and nb <= FFN_ROWS and nb % SUBLANES == 0
    a_tiles = na // FFN_ROWS
    a_tile = lambda i: (jnp.minimum(i, a_tiles - 1), 0)
    b_whole = lambda i: (0, 0)
    cast_specs, cast_shapes = _cast_specs(cast, a_tiles, lambda i: i)
    return pl.pallas_call(
        functools.partial(_ffn_kernel, final_norm=final_norm, a_tiles=a_tiles),
        grid=(a_tiles + 1,),
        in_specs=[
            pl.BlockSpec((FFN_ROWS, D_MODEL), a_tile),
            pl.BlockSpec((nb, D_MODEL), b_whole),
            _resident((1, D_MODEL)),
            _resident((D_MODEL, D_FF)),
            _resident((D_MODEL, D_FF)),
            _resident((D_FF, D_MODEL)),
            _resident((1, D_MODEL)),
            *cast_specs,
        ],
        out_specs=(pl.BlockSpec((FFN_ROWS, D_MODEL), a_tile),
                   pl.BlockSpec((nb, D_MODEL), b_whole),
                   *cast_specs),
        out_shape=(jax.ShapeDtypeStruct((na, D_MODEL), F32),
                   jax.ShapeDtypeStruct((nb, D_MODEL), F32),
                   *cast_shapes),
        name="ffn_final" if final_norm else "ffn",
        compiler_params=pltpu.CompilerParams(
            dimension_semantics=("arbitrary",), vmem_limit_bytes=VMEM_LIMIT_BYTES),
    )(xa, xb, g, wg, wu, wd, gf, *cast)


SCAN_SEG = 4
SCAN_GROUP = SCAN_SEG * SUBLANES
CONV_PAD = SUBLANES
PIECE_ROWS = 2 * SUBLANES
QA_W = 2 * LANES


MIX_INPUTS = 15
MIX_OUTPUTS = 4
MIX_SCRATCH = 7


def _mix_prompt_kernel(*refs):
    n_cast = (len(refs) - MIX_INPUTS - MIX_OUTPUTS - MIX_SCRATCH) // 2
    (x_ref, xnext_ref, nmix_ref, win_ref, lbraw_ref, onorm_ref, convw_ref, convb_ref,
     wax_ref, ba_ref, bx_ref, lam_ref, waup_ref, wbup_ref, wout_ref) = refs[:MIX_INPUTS]
    outs = refs[MIX_INPUTS + n_cast:]
    xo_ref, s_ref, hl_ref, cv_ref = outs[:MIX_OUTPUTS]
    qa_s, ud_s, vb_s, oa_s, xr_s, h_s, hb_s = outs[MIX_OUTPUTS + n_cast:]
    _cast_rows(refs[MIX_INPUTS:MIX_INPUTS + n_cast], outs[MIX_OUTPUTS:MIX_OUTPUTS + n_cast])

    rows = MIX_ROWS
    t = pl.program_id(1)

    @pl.when(t == 0)
    def _():
        s_ref[...] = jnp.zeros_like(s_ref)
        hl_ref[...] = jnp.zeros_like(hl_ref)
        xr_s[:, 0:CONV_PAD, :] = jnp.zeros((LRU_BLOCKS, CONV_PAD, LRU_BW), F32)

    @pl.when((pl.program_id(0) == 0) & (t == 0))
    def _():
        hb_s[...] = _rms(x_ref[0], nmix_ref[...]).astype(BF16)

    x = x_ref[0]
    hb = hb_s[...]

    def proj(i):
        return _dot(hb, win_ref[:, i * D_MODEL:(i + 1) * D_MODEL])

    early = {}

    ncb = D_MODEL // MXU_COLS

    def issue_proj(i, cb):
        if cb < ncb and (i, cb) not in early:
            c0 = i * D_MODEL + cb * MXU_COLS
            early[(i, cb)] = _dot(hb, win_ref[:, c0:c0 + MXU_COLS])

    def gather(i):
        for cb in range(ncb):
            issue_proj(i, cb)
        return jnp.concatenate([early[(i, cb)] for cb in range(ncb)], axis=1)

    lb = _lower_bound(lbraw_ref[...], 0)
    q_act = _silu(proj(0))
    f = lb + (1.0 - lb) * _sigmoid(proj(1))
    kk = 1.0 - f
    vb_s[...] = proj(2).astype(BF16)

    ri = lax.broadcasted_iota(jnp.int32, (rows, rows), 0)
    ci = lax.broadcasted_iota(jnp.int32, (rows, rows), 1)
    same_chunk = (ri // CHUNK) == (ci // CHUNK)
    tril_blk = jnp.where((ri >= ci) & same_chunk, 1.0, 0.0).astype(BF16)
    logf = jnp.log(f)
    g1 = logf.astype(BF16)
    g2 = (logf - g1.astype(F32)).astype(BF16)
    b_all = _dot(tril_blk, g1) + _dot(tril_blk, g2)

    causal = (lax.broadcasted_iota(jnp.int32, (CHUNK, CHUNK), 0)
              >= lax.broadcasted_iota(jnp.int32, (CHUNK, CHUNK), 1))
    piece_row = lax.broadcasted_iota(jnp.int32, (PIECE_ROWS, D_MODEL), 0)
    ones_pv = jnp.ones((3 * PIECE_ROWS, DV), BF16)
    zeros_pv = jnp.zeros((3 * PIECE_ROWS, DV), BF16)
    zeros_cv = jnp.zeros((CHUNK, DV), BF16)

    nchunk = rows // CHUNK
    last = piece_row == PIECE_ROWS - 1
    zero_p = jnp.zeros((PIECE_ROWS, D_MODEL), BF16)

    for c in range(nchunk):
        issue_proj(3, c)
        rs = slice(c * CHUNK, (c + 1) * CHUNK)
        b = b_all[rs, :]
        bl = b[CHUNK - 1:CHUNK, :]
        qe = (q_act[rs, :] * jnp.exp(b)).astype(BF16)
        ke = (kk[rs, :] * jnp.exp(-b)).astype(BF16)
        kd = (kk[rs, :] * jnp.exp(bl - b)).astype(BF16)
        pieces = jnp.concatenate(
            [jnp.where(last, e, zero_p) for e in _split3(jnp.exp(b[CHUNK - PIECE_ROWS:, :]))], axis=0)
        for h in range(HEADS):
            hs = slice(h * DK, (h + 1) * DK)
            qa_s[rs, h * QA_W:h * QA_W + DK] = qe[:, hs]
            att = jnp.where(causal, _dot_nt(qe[:, hs], ke[:, hs]), 0.0)
            qa_s[rs, h * QA_W + DK:h * QA_W + DK + CHUNK] = att.astype(BF16)
            lhs = jnp.concatenate([kd[:, hs], pieces[:, hs]], axis=0)
            rhs = jnp.concatenate([jnp.concatenate([vb_s[rs, hs], zeros_cv], axis=1),
                                   jnp.concatenate([zeros_pv, ones_pv], axis=1)], axis=0)
            ud_s[c * HEADS + h] = _dot_tn(lhs, rhs)

    for c in range(nchunk):
        rs = slice(c * CHUNK, (c + 1) * CHUNK)
        for h in range(HEADS):
            hs = slice(h * DK, (h + 1) * DK)
            s_old = s_ref[0, h]
            rhs = jnp.concatenate([s_old.astype(BF16), vb_s[rs, hs]], axis=0)
            o = _dot(qa_s[rs, h * QA_W:h * QA_W + DK + CHUNK], rhs)
            oa_s[rs, hs] = o * lax.rsqrt(jnp.mean(o * o, axis=-1, keepdims=True) + EPS)
            ud = ud_s[c * HEADS + h]
            s_ref[0, h] = ud[:, DV:] * s_old + ud[:, 0:DV]

    o_a = (oa_s[...] * onorm_ref[...] * _silu(gather(3))).astype(BF16)
    up_a = {}

    def issue_up_a(cb):
        if cb < ncb and cb not in up_a:
            up_a[cb] = _dot(o_a, waup_ref[:, cb * MXU_COLS:(cb + 1) * MXU_COLS])

    xr = proj(4)
    h_prev = hl_ref[0]
    sub = lax.broadcasted_iota(jnp.int32, (SUBLANES, LANES), 0)
    ngroup = rows // SCAN_GROUP
    tails = []
    h_last = []
    for n in range(LRU_BLOCKS):
        ns = slice(n * LRU_BW, (n + 1) * LRU_BW)
        xr_s[n, CONV_PAD:CONV_PAD + rows, :] = xr[:, ns]
        taps = [jnp.broadcast_to(convw_ref[j:j + 1, ns], (SUBLANES, LRU_BW)) for j in range(CONV_W)]
        bias = jnp.broadcast_to(convb_ref[:, ns], (SUBLANES, LRU_BW))
        xc_parts = []
        for grp in range(ngroup):
            xin = {d: xr_s[n, pl.ds(CONV_PAD + grp * SCAN_GROUP + d, SUBLANES, stride=SCAN_SEG), :]
                   for d in range(1 - CONV_W, SCAN_SEG)}
            for j in range(SCAN_SEG):
                acc = bias
                for tap in range(CONV_W):
                    acc = acc + xin[j + tap - (CONV_W - 1)] * taps[tap]
                xc_parts.append(acc)
        xc = jnp.concatenate(xc_parts, axis=0)
        tail = xr_s[n, CONV_PAD + rows - (CONV_W - 1):CONV_PAD + rows, :]
        tails.append(tail)
        xr_s[n, CONV_PAD - (CONV_W - 1):CONV_PAD, :] = tail
        pre = _dot(xc.astype(BF16), wax_ref[n])
        issue_proj(5 + n % 2, n // 2)
        a, bterm = _lru_coeffs(pre[:, :LRU_BW] + ba_ref[:, ns], pre[:, LRU_BW:] + bx_ref[:, ns],
                               xc, lam_ref[:, ns])
        carry = h_prev[:, ns]
        for grp in range(ngroup):
            def vreg(v, j, grp=grp):
                r0 = grp * SCAN_GROUP + j * SUBLANES
                return v[r0:r0 + SUBLANES, :]

            def strided(j, grp=grp):
                return pl.ds(grp * SCAN_GROUP + j, SUBLANES, stride=SCAN_SEG)
            pa = [vreg(a, 0)]
            hh = [vreg(bterm, 0)]
            for j in range(1, SCAN_SEG):
                aj = vreg(a, j)
                hh.append(aj * hh[-1] + vreg(bterm, j))
                pa.append(aj * pa[-1])
            pi, hi = pa[-1], hh[-1]
            for d in (1, 2, 4):
                keep = sub >= d
                hi = pi * jnp.where(keep, pltpu.roll(hi, d, 0), 0.0) + hi
                pi = pi * jnp.where(keep, pltpu.roll(pi, d, 0), 1.0)
            first = sub >= 1
            seg_in = (jnp.where(first, pltpu.roll(pi, 1, 0), 1.0) * carry
                      + jnp.where(first, pltpu.roll(hi, 1, 0), 0.0))
            carry = pi[SUBLANES - 1:, :] * carry + hi[SUBLANES - 1:, :]
            for j in range(SCAN_SEG):
                h_s[n, strided(j), :] = pa[j] * seg_in + hh[j]
        h_last.append(carry)
        if n % 2 == 0:
            issue_proj(7, n // 2)
        else:
            issue_up_a(n // 2)
    hl_ref[0] = jnp.concatenate(h_last, axis=1)
    cv_ref[0] = jnp.concatenate(tails, axis=1)
    gel = _gelu_tanh(gather(5))
    o_b = jnp.concatenate(
        [h_s[n] * gel[:, n * LRU_BW:(n + 1) * LRU_BW] for n in range(LRU_BLOCKS)], axis=1).astype(BF16)

    for cb in range(ncb):
        issue_up_a(cb)
    m = (_sigmoid(gather(6)) * jnp.concatenate([up_a[cb] for cb in range(ncb)], axis=1)
         + _sigmoid(gather(7)) * _dot(o_b, wbup_ref[...]))
    xo_ref[0] = x + _dot(m.astype(BF16), wout_ref[...])
    hb_s[...] = _rms(xnext_ref[0], nmix_ref[...]).astype(BF16)


def _mix_prompt(x, p, cast=()):
    batch, seq, _ = x.shape
    rows = MIX_ROWS
    assert seq % rows == 0 and rows % CHUNK == 0 and rows >= CONV_W - 1
    tiles = seq // rows
    row_vec = _resident((1, D_MODEL))
    cast_specs, cast_shapes = _cast_specs(cast, batch * tiles, lambda b, t: b * tiles + t)

    def next_tile(b, t):
        nxt = jnp.minimum(b * tiles + t + 1, batch * tiles - 1)
        return (nxt // tiles, nxt % tiles, 0)

    out_shape = (
        jax.ShapeDtypeStruct((batch, seq, D_MODEL), F32),
        jax.ShapeDtypeStruct((batch, HEADS, DK, DV), F32),
        jax.ShapeDtypeStruct((batch, 1, D_MODEL), F32),
        jax.ShapeDtypeStruct((batch, CONV_W - 1, D_MODEL), F32),
        *cast_shapes,
    )
    return pl.pallas_call(
        _mix_prompt_kernel,
        grid=(batch, seq // rows),
        in_specs=[
            pl.BlockSpec((1, rows, D_MODEL), lambda b, t: (b, t, 0)),
            pl.BlockSpec((1, rows, D_MODEL), next_tile),
            row_vec,
            _resident((D_MODEL, N_SPLITS * D_MODEL)),
            _resident(p["lb_raw"].shape),
            row_vec,
            _resident((CONV_W, D_MODEL)),
            row_vec,
            _resident((LRU_BLOCKS, LRU_BW, 2 * LRU_BW)),
            row_vec, row_vec, row_vec,
            _resident((D_MODEL, D_MODEL)),
            _resident((D_MODEL, D_MODEL)),
            _resident((D_MODEL, D_MODEL)),
            *cast_specs,
        ],
        out_specs=(
            pl.BlockSpec((1, rows, D_MODEL), lambda b, t: (b, t, 0)),
            pl.BlockSpec((1, HEADS, DK, DV), lambda b, t: (b, 0, 0, 0)),
            pl.BlockSpec((1, 1, D_MODEL), lambda b, t: (b, 0, 0)),
            pl.BlockSpec((1, CONV_W - 1, D_MODEL), lambda b, t: (b, 0, 0)),
            *cast_specs,
        ),
        out_shape=out_shape,
        scratch_shapes=[
            pltpu.VMEM((rows, HEADS * QA_W), BF16),
            pltpu.VMEM((rows // CHUNK * HEADS, DK, 2 * DV), F32),
            pltpu.VMEM((rows, D_MODEL), BF16),
            pltpu.VMEM((rows, D_MODEL), F32),
            pltpu.VMEM((LRU_BLOCKS, CONV_PAD + rows, LRU_BW), F32),
            pltpu.VMEM((LRU_BLOCKS, rows, LRU_BW), F32),
            pltpu.VMEM((rows, D_MODEL), BF16),
        ],
        name="mix_prompt",
        compiler_params=pltpu.CompilerParams(
            dimension_semantics=("arbitrary", "arbitrary"), vmem_limit_bytes=VMEM_LIMIT_BYTES),
    )(x, x, p["mix_norm"], p["w_in"], p["lb_raw"], p["o_norm"], p["conv_w"], p["conv_b"],
      p["w_ax"], p["b_a"], p["b_x"], p["lam"], p["w_a_up"], p["w_b_up"], p["w_out"], *cast)


def _mix_sample_kernel(x_ref, nmix_ref, win_ref, lbraw_ref, onorm_ref, convw_ref, convb_ref,
                       wax_ref, ba_ref, bx_ref, lam_ref, waup_ref, wbup_ref, wout_ref,
                       sin_ref, hlin_ref, cvin_ref,
                       xo_ref, sout_ref, hl_ref, cv_ref,
                       qt_s, ft_s, v_s, oa_s, gsil_s, ga_s, mb_s):
    nseq = x_ref.shape[0]
    blk = SAMPLE_STATE_BLOCK
    j = pl.program_id(0)

    @pl.when(j == 0)
    def _():
        x = x_ref[...]
        hb = _rms(x, nmix_ref[...]).astype(BF16)

        def proj(i):
            return _dot(hb, win_ref[:, i * D_MODEL:(i + 1) * D_MODEL])

        lb = _lower_bound(lbraw_ref[...], 0)
        qt_s[...] = _silu(proj(0)).T
        ft_s[...] = (lb + (1.0 - lb) * _sigmoid(proj(1))).T
        v_s[...] = proj(2)
        gsil_s[...] = onorm_ref[...] * _silu(proj(3))

        xr = proj(4)
        xc = convb_ref[...] + xr * convw_ref[CONV_W - 1:CONV_W, :]
        for i in range(CONV_W - 1):
            xc = xc + cvin_ref[:, i, :] * convw_ref[i:i + 1, :]
        for i in range(CONV_W - 2):
            cv_ref[:, i, :] = cvin_ref[:, i + 1, :]
        cv_ref[:, CONV_W - 2, :] = xr

        xcb = xc.astype(BF16)
        h_parts = []
        for n in range(LRU_BLOCKS):
            ns = slice(n * LRU_BW, (n + 1) * LRU_BW)
            pre = _dot(xcb[:, ns], wax_ref[n])
            a, bterm = _lru_coeffs(pre[:, :LRU_BW] + ba_ref[:, ns], pre[:, LRU_BW:] + bx_ref[:, ns],
                                   xc[:, ns], lam_ref[:, ns])
            h_parts.append(a * hlin_ref[:, ns] + bterm)
        h_new = jnp.concatenate(h_parts, axis=1)
        hl_ref[...] = h_new
        o_b = (h_new * _gelu_tanh(proj(5))).astype(BF16)
        ga_s[...] = _sigmoid(proj(6))
        mb_s[...] = _sigmoid(proj(7)) * _dot(o_b, wbup_ref[...])

    b0 = pl.multiple_of(j * blk, blk)
    shifts = [lax.rem(nseq - b0 + p * blk, nseq) for p in range(3)]

    lane = lax.broadcasted_iota(jnp.int32, (DK, nseq), 1)
    sel_r = lax.broadcasted_iota(jnp.int32, (nseq, blk * DV), 0)
    sel_c = lax.broadcasted_iota(jnp.int32, (nseq, blk * DV), 1)
    selector = jnp.where((sel_r < 3 * blk) & (jnp.bitwise_and(sel_r, blk - 1) == sel_c // DV),
                         1.0, 0.0).astype(BF16)

    def spread(cols_t):
        p1 = cols_t.astype(BF16).astype(F32)
        r1 = cols_t - p1
        p2 = r1.astype(BF16).astype(F32)
        p3 = r1 - p2
        lhs = jnp.where(lane < blk, pltpu.roll(p1, shifts[0], 1),
                        jnp.where(lane < 2 * blk, pltpu.roll(p2, shifts[1], 1),
                                  jnp.where(lane < 3 * blk, pltpu.roll(p3, shifts[2], 1), 0.0)))
        return _dot(lhs.astype(BF16), selector)

    for h in range(HEADS):
        hrows = slice(h * DK, (h + 1) * DK)
        q_all = spread(qt_s[hrows, :])
        f_all = spread(ft_s[hrows, :])
        v_rows = v_s[pl.ds(b0, blk), hrows]
        o_rows = []
        for i in range(blk):
            f_col = f_all[:, i * DV:(i + 1) * DV]
            s_new = f_col * sin_ref[i, h] + (1.0 - f_col) * v_rows[i:i + 1, :]
            sout_ref[i, h] = s_new
            o_rows.append(jnp.sum(q_all[:, i * DV:(i + 1) * DV] * s_new, axis=0, keepdims=True))
        o = jnp.concatenate(o_rows, axis=0)
        oa_s[pl.ds(b0, blk), hrows] = o * lax.rsqrt(
            jnp.mean(o * o, axis=-1, keepdims=True) + EPS)

    @pl.when(j == pl.num_programs(0) - 1)
    def _():
        o_a = (oa_s[...] * gsil_s[...]).astype(BF16)
        m = ga_s[...] * _dot(o_a, waup_ref[...]) + mb_s[...]
        xo_ref[...] = x_ref[...] + _dot(m.astype(BF16), wout_ref[...])


def _mix_sample(x2d, s_in, h_in, c_in, p):
    nseq = x2d.shape[0]
    blk = SAMPLE_STATE_BLOCK
    assert nseq == LANES and nseq % blk == 0
    row_vec = _resident((1, D_MODEL))
    tok = _resident((nseq, D_MODEL))
    conv_state = _resident((nseq, CONV_W - 1, D_MODEL))
    out_shape = (
        jax.ShapeDtypeStruct((nseq, D_MODEL), F32),
        jax.ShapeDtypeStruct((nseq, HEADS, DK, DV), F32),
        jax.ShapeDtypeStruct((nseq, D_MODEL), F32),
        jax.ShapeDtypeStruct((nseq, CONV_W - 1, D_MODEL), F32),
    )
    state_spec = pl.BlockSpec((blk, HEADS, DK, DV), lambda j: (j, 0, 0, 0))
    return pl.pallas_call(
        _mix_sample_kernel,
        grid=(nseq // blk,),
        in_specs=[
            tok, row_vec,
            _resident((D_MODEL, N_SPLITS * D_MODEL)),
            _resident(p["lb_raw"].shape),
            row_vec,
            _resident((CONV_W, D_MODEL)),
            row_vec,
            _resident((LRU_BLOCKS, LRU_BW, 2 * LRU_BW)),
            row_vec, row_vec, row_vec,
            _resident((D_MODEL, D_MODEL)),
            _resident((D_MODEL, D_MODEL)),
            _resident((D_MODEL, D_MODEL)),
            state_spec, tok, conv_state,
        ],
        out_specs=(
            pl.BlockSpec((nseq, D_MODEL), lambda j: (0, 0)),
            state_spec,
            pl.BlockSpec((nseq, D_MODEL), lambda j: (0, 0)),
            pl.BlockSpec((nseq, CONV_W - 1, D_MODEL), lambda j: (0, 0, 0)),
        ),
        out_shape=out_shape,
        scratch_shapes=[
            pltpu.VMEM((D_MODEL, nseq), F32),
            pltpu.VMEM((D_MODEL, nseq), F32),
            pltpu.VMEM((nseq, D_MODEL), F32),
            pltpu.VMEM((nseq, D_MODEL), F32),
            pltpu.VMEM((nseq, D_MODEL), F32),
            pltpu.VMEM((nseq, D_MODEL), F32),
            pltpu.VMEM((nseq, D_MODEL), F32),
        ],
        name="mix_sample",
        compiler_params=pltpu.CompilerParams(
            dimension_semantics=("arbitrary",), vmem_limit_bytes=VMEM_LIMIT_BYTES),
    )(x2d, p["mix_norm"], p["w_in"], p["lb_raw"], p["o_norm"], p["conv_w"], p["conv_b"],
      p["w_ax"], p["b_a"], p["b_x"], p["lam"], p["w_a_up"], p["w_b_up"], p["w_out"],
      s_in, h_in, c_in)


def kernel(x_prompt, x_sample, state_hgrn, state_lru, state_conv, ffn1_norm, ffn1_w_gate, ffn1_w_up, ffn1_w_down, mix_norm, w_in, hgrn_lower_bounds, hgrn_out_norm, conv_w, conv_b, lru_w_a, lru_b_a, lru_w_x, lru_b_x, lru_lambda, w_a_up, w_b_up, w_out, ffn2_norm, ffn2_w_gate, ffn2_w_up, ffn2_w_down, final_norm):
    depth = w_in.shape[0]
    assert depth == 1, "single-layer trunk"
    l = 0
    row = lambda v: v.reshape(1, -1).astype(F32)
    bf = lambda w: w.astype(BF16)
    p = {
        "mix_norm": row(mix_norm[l]),
        "lb_raw": hgrn_lower_bounds.astype(F32),
        "o_norm": row(hgrn_out_norm[l]),
        "conv_w": conv_w[l].astype(F32),
        "conv_b": row(conv_b[l]),
        "b_a": row(lru_b_a[l]),
        "b_x": row(lru_b_x[l]),
        "lam": row(lru_lambda[l]),
    }
    f1 = (row(ffn1_norm[l]), bf(ffn1_w_gate[l]), bf(ffn1_w_up[l]), bf(ffn1_w_down[l]))
    gfin = row(final_norm)

    bp, tp, _ = x_prompt.shape
    bs, ts, _ = x_sample.shape
    assert ts == 1
    flat = lambda w: w.reshape(-1, w.shape[-1])
    w_ax = jnp.concatenate([lru_w_a[l], lru_w_x[l]], axis=-1)
    xp, xs, p["w_in"], p["w_a_up"], p["w_b_up"], p["w_out"], w_ax = _ffn(
        x_prompt.reshape(bp * tp, D_MODEL), x_sample.reshape(bs, D_MODEL), *f1, gfin,
        final_norm=False, cast=(flat(w_in), flat(w_a_up), flat(w_b_up), flat(w_out), flat(w_ax)))
    p["w_ax"] = w_ax.reshape(LRU_BLOCKS, LRU_BW, 2 * LRU_BW)
    xp, s_p, h_p, c_p, wg2, wu2, wd2 = _mix_prompt(
        xp.reshape(bp, tp, D_MODEL), p,
        cast=(flat(ffn2_w_gate), flat(ffn2_w_up), flat(ffn2_w_down)))
    f2 = (row(ffn2_norm[l]), wg2, wu2, wd2)
    xs, s_s, h_s, c_s = _mix_sample(xs, state_hgrn[l], state_lru[l], state_conv[l], p)
    y_p, y_s = _ffn(xp.reshape(bp * tp, D_MODEL), xs, *f2, gfin, final_norm=True)
    y_p = y_p.reshape(bp, tp, D_MODEL)
    y_s = y_s.reshape(bs, ts, D_MODEL)

    return (y_p, y_s, s_p[None], h_p.reshape(bp, D_MODEL)[None], c_p[None],
            s_s[None], h_s[None], c_s[None])
```

```python
import functools

import jax
import jax.numpy as jnp
from jax import lax
from jax.experimental import pallas as pl
from jax.experimental.pallas import tpu as pltpu

D_MODEL = 1024
HEADS = 8
DK = 128
DV = 128
CHUNK = 64
LRU_BLOCKS = 8
LRU_BW = 128
LRU_C = 8.0
CONV_W = 4
D_FF = 2816
EPS = 1e-6
LOG2_E = 1.4426950408889634
N_SPLITS = 8

SUBLANES = 8
LANES = 128
MXU_COLS = 256
VMEM_LIMIT_BYTES = 56 * 1024 * 1024

FFN_ROWS = 1024
MIX_ROWS = 256
SAMPLE_STATE_BLOCK = 8
STATE_RING = 3

BF16 = jnp.bfloat16
F32 = jnp.float32


def _dot(a, b):
    return jnp.dot(a, b, preferred_element_type=F32)


def _dot_tn(a, b):
    return lax.dot_general(a, b, (((0,), (0,)), ((), ())), preferred_element_type=F32)


def _dot_nt(a, b):
    return lax.dot_general(a, b, (((1,), (1,)), ((), ())), preferred_element_type=F32)


def _sigmoid(x):
    return 1.0 / (1.0 + jnp.exp2(x * (-LOG2_E)))


def _silu(x):
    return x * _sigmoid(x)


def _gelu_tanh(x):
    c = 0.7978845608028654
    return 0.5 * x * (1.0 + jnp.tanh(c * (x + 0.044715 * (x * x * x))))


def _rms(x, g):
    return x * lax.rsqrt(jnp.mean(x * x, axis=-1, keepdims=True) + EPS) * g


def _lower_bound(raw, layer):
    m = jnp.max(raw, axis=0, keepdims=True)
    e = jnp.exp(raw - m)
    den = jnp.sum(e, axis=0, keepdims=True)
    num = jnp.sum(e[0:layer + 1], axis=0, keepdims=True)
    return num / den


def _softplus(x):
    return jnp.maximum(x, 0.0) + jnp.log1p(jnp.exp(-jnp.abs(x)))


def _lru_coeffs(r_pre, i_pre, xc, lam):
    r = _sigmoid(r_pre)
    ig = _sigmoid(i_pre)
    log_a = (-LRU_C) * r * _softplus(-lam)
    a = jnp.exp(log_a)
    th = jnp.tanh(log_a)
    mult = jnp.sqrt(-2.0 * th / (1.0 - th))
    return a, mult * (ig * xc)


def _split3(x):
    p1 = x.astype(BF16)
    r1 = x - p1.astype(F32)
    p2 = r1.astype(BF16)
    r2 = r1 - p2.astype(F32)
    return p1, p2, r2.astype(BF16)


def _cast_rows(src_refs, dst_refs):
    for src, dst in zip(src_refs, dst_refs):
        dst[...] = src[...].astype(BF16)


def _cast_specs(weights, steps, step_index):
    specs, shapes = [], []
    for w in weights:
        assert w.ndim == 2
        count = max(c for c in range(1, steps + 1)
                    if w.shape[0] % c == 0 and (w.shape[0] // c) % (2 * SUBLANES) == 0)
        index = functools.partial(
            lambda *ids, last: (jnp.minimum(step_index(*ids), last), 0), last=count - 1)
        specs.append(pl.BlockSpec((w.shape[0] // count, w.shape[1]), index))
        shapes.append(jax.ShapeDtypeStruct(w.shape, BF16))
    return specs, shapes


def _ffn_kernel(xa_ref, xb_ref, g_ref, wg_ref, wu_ref, wd_ref, gf_ref, *rest, final_norm, a_tiles):
    n_cast = (len(rest) - 2) // 2
    oa_ref, ob_ref = rest[n_cast:n_cast + 2]
    _cast_rows(rest[:n_cast], rest[n_cast + 2:])

    def half_step(x_ref, o_ref):
        x = x_ref[...]
        inv_rms = lax.rsqrt(jnp.mean(x * x, axis=-1, keepdims=True) + EPS)
        h = (x * g_ref[...]).astype(BF16)
        gate = _dot(h, wg_ref[...]) * inv_rms
        up = _dot(h, wu_ref[...]) * inv_rms
        act = (_silu(gate) * up).astype(BF16)
        y = x + 0.5 * _dot(act, wd_ref[...])
        if final_norm:
            y = _rms(y, gf_ref[...])
        o_ref[...] = y

    i = pl.program_id(0)

    @pl.when(i < a_tiles)
    def _():
        half_step(xa_ref, oa_ref)

    @pl.when(i == a_tiles)
    def _():
        half_step(xb_ref, ob_ref)


def _resident(shape):
    nd = len(shape)
    return pl.BlockSpec(shape, lambda *_: (0,) * nd, pipeline_mode=pl.Buffered(1))


def _ffn(xa, xb, g, wg, wu, wd, gf, *, final_norm, cast=()):
    na, nb = xa.shape[0], xb.shape[0]
    assert na % FFN_ROWS == 0 and nb <= FFN_ROWS and nb % SUBLANES == 0
    a_tiles = na // FFN_ROWS
    a_tile = lambda i: (jnp.minimum(i, a_tiles - 1), 0)
    b_whole = lambda i: (0, 0)
    cast_specs, cast_shapes = _cast_specs(cast, a_tiles, lambda i: i)
    return pl.pallas_call(
        functools.partial(_ffn_kernel, final_norm=final_norm, a_tiles=a_tiles),
        grid=(a_tiles + 1,),
        in_specs=[
            pl.BlockSpec((FFN_ROWS, D_MODEL), a_tile),
            pl.BlockSpec((nb, D_MODEL), b_whole),
            _resident((1, D_MODEL)),
            _resident((D_MODEL, D_FF)),
            _resident((D_MODEL, D_FF)),
            _resident((D_FF, D_MODEL)),
            _resident((1, D_MODEL)),
            *cast_specs,
        ],
        out_specs=(pl.BlockSpec((FFN_ROWS, D_MODEL), a_tile),
                   pl.BlockSpec((nb, D_MODEL), b_whole),
                   *cast_specs),
        out_shape=(jax.ShapeDtypeStruct((na, D_MODEL), F32),
                   jax.ShapeDtypeStruct((nb, D_MODEL), F32),
                   *cast_shapes),
        name="ffn_final" if final_norm else "ffn",
        compiler_params=pltpu.CompilerParams(
            dimension_semantics=("arbitrary",), vmem_limit_bytes=VMEM_LIMIT_BYTES),
    )(xa, xb, g, wg, wu, wd, gf, *cast)


SCAN_SEG = 4
SCAN_GROUP = SCAN_SEG * SUBLANES
CONV_PAD = SUBLANES
PIECE_ROWS = 2 * SUBLANES
QA_W = 2 * LANES


MIX_INPUTS = 14
MIX_OUTPUTS = 4
MIX_SCRATCH = 6


def _mix_prompt_kernel(*refs):
    n_cast = (len(refs) - MIX_INPUTS - MIX_OUTPUTS - MIX_SCRATCH) // 2
    (x_ref, nmix_ref, win_ref, lbraw_ref, onorm_ref, convw_ref, convb_ref,
     wax_ref, ba_ref, bx_ref, lam_ref, waup_ref, wbup_ref, wout_ref) = refs[:MIX_INPUTS]
    outs = refs[MIX_INPUTS + n_cast:]
    xo_ref, s_ref, hl_ref, cv_ref = outs[:MIX_OUTPUTS]
    qa_s, ud_s, vb_s, oa_s, xr_s, h_s = outs[MIX_OUTPUTS + n_cast:]
    _cast_rows(refs[MIX_INPUTS:MIX_INPUTS + n_cast], outs[MIX_OUTPUTS:MIX_OUTPUTS + n_cast])

    rows = MIX_ROWS
    t = pl.program_id(1)

    @pl.when(t == 0)
    def _():
        s_ref[...] = jnp.zeros_like(s_ref)
        hl_ref[...] = jnp.zeros_like(hl_ref)
        xr_s[:, 0:CONV_PAD, :] = jnp.zeros((LRU_BLOCKS, CONV_PAD, LRU_BW), F32)

    x = x_ref[0]
    hb = _rms(x, nmix_ref[...]).astype(BF16)

    def proj(i):
        return _dot(hb, win_ref[:, i * D_MODEL:(i + 1) * D_MODEL])

    early = {}

    ncb = D_MODEL // MXU_COLS

    def issue_proj(i, cb):
        if cb < ncb and (i, cb) not in early:
            c0 = i * D_MODEL + cb * MXU_COLS
            early[(i, cb)] = _dot(hb, win_ref[:, c0:c0 + MXU_COLS])

    def gather(i):
        for cb in range(ncb):
            issue_proj(i, cb)
        return jnp.concatenate([early[(i, cb)] for cb in range(ncb)], axis=1)

    lb = _lower_bound(lbraw_ref[...], 0)
    q_act = _silu(proj(0))
    f = lb + (1.0 - lb) * _sigmoid(proj(1))
    kk = 1.0 - f
    vb_s[...] = proj(2).astype(BF16)

    ri = lax.broadcasted_iota(jnp.int32, (rows, rows), 0)
    ci = lax.broadcasted_iota(jnp.int32, (rows, rows), 1)
    same_chunk = (ri // CHUNK) == (ci // CHUNK)
    tril_blk = jnp.where((ri >= ci) & same_chunk, 1.0, 0.0).astype(BF16)
    logf = jnp.log(f)
    g1 = logf.astype(BF16)
    g2 = (logf - g1.astype(F32)).astype(BF16)
    b_all = _dot(tril_blk, g1) + _dot(tril_blk, g2)

    causal = (lax.broadcasted_iota(jnp.int32, (CHUNK, CHUNK), 0)
              >= lax.broadcasted_iota(jnp.int32, (CHUNK, CHUNK), 1))
    piece_row = lax.broadcasted_iota(jnp.int32, (PIECE_ROWS, D_MODEL), 0)
    ones_pv = jnp.ones((3 * PIECE_ROWS, DV), BF16)
    zeros_pv = jnp.zeros((3 * PIECE_ROWS, DV), BF16)
    zeros_cv = jnp.zeros((CHUNK, DV), BF16)

    nchunk = rows // CHUNK
    last = piece_row == PIECE_ROWS - 1
    zero_p = jnp.zeros((PIECE_ROWS, D_MODEL), BF16)

    for c in range(nchunk):
        issue_proj(3, c)
        rs = slice(c * CHUNK, (c + 1) * CHUNK)
        b = b_all[rs, :]
        bl = b[CHUNK - 1:CHUNK, :]
        qe = (q_act[rs, :] * jnp.exp(b)).astype(BF16)
        ke = (kk[rs, :] * jnp.exp(-b)).astype(BF16)
        kd = (kk[rs, :] * jnp.exp(bl - b)).astype(BF16)
        pieces = jnp.concatenate(
            [jnp.where(last, e, zero_p) for e in _split3(jnp.exp(b[CHUNK - PIECE_ROWS:, :]))], axis=0)
        for h in range(HEADS):
            hs = slice(h * DK, (h + 1) * DK)
            qa_s[rs, h * QA_W:h * QA_W + DK] = qe[:, hs]
            att = jnp.where(causal, _dot_nt(qe[:, hs], ke[:, hs]), 0.0)
            qa_s[rs, h * QA_W + DK:h * QA_W + DK + CHUNK] = att.astype(BF16)
            lhs = jnp.concatenate([kd[:, hs], pieces[:, hs]], axis=0)
            rhs = jnp.concatenate([jnp.concatenate([vb_s[rs, hs], zeros_cv], axis=1),
                                   jnp.concatenate([zeros_pv, ones_pv], axis=1)], axis=0)
            ud_s[c * HEADS + h] = _dot_tn(lhs, rhs)

    for c in range(nchunk):
        rs = slice(c * CHUNK, (c + 1) * CHUNK)
        for h in range(HEADS):
            hs = slice(h * DK, (h + 1) * DK)
            s_old = s_ref[0, h]
            rhs = jnp.concatenate([s_old.astype(BF16), vb_s[rs, hs]], axis=0)
            o = _dot(qa_s[rs, h * QA_W:h * QA_W + DK + CHUNK], rhs)
            oa_s[rs, hs] = o * lax.rsqrt(jnp.mean(o * o, axis=-1, keepdims=True) + EPS)
            ud = ud_s[c * HEADS + h]
            s_ref[0, h] = ud[:, DV:] * s_old + ud[:, 0:DV]

    o_a = (oa_s[...] * onorm_ref[...] * _silu(gather(3))).astype(BF16)
    up_a = {}

    def issue_up_a(cb):
        if cb < ncb and cb not in up_a:
            up_a[cb] = _dot(o_a, waup_ref[:, cb * MXU_COLS:(cb + 1) * MXU_COLS])

    xr = proj(4)
    h_prev = hl_ref[0]
    sub = lax.broadcasted_iota(jnp.int32, (SUBLANES, LANES), 0)
    ngroup = rows // SCAN_GROUP
    tails = []
    h_last = []
    for n in range(LRU_BLOCKS):
        ns = slice(n * LRU_BW, (n + 1) * LRU_BW)
        xr_s[n, CONV_PAD:CONV_PAD + rows, :] = xr[:, ns]
        taps = [jnp.broadcast_to(convw_ref[j:j + 1, ns], (SUBLANES, LRU_BW)) for j in range(CONV_W)]
        bias = jnp.broadcast_to(convb_ref[:, ns], (SUBLANES, LRU_BW))
        xc_parts = []
        for grp in range(ngroup):
            xin = {d: xr_s[n, pl.ds(CONV_PAD + grp * SCAN_GROUP + d, SUBLANES, stride=SCAN_SEG), :]
                   for d in range(1 - CONV_W, SCAN_SEG)}
            for j in range(SCAN_SEG):
                acc = bias
                for tap in range(CONV_W):
                    acc = acc + xin[j + tap - (CONV_W - 1)] * taps[tap]
                xc_parts.append(acc)
        xc = jnp.concatenate(xc_parts, axis=0)
        tail = xr_s[n, CONV_PAD + rows - (CONV_W - 1):CONV_PAD + rows, :]
        tails.append(tail)
        xr_s[n, CONV_PAD - (CONV_W - 1):CONV_PAD, :] = tail
        pre = _dot(xc.astype(BF16), wax_ref[n])
        issue_proj(5 + n % 2, n // 2)
        a, bterm = _lru_coeffs(pre[:, :LRU_BW] + ba_ref[:, ns], pre[:, LRU_BW:] + bx_ref[:, ns],
                               xc, lam_ref[:, ns])
        carry = h_prev[:, ns]
        for grp in range(ngroup):
            def vreg(v, j, grp=grp):
                r0 = grp * SCAN_GROUP + j * SUBLANES
                return v[r0:r0 + SUBLANES, :]

            def strided(j, grp=grp):
                return pl.ds(grp * SCAN_GROUP + j, SUBLANES, stride=SCAN_SEG)
            pa = [vreg(a, 0)]
            hh = [vreg(bterm, 0)]
            for j in range(1, SCAN_SEG):
                aj = vreg(a, j)
                hh.append(aj * hh[-1] + vreg(bterm, j))
                pa.append(aj * pa[-1])
            pi, hi = pa[-1], hh[-1]
            for d in (1, 2, 4):
                keep = sub >= d
                hi = pi * jnp.where(keep, pltpu.roll(hi, d, 0), 0.0) + hi
                pi = pi * jnp.where(keep, pltpu.roll(pi, d, 0), 1.0)
            first = sub >= 1
            seg_in = (jnp.where(first, pltpu.roll(pi, 1, 0), 1.0) * carry
                      + jnp.where(first, pltpu.roll(hi, 1, 0), 0.0))
            carry = pi[SUBLANES - 1:, :] * carry + hi[SUBLANES - 1:, :]
            for j in range(SCAN_SEG):
                h_s[n, strided(j), :] = pa[j] * seg_in + hh[j]
        h_last.append(carry)
        if n % 2 == 0:
            issue_proj(7, n // 2)
        else:
            issue_up_a(n // 2)
    hl_ref[0] = jnp.concatenate(h_last, axis=1)
    cv_ref[0] = jnp.concatenate(tails, axis=1)
    gel = _gelu_tanh(gather(5))
    o_b = jnp.concatenate(
        [h_s[n] * gel[:, n * LRU_BW:(n + 1) * LRU_BW] for n in range(LRU_BLOCKS)], axis=1).astype(BF16)

    for cb in range(ncb):
        issue_up_a(cb)
    m = (_sigmoid(gather(6)) * jnp.concatenate([up_a[cb] for cb in range(ncb)], axis=1)
         + _sigmoid(gather(7)) * _dot(o_b, wbup_ref[...]))
    xo_ref[0] = x + _dot(m.astype(BF16), wout_ref[...])


def _mix_prompt(x, p, cast=()):
    batch, seq, _ = x.shape
    rows = MIX_ROWS
    assert seq % rows == 0 and rows % CHUNK == 0 and rows >= CONV_W - 1
    tiles = seq // rows
    row_vec = _resident((1, D_MODEL))
    cast_specs, cast_shapes = _cast_specs(cast, batch * tiles, lambda b, t: b * tiles + t)
    out_shape = (
        jax.ShapeDtypeStruct((batch, seq, D_MODEL), F32),
        jax.ShapeDtypeStruct((batch, HEADS, DK, DV), F32),
        jax.ShapeDtypeStruct((batch, 1, D_MODEL), F32),
        jax.ShapeDtypeStruct((batch, CONV_W - 1, D_MODEL), F32),
        *cast_shapes,
    )
    return pl.pallas_call(
        _mix_prompt_kernel,
        grid=(batch, seq // rows),
        in_specs=[
            pl.BlockSpec((1, rows, D_MODEL), lambda b, t: (b, t, 0)),
            row_vec,
            _resident((D_MODEL, N_SPLITS * D_MODEL)),
            _resident(p["lb_raw"].shape),
            row_vec,
            _resident((CONV_W, D_MODEL)),
            row_vec,
            _resident((LRU_BLOCKS, LRU_BW, 2 * LRU_BW)),
            row_vec, row_vec, row_vec,
            _resident((D_MODEL, D_MODEL)),
            _resident((D_MODEL, D_MODEL)),
            _resident((D_MODEL, D_MODEL)),
            *cast_specs,
        ],
        out_specs=(
            pl.BlockSpec((1, rows, D_MODEL), lambda b, t: (b, t, 0)),
            pl.BlockSpec((1, HEADS, DK, DV), lambda b, t: (b, 0, 0, 0)),
            pl.BlockSpec((1, 1, D_MODEL), lambda b, t: (b, 0, 0)),
            pl.BlockSpec((1, CONV_W - 1, D_MODEL), lambda b, t: (b, 0, 0)),
            *cast_specs,
        ),
        out_shape=out_shape,
        scratch_shapes=[
            pltpu.VMEM((rows, HEADS * QA_W), BF16),
            pltpu.VMEM((rows // CHUNK * HEADS, DK, 2 * DV), F32),
            pltpu.VMEM((rows, D_MODEL), BF16),
            pltpu.VMEM((rows, D_MODEL), F32),
            pltpu.VMEM((LRU_BLOCKS, CONV_PAD + rows, LRU_BW), F32),
            pltpu.VMEM((LRU_BLOCKS, rows, LRU_BW), F32),
        ],
        name="mix_prompt",
        compiler_params=pltpu.CompilerParams(
            dimension_semantics=("arbitrary", "arbitrary"), vmem_limit_bytes=VMEM_LIMIT_BYTES),
    )(x, p["mix_norm"], p["w_in"], p["lb_raw"], p["o_norm"], p["conv_w"], p["conv_b"],
      p["w_ax"], p["b_a"], p["b_x"], p["lam"], p["w_a_up"], p["w_b_up"], p["w_out"], *cast)


def _mix_sample_kernel(x_ref, nmix_ref, win_ref, lbraw_ref, onorm_ref, convw_ref, convb_ref,
                       wax_ref, ba_ref, bx_ref, lam_ref, waup_ref, wbup_ref, wout_ref,
                       sin_hbm, hlin_ref, cvin_ref,
                       xo_ref, sout_ref, hl_ref, cv_ref,
                       qt_s, ft_s, v_s, oa_s, gsil_s, ga_s, mb_s, sring, ssem):
    nseq = x_ref.shape[0]
    blk = SAMPLE_STATE_BLOCK
    j = pl.program_id(0)
    nsteps = pl.num_programs(0)

    def state_copy(step):
        slot = lax.rem(step, STATE_RING)
        return pltpu.make_async_copy(sin_hbm.at[pl.ds(step * blk, blk)], sring.at[slot],
                                     ssem.at[slot])

    @pl.when(j == 0)
    def _():
        for ahead in range(STATE_RING):
            state_copy(ahead).start()

    @pl.when((j > 0) & (j + STATE_RING - 1 < nsteps))
    def _():
        state_copy(j + STATE_RING - 1).start()

    @pl.when(j == 0)
    def _():
        x = x_ref[...]
        hb = _rms(x, nmix_ref[...]).astype(BF16)

        def proj(i):
            return _dot(hb, win_ref[:, i * D_MODEL:(i + 1) * D_MODEL])

        lb = _lower_bound(lbraw_ref[...], 0)
        qt_s[...] = _silu(proj(0)).T
        ft_s[...] = (lb + (1.0 - lb) * _sigmoid(proj(1))).T
        v_s[...] = proj(2)
        gsil_s[...] = onorm_ref[...] * _silu(proj(3))

        xr = proj(4)
        xc = convb_ref[...] + xr * convw_ref[CONV_W - 1:CONV_W, :]
        for i in range(CONV_W - 1):
            xc = xc + cvin_ref[:, i, :] * convw_ref[i:i + 1, :]
        for i in range(CONV_W - 2):
            cv_ref[:, i, :] = cvin_ref[:, i + 1, :]
        cv_ref[:, CONV_W - 2, :] = xr

        xcb = xc.astype(BF16)
        h_parts = []
        for n in range(LRU_BLOCKS):
            ns = slice(n * LRU_BW, (n + 1) * LRU_BW)
            pre = _dot(xcb[:, ns], wax_ref[n])
            a, bterm = _lru_coeffs(pre[:, :LRU_BW] + ba_ref[:, ns], pre[:, LRU_BW:] + bx_ref[:, ns],
                                   xc[:, ns], lam_ref[:, ns])
            h_parts.append(a * hlin_ref[:, ns] + bterm)
        h_new = jnp.concatenate(h_parts, axis=1)
        hl_ref[...] = h_new
        o_b = (h_new * _gelu_tanh(proj(5))).astype(BF16)
        ga_s[...] = _sigmoid(proj(6))
        mb_s[...] = _sigmoid(proj(7)) * _dot(o_b, wbup_ref[...])

    state_copy(j).wait()
    sin_ref = sring.at[lax.rem(j, STATE_RING)]
    b0 = pl.multiple_of(j * blk, blk)
    shifts = [lax.rem(nseq - b0 + p * blk, nseq) for p in range(3)]

    lane = lax.broadcasted_iota(jnp.int32, (DK, nseq), 1)
    sel_r = lax.broadcasted_iota(jnp.int32, (nseq, blk * DV), 0)
    sel_c = lax.broadcasted_iota(jnp.int32, (nseq, blk * DV), 1)
    selector = jnp.where((sel_r < 3 * blk) & (jnp.bitwise_and(sel_r, blk - 1) == sel_c // DV),
                         1.0, 0.0).astype(BF16)

    def spread(cols_t):
        p1 = cols_t.astype(BF16).astype(F32)
        r1 = cols_t - p1
        p2 = r1.astype(BF16).astype(F32)
        p3 = r1 - p2
        lhs = jnp.where(lane < blk, pltpu.roll(p1, shifts[0], 1),
                        jnp.where(lane < 2 * blk, pltpu.roll(p2, shifts[1], 1),
                                  jnp.where(lane < 3 * blk, pltpu.roll(p3, shifts[2], 1), 0.0)))
        return _dot(lhs.astype(BF16), selector)

    for h in range(HEADS):
        hrows = slice(h * DK, (h + 1) * DK)
        q_all = spread(qt_s[hrows, :])
        f_all = spread(ft_s[hrows, :])
        v_rows = v_s[pl.ds(b0, blk), hrows]
        o_rows = []
        for i in range(blk):
            f_col = f_all[:, i * DV:(i + 1) * DV]
            s_new = f_col * sin_ref[i, h] + (1.0 - f_col) * v_rows[i:i + 1, :]
            sout_ref[i, h] = s_new
            o_rows.append(jnp.sum(q_all[:, i * DV:(i + 1) * DV] * s_new, axis=0, keepdims=True))
        o = jnp.concatenate(o_rows, axis=0)
        oa_s[pl.ds(b0, blk), hrows] = o * lax.rsqrt(
            jnp.mean(o * o, axis=-1, keepdims=True) + EPS)

    @pl.when(j == pl.num_programs(0) - 1)
    def _():
        o_a = (oa_s[...] * gsil_s[...]).astype(BF16)
        m = ga_s[...] * _dot(o_a, waup_ref[...]) + mb_s[...]
        xo_ref[...] = x_ref[...] + _dot(m.astype(BF16), wout_ref[...])


def _mix_sample(x2d, s_in, h_in, c_in, p):
    nseq = x2d.shape[0]
    blk = SAMPLE_STATE_BLOCK
    assert nseq == LANES and nseq % blk == 0
    row_vec = _resident((1, D_MODEL))
    tok = _resident((nseq, D_MODEL))
    conv_state = _resident((nseq, CONV_W - 1, D_MODEL))
    out_shape = (
        jax.ShapeDtypeStruct((nseq, D_MODEL), F32),
        jax.ShapeDtypeStruct((nseq, HEADS, DK, DV), F32),
        jax.ShapeDtypeStruct((nseq, D_MODEL), F32),
        jax.ShapeDtypeStruct((nseq, CONV_W - 1, D_MODEL), F32),
    )
    state_spec = pl.BlockSpec((blk, HEADS, DK, DV), lambda j: (j, 0, 0, 0))
    return pl.pallas_call(
        _mix_sample_kernel,
        grid=(nseq // blk,),
        in_specs=[
            tok, row_vec,
            _resident((D_MODEL, N_SPLITS * D_MODEL)),
            _resident(p["lb_raw"].shape),
            row_vec,
            _resident((CONV_W, D_MODEL)),
            row_vec,
            _resident((LRU_BLOCKS, LRU_BW, 2 * LRU_BW)),
            row_vec, row_vec, row_vec,
            _resident((D_MODEL, D_MODEL)),
            _resident((D_MODEL, D_MODEL)),
            _resident((D_MODEL, D_MODEL)),
            pl.BlockSpec(memory_space=pl.ANY), tok, conv_state,
        ],
        out_specs=(
            pl.BlockSpec((nseq, D_MODEL), lambda j: (0, 0)),
            state_spec,
            pl.BlockSpec((nseq, D_MODEL), lambda j: (0, 0)),
            pl.BlockSpec((nseq, CONV_W - 1, D_MODEL), lambda j: (0, 0, 0)),
        ),
        out_shape=out_shape,
        scratch_shapes=[
            pltpu.VMEM((D_MODEL, nseq), F32),
            pltpu.VMEM((D_MODEL, nseq), F32),
            pltpu.VMEM((nseq, D_MODEL), F32),
            pltpu.VMEM((nseq, D_MODEL), F32),
            pltpu.VMEM((nseq, D_MODEL), F32),
            pltpu.VMEM((nseq, D_MODEL), F32),
            pltpu.VMEM((nseq, D_MODEL), F32),
            pltpu.VMEM((STATE_RING, blk, HEADS, DK, DV), F32),
            pltpu.SemaphoreType.DMA((STATE_RING,)),
        ],
        name="mix_sample",
        compiler_params=pltpu.CompilerParams(
            dimension_semantics=("arbitrary",), vmem_limit_bytes=VMEM_LIMIT_BYTES),
    )(x2d, p["mix_norm"], p["w_in"], p["lb_raw"], p["o_norm"], p["conv_w"], p["conv_b"],
      p["w_ax"], p["b_a"], p["b_x"], p["lam"], p["w_a_up"], p["w_b_up"], p["w_out"],
      s_in, h_in, c_in)


def kernel(x_prompt, x_sample, state_hgrn, state_lru, state_conv, ffn1_norm, ffn1_w_gate, ffn1_w_up, ffn1_w_down, mix_norm, w_in, hgrn_lower_bounds, hgrn_out_norm, conv_w, conv_b, lru_w_a, lru_b_a, lru_w_x, lru_b_x, lru_lambda, w_a_up, w_b_up, w_out, ffn2_norm, ffn2_w_gate, ffn2_w_up, ffn2_w_down, final_norm):
    depth = w_in.shape[0]
    assert depth == 1, "single-layer trunk"
    l = 0
    row = lambda v: v.reshape(1, -1).astype(F32)
    bf = lambda w: w.astype(BF16)
    p = {
        "mix_norm": row(mix_norm[l]),
        "lb_raw": hgrn_lower_bounds.astype(F32),
        "o_norm": row(hgrn_out_norm[l]),
        "conv_w": conv_w[l].astype(F32),
        "conv_b": row(conv_b[l]),
        "b_a": row(lru_b_a[l]),
        "b_x": row(lru_b_x[l]),
        "lam": row(lru_lambda[l]),
    }
    f1 = (row(ffn1_norm[l]), bf(ffn1_w_gate[l]), bf(ffn1_w_up[l]), bf(ffn1_w_down[l]))
    gfin = row(final_norm)

    bp, tp, _ = x_prompt.shape
    bs, ts, _ = x_sample.shape
    assert ts == 1
    flat = lambda w: w.reshape(-1, w.shape[-1])
    w_ax = jnp.concatenate([lru_w_a[l], lru_w_x[l]], axis=-1)
    xp, xs, p["w_in"], p["w_a_up"], p["w_b_up"], p["w_out"], w_ax = _ffn(
        x_prompt.reshape(bp * tp, D_MODEL), x_sample.reshape(bs, D_MODEL), *f1, gfin,
        final_norm=False, cast=(flat(w_in), flat(w_a_up), flat(w_b_up), flat(w_out), flat(w_ax)))
    p["w_ax"] = w_ax.reshape(LRU_BLOCKS, LRU_BW, 2 * LRU_BW)
    xp, s_p, h_p, c_p, wg2, wu2, wd2 = _mix_prompt(
        xp.reshape(bp, tp, D_MODEL), p,
        cast=(flat(ffn2_w_gate), flat(ffn2_w_up), flat(ffn2_w_down)))
    f2 = (row(ffn2_norm[l]), wg2, wu2, wd2)
    xs, s_s, h_s, c_s = _mix_sample(xs, state_hgrn[l], state_lru[l], state_conv[l], p)
    y_p, y_s = _ffn(xp.reshape(bp * tp, D_MODEL), xs, *f2, gfin, final_norm=True)
    y_p = y_p.reshape(bp, tp, D_MODEL)
    y_s = y_s.reshape(bs, ts, D_MODEL)

    return (y_p, y_s, s_p[None], h_p.reshape(bp, D_MODEL)[None], c_p[None],
            s_s[None], h_s[None], c_s[None])
```

```python
import functools

import jax
import jax.numpy as jnp
from jax import lax
from jax.experimental import pallas as pl
from jax.experimental.pallas import tpu as pltpu

D_MODEL = 1024
HEADS = 8
DK = 128
DV = 128
CHUNK = 64
LRU_BLOCKS = 8
LRU_BW = 128
LRU_C = 8.0
CONV_W = 4
D_FF = 2816
EPS = 1e-6
LOG2_E = 1.4426950408889634
N_SPLITS = 8

SUBLANES = 8
LANES = 128
MXU_COLS = 256
VMEM_LIMIT_BYTES = 56 * 1024 * 1024

FFN_ROWS = 1024
MIX_ROWS = 256
SAMPLE_STATE_BLOCK = 8
STATE_RING = 3

BF16 = jnp.bfloat16
F32 = jnp.float32


def _dot(a, b):
    return jnp.dot(a, b, preferred_element_type=F32)


def _dot_tn(a, b):
    return lax.dot_general(a, b, (((0,), (0,)), ((), ())), preferred_element_type=F32)


def _dot_nt(a, b):
    return lax.dot_general(a, b, (((1,), (1,)), ((), ())), preferred_element_type=F32)


def _sigmoid(x):
    return 1.0 / (1.0 + jnp.exp2(x * (-LOG2_E)))


def _silu(x):
    return x * _sigmoid(x)


def _gelu_tanh(x):
    c = 0.7978845608028654
    return 0.5 * x * (1.0 + jnp.tanh(c * (x + 0.044715 * (x * x * x))))


def _rms(x, g):
    return x * lax.rsqrt(jnp.mean(x * x, axis=-1, keepdims=True) + EPS) * g


def _lower_bound(raw, layer):
    m = jnp.max(raw, axis=0, keepdims=True)
    e = jnp.exp(raw - m)
    den = jnp.sum(e, axis=0, keepdims=True)
    num = jnp.sum(e[0:layer + 1], axis=0, keepdims=True)
    return num / den


def _softplus(x):
    return jnp.maximum(x, 0.0) + jnp.log1p(jnp.exp(-jnp.abs(x)))


def _lru_coeffs(r_pre, i_pre, xc, lam):
    r = _sigmoid(r_pre)
    ig = _sigmoid(i_pre)
    log_a = (-LRU_C) * r * _softplus(-lam)
    a = jnp.exp(log_a)
    th = jnp.tanh(log_a)
    mult = jnp.sqrt(-2.0 * th / (1.0 - th))
    return a, mult * (ig * xc)


def _split3(x):
    p1 = x.astype(BF16)
    r1 = x - p1.astype(F32)
    p2 = r1.astype(BF16)
    r2 = r1 - p2.astype(F32)
    return p1, p2, r2.astype(BF16)


def _cast_rows(src_refs, dst_refs):
    for src, dst in zip(src_refs, dst_refs):
        dst[...] = src[...].astype(BF16)


def _cast_specs(weights, steps, step_index):
    specs, shapes = [], []
    for w in weights:
        assert w.ndim == 2
        count = max(c for c in range(1, steps + 1)
                    if w.shape[0] % c == 0 and (w.shape[0] // c) % (2 * SUBLANES) == 0)
        index = functools.partial(
            lambda *ids, last: (jnp.minimum(step_index(*ids), last), 0), last=count - 1)
        specs.append(pl.BlockSpec((w.shape[0] // count, w.shape[1]), index))
        shapes.append(jax.ShapeDtypeStruct(w.shape, BF16))
    return specs, shapes


def _ffn_kernel(xa_ref, xb_ref, g_ref, wg_ref, wu_ref, wd_ref, gf_ref, *rest, final_norm, a_tiles):
    n_cast = (len(rest) - 2) // 2
    oa_ref, ob_ref = rest[n_cast:n_cast + 2]
    _cast_rows(rest[:n_cast], rest[n_cast + 2:])

    def half_step(x_ref, o_ref):
        x = x_ref[...]
        inv_rms = lax.rsqrt(jnp.mean(x * x, axis=-1, keepdims=True) + EPS)
        h = (x * g_ref[...]).astype(BF16)
        gate = _dot(h, wg_ref[...]) * inv_rms
        up = _dot(h, wu_ref[...]) * inv_rms
        act = (_silu(gate) * up).astype(BF16)
        y = x + 0.5 * _dot(act, wd_ref[...])
        if final_norm:
            y = _rms(y, gf_ref[...])
        o_ref[...] = y

    i = pl.program_id(0)

    @pl.when(i < a_tiles)
    def _():
        half_step(xa_ref, oa_ref)

    @pl.when(i == a_tiles)
    def _():
        half_step(xb_ref, ob_ref)


def _resident(shape):
    nd = len(shape)
    return pl.BlockSpec(shape, lambda *_: (0,) * nd, pipeline_mode=pl.Buffered(1))


def _ffn(xa, xb, g, wg, wu, wd, gf, *, final_norm, cast=()):
    na, nb = xa.shape[0], xb.shape[0]
    assert na % FFN_ROWS == 0 and nb <= FFN_ROWS and nb % SUBLANES == 0
    a_tiles = na // FFN_ROWS
    a_tile = lambda i: (jnp.minimum(i, a_tiles - 1), 0)
    b_whole = lambda i: (0, 0)
    cast_specs, cast_shapes = _cast_specs(cast, a_tiles, lambda i: i)
    return pl.pallas_call(
        functools.partial(_ffn_kernel, final_norm=final_norm, a_tiles=a_tiles),
        grid=(a_tiles + 1,),
        in_specs=[
            pl.BlockSpec((FFN_ROWS, D_MODEL), a_tile),
            pl.BlockSpec((nb, D_MODEL), b_whole),
            _resident((1, D_MODEL)),
            _resident((D_MODEL, D_FF)),
            _resident((D_MODEL, D_FF)),
            _resident((D_FF, D_MODEL)),
            _resident((1, D_MODEL)),
            *cast_specs,
        ],
        out_specs=(pl.BlockSpec((FFN_ROWS, D_MODEL), a_tile),
                   pl.BlockSpec((nb, D_MODEL), b_whole),
                   *cast_specs),
        out_shape=(jax.ShapeDtypeStruct((na, D_MODEL), F32),
                   jax.ShapeDtypeStruct((nb, D_MODEL), F32),
                   *cast_shapes),
        name="ffn_final" if final_norm else "ffn",
        compiler_params=pltpu.CompilerParams(
            dimension_semantics=("arbitrary",), vmem_limit_bytes=VMEM_LIMIT_BYTES),
    )(xa, xb, g, wg, wu, wd, gf, *cast)


SCAN_SEG = 4
SCAN_GROUP = SCAN_SEG * SUBLANES
CONV_PAD = SUBLANES
PIECE_ROWS = 2 * SUBLANES
QA_W = 2 * LANES


MIX_INPUTS = 14
MIX_OUTPUTS = 4
MIX_SCRATCH = 6


def _mix_prompt_kernel(*refs):
    n_cast = (len(refs) - MIX_INPUTS - MIX_OUTPUTS - MIX_SCRATCH) // 2
    (x_ref, nmix_ref, win_ref, lbraw_ref, onorm_ref, convw_ref, convb_ref,
     wax_ref, ba_ref, bx_ref, lam_ref, waup_ref, wbup_ref, wout_ref) = refs[:MIX_INPUTS]
    outs = refs[MIX_INPUTS + n_cast:]
    xo_ref, s_ref, hl_ref, cv_ref = outs[:MIX_OUTPUTS]
    qa_s, ud_s, vb_s, oa_s, xr_s, h_s = outs[MIX_OUTPUTS + n_cast:]
    _cast_rows(refs[MIX_INPUTS:MIX_INPUTS + n_cast], outs[MIX_OUTPUTS:MIX_OUTPUTS + n_cast])

    rows = MIX_ROWS
    t = pl.program_id(1)

    @pl.when(t == 0)
    def _():
        s_ref[...] = jnp.zeros_like(s_ref)
        hl_ref[...] = jnp.zeros_like(hl_ref)
        xr_s[:, 0:CONV_PAD, :] = jnp.zeros((LRU_BLOCKS, CONV_PAD, LRU_BW), F32)

    x = x_ref[0]
    hb = _rms(x, nmix_ref[...]).astype(BF16)

    def proj(i):
        return _dot(hb, win_ref[:, i * D_MODEL:(i + 1) * D_MODEL])

    early = {}

    ncb = D_MODEL // MXU_COLS

    def issue_proj(i, cb):
        if cb < ncb and (i, cb) not in early:
            c0 = i * D_MODEL + cb * MXU_COLS
            early[(i, cb)] = _dot(hb, win_ref[:, c0:c0 + MXU_COLS])

    def gather(i):
        for cb in range(ncb):
            issue_proj(i, cb)
        return jnp.concatenate([early[(i, cb)] for cb in range(ncb)], axis=1)

    lb = _lower_bound(lbraw_ref[...], 0)
    q_act = _silu(proj(0))
    f = lb + (1.0 - lb) * _sigmoid(proj(1))
    kk = 1.0 - f
    vb_s[...] = proj(2).astype(BF16)

    ri = lax.broadcasted_iota(jnp.int32, (rows, rows), 0)
    ci = lax.broadcasted_iota(jnp.int32, (rows, rows), 1)
    same_chunk = (ri // CHUNK) == (ci // CHUNK)
    tril_blk = jnp.where((ri >= ci) & same_chunk, 1.0, 0.0).astype(BF16)
    logf = jnp.log(f)
    g1 = logf.astype(BF16)
    g2 = (logf - g1.astype(F32)).astype(BF16)
    b_all = _dot(tril_blk, g1) + _dot(tril_blk, g2)

    causal = (lax.broadcasted_iota(jnp.int32, (CHUNK, CHUNK), 0)
              >= lax.broadcasted_iota(jnp.int32, (CHUNK, CHUNK), 1))
    piece_row = lax.broadcasted_iota(jnp.int32, (PIECE_ROWS, D_MODEL), 0)
    ones_pv = jnp.ones((3 * PIECE_ROWS, DV), BF16)
    zeros_pv = jnp.zeros((3 * PIECE_ROWS, DV), BF16)
    zeros_cv = jnp.zeros((CHUNK, DV), BF16)

    nchunk = rows // CHUNK
    last = piece_row == PIECE_ROWS - 1
    zero_p = jnp.zeros((PIECE_ROWS, D_MODEL), BF16)

    for c in range(nchunk):
        issue_proj(3, c)
        rs = slice(c * CHUNK, (c + 1) * CHUNK)
        b = b_all[rs, :]
        bl = b[CHUNK - 1:CHUNK, :]
        qe = (q_act[rs, :] * jnp.exp(b)).astype(BF16)
        ke = (kk[rs, :] * jnp.exp(-b)).astype(BF16)
        kd = (kk[rs, :] * jnp.exp(bl - b)).astype(BF16)
        pieces = jnp.concatenate(
            [jnp.where(last, e, zero_p) for e in _split3(jnp.exp(b[CHUNK - PIECE_ROWS:, :]))], axis=0)
        for h in range(HEADS):
            hs = slice(h * DK, (h + 1) * DK)
            qa_s[rs, h * QA_W:h * QA_W + DK] = qe[:, hs]
            att = jnp.where(causal, _dot_nt(qe[:, hs], ke[:, hs]), 0.0)
            qa_s[rs, h * QA_W + DK:h * QA_W + DK + CHUNK] = att.astype(BF16)
            lhs = jnp.concatenate([kd[:, hs], pieces[:, hs]], axis=0)
            rhs = jnp.concatenate([jnp.concatenate([vb_s[rs, hs], zeros_cv], axis=1),
                                   jnp.concatenate([zeros_pv, ones_pv], axis=1)], axis=0)
            ud_s[c * HEADS + h] = _dot_tn(lhs, rhs)

    for c in range(nchunk):
        rs = slice(c * CHUNK, (c + 1) * CHUNK)
        for h in range(HEADS):
            hs = slice(h * DK, (h + 1) * DK)
            s_old = s_ref[0, h]
            rhs = jnp.concatenate([s_old.astype(BF16), vb_s[rs, hs]], axis=0)
            o = _dot(qa_s[rs, h * QA_W:h * QA_W + DK + CHUNK], rhs)
            oa_s[rs, hs] = o * lax.rsqrt(jnp.mean(o * o, axis=-1, keepdims=True) + EPS)
            ud = ud_s[c * HEADS + h]
            s_ref[0, h] = ud[:, DV:] * s_old + ud[:, 0:DV]

    o_a = (oa_s[...] * onorm_ref[...] * _silu(gather(3))).astype(BF16)
    up_a = {}

    def issue_up_a(cb):
        if cb < ncb and cb not in up_a:
            up_a[cb] = _dot(o_a, waup_ref[:, cb * MXU_COLS:(cb + 1) * MXU_COLS])

    xr = proj(4)
    h_prev = hl_ref[0]
    sub = lax.broadcasted_iota(jnp.int32, (SUBLANES, LANES), 0)
    ngroup = rows // SCAN_GROUP
    tails = []
    h_last = []
    for n in range(LRU_BLOCKS):
        ns = slice(n * LRU_BW, (n + 1) * LRU_BW)
        xr_s[n, CONV_PAD:CONV_PAD + rows, :] = xr[:, ns]
        taps = [jnp.broadcast_to(convw_ref[j:j + 1, ns], (SUBLANES, LRU_BW)) for j in range(CONV_W)]
        bias = jnp.broadcast_to(convb_ref[:, ns], (SUBLANES, LRU_BW))
        xc_parts = []
        for grp in range(ngroup):
            xin = {d: xr_s[n, pl.ds(CONV_PAD + grp * SCAN_GROUP + d, SUBLANES, stride=SCAN_SEG), :]
                   for d in range(1 - CONV_W, SCAN_SEG)}
            for j in range(SCAN_SEG):
                acc = bias
                for tap in range(CONV_W):
                    acc = acc + xin[j + tap - (CONV_W - 1)] * taps[tap]
                xc_parts.append(acc)
        xc = jnp.concatenate(xc_parts, axis=0)
        tail = xr_s[n, CONV_PAD + rows - (CONV_W - 1):CONV_PAD + rows, :]
        tails.append(tail)
        xr_s[n, CONV_PAD - (CONV_W - 1):CONV_PAD, :] = tail
        pre = _dot(xc.astype(BF16), wax_ref[n])
        issue_proj(5 + n % 2, n // 2)
        a, bterm = _lru_coeffs(pre[:, :LRU_BW] + ba_ref[:, ns], pre[:, LRU_BW:] + bx_ref[:, ns],
                               xc, lam_ref[:, ns])
        carry = h_prev[:, ns]
        for grp in range(ngroup):
            def vreg(v, j, grp=grp):
                r0 = grp * SCAN_GROUP + j * SUBLANES
                return v[r0:r0 + SUBLANES, :]

            def strided(j, grp=grp):
                return pl.ds(grp * SCAN_GROUP + j, SUBLANES, stride=SCAN_SEG)
            pa = [vreg(a, 0)]
            hh = [vreg(bterm, 0)]
            for j in range(1, SCAN_SEG):
                aj = vreg(a, j)
                hh.append(aj * hh[-1] + vreg(bterm, j))
                pa.append(aj * pa[-1])
            pi, hi = pa[-1], hh[-1]
            for d in (1, 2, 4):
                keep = sub >= d
                hi = pi * jnp.where(keep, pltpu.roll(hi, d, 0), 0.0) + hi
                pi = pi * jnp.where(keep, pltpu.roll(pi, d, 0), 1.0)
            first = sub >= 1
            seg_in = (jnp.where(first, pltpu.roll(pi, 1, 0), 1.0) * carry
                      + jnp.where(first, pltpu.roll(hi, 1, 0), 0.0))
            carry = pi[SUBLANES - 1:, :] * carry + hi[SUBLANES - 1:, :]
            for j in range(SCAN_SEG):
                h_s[n, strided(j), :] = pa[j] * seg_in + hh[j]
        h_last.append(carry)
        if n % 2 == 0:
            issue_proj(7, n // 2)
        else:
            issue_up_a(n // 2)
    hl_ref[0] = jnp.concatenate(h_last, axis=1)
    cv_ref[0] = jnp.concatenate(tails, axis=1)
    gel = _gelu_tanh(gather(5))
    o_b = jnp.concatenate(
        [h_s[n] * gel[:, n * LRU_BW:(n + 1) * LRU_BW] for n in range(LRU_BLOCKS)], axis=1).astype(BF16)

    for cb in range(ncb):
        issue_up_a(cb)
    m = (_sigmoid(gather(6)) * jnp.concatenate([up_a[cb] for cb in range(ncb)], axis=1)
         + _sigmoid(gather(7)) * _dot(o_b, wbup_ref[...]))
    xo_ref[0] = x + _dot(m.astype(BF16), wout_ref[...])


def _mix_prompt(x, p, cast=()):
    batch, seq, _ = x.shape
    rows = MIX_ROWS
    assert seq % rows == 0 and rows % CHUNK == 0 and rows >= CONV_W - 1
    tiles = seq // rows
    row_vec = _resident((1, D_MODEL))
    cast_specs, cast_shapes = _cast_specs(cast, batch * tiles, lambda b, t: b * tiles + t)
    out_shape = (
        jax.ShapeDtypeStruct((batch, seq, D_MODEL), F32),
        jax.ShapeDtypeStruct((batch, HEADS, DK, DV), F32),
        jax.ShapeDtypeStruct((batch, 1, D_MODEL), F32),
        jax.ShapeDtypeStruct((batch, CONV_W - 1, D_MODEL), F32),
        *cast_shapes,
    )
    return pl.pallas_call(
        _mix_prompt_kernel,
        grid=(batch, seq // rows),
        in_specs=[
            pl.BlockSpec((1, rows, D_MODEL), lambda b, t: (b, t, 0)),
            row_vec,
            _resident((D_MODEL, N_SPLITS * D_MODEL)),
            _resident(p["lb_raw"].shape),
            row_vec,
            _resident((CONV_W, D_MODEL)),
            row_vec,
            _resident((LRU_BLOCKS, LRU_BW, 2 * LRU_BW)),
            row_vec, row_vec, row_vec,
            _resident((D_MODEL, D_MODEL)),
            _resident((D_MODEL, D_MODEL)),
            _resident((D_MODEL, D_MODEL)),
            *cast_specs,
        ],
        out_specs=(
            pl.BlockSpec((1, rows, D_MODEL), lambda b, t: (b, t, 0)),
            pl.BlockSpec((1, HEADS, DK, DV), lambda b, t: (b, 0, 0, 0)),
            pl.BlockSpec((1, 1, D_MODEL), lambda b, t: (b, 0, 0)),
            pl.BlockSpec((1, CONV_W - 1, D_MODEL), lambda b, t: (b, 0, 0)),
            *cast_specs,
        ),
        out_shape=out_shape,
        scratch_shapes=[
            pltpu.VMEM((rows, HEADS * QA_W), BF16),
            pltpu.VMEM((rows // CHUNK * HEADS, DK, 2 * DV), F32),
            pltpu.VMEM((rows, D_MODEL), BF16),
            pltpu.VMEM((rows, D_MODEL), F32),
            pltpu.VMEM((LRU_BLOCKS, CONV_PAD + rows, LRU_BW), F32),
            pltpu.VMEM((LRU_BLOCKS, rows, LRU_BW), F32),
        ],
        name="mix_prompt",
        compiler_params=pltpu.CompilerParams(
            dimension_semantics=("arbitrary", "arbitrary"), vmem_limit_bytes=VMEM_LIMIT_BYTES),
    )(x, p["mix_norm"], p["w_in"], p["lb_raw"], p["o_norm"], p["conv_w"], p["conv_b"],
      p["w_ax"], p["b_a"], p["b_x"], p["lam"], p["w_a_up"], p["w_b_up"], p["w_out"], *cast)


def _mix_sample_kernel(x_ref, nmix_ref, win_ref, lbraw_ref, onorm_ref, convw_ref, convb_ref,
                       wax_ref, ba_ref, bx_ref, lam_ref, waup_hbm, wbup_ref, wout_hbm,
                       sin_hbm, hlin_ref, cvin_ref,
                       xo_ref, sout_ref, hl_ref, cv_ref,
                       qt_s, ft_s, v_s, oa_s, gsil_s, ga_s, mb_s, sring, ssem, wlate, wsem):
    nseq = x_ref.shape[0]
    blk = SAMPLE_STATE_BLOCK

    def late_copy(k):
        return pltpu.make_async_copy((waup_hbm, wout_hbm)[k], wlate.at[k], wsem.at[k])
    j = pl.program_id(0)
    nsteps = pl.num_programs(0)

    def state_copy(step):
        slot = lax.rem(step, STATE_RING)
        return pltpu.make_async_copy(sin_hbm.at[pl.ds(step * blk, blk)], sring.at[slot],
                                     ssem.at[slot])

    @pl.when(j == 0)
    def _():
        for ahead in range(STATE_RING):
            state_copy(ahead).start()
        late_copy(0).start()
        late_copy(1).start()

    @pl.when((j > 0) & (j + STATE_RING - 1 < nsteps))
    def _():
        state_copy(j + STATE_RING - 1).start()

    @pl.when(j == 0)
    def _():
        x = x_ref[...]
        hb = _rms(x, nmix_ref[...]).astype(BF16)

        def proj(i):
            return _dot(hb, win_ref[:, i * D_MODEL:(i + 1) * D_MODEL])

        lb = _lower_bound(lbraw_ref[...], 0)
        qt_s[...] = _silu(proj(0)).T
        ft_s[...] = (lb + (1.0 - lb) * _sigmoid(proj(1))).T
        v_s[...] = proj(2)
        gsil_s[...] = onorm_ref[...] * _silu(proj(3))

        xr = proj(4)
        xc = convb_ref[...] + xr * convw_ref[CONV_W - 1:CONV_W, :]
        for i in range(CONV_W - 1):
            xc = xc + cvin_ref[:, i, :] * convw_ref[i:i + 1, :]
        for i in range(CONV_W - 2):
            cv_ref[:, i, :] = cvin_ref[:, i + 1, :]
        cv_ref[:, CONV_W - 2, :] = xr

        xcb = xc.astype(BF16)
        h_parts = []
        for n in range(LRU_BLOCKS):
            ns = slice(n * LRU_BW, (n + 1) * LRU_BW)
            pre = _dot(xcb[:, ns], wax_ref[n])
            a, bterm = _lru_coeffs(pre[:, :LRU_BW] + ba_ref[:, ns], pre[:, LRU_BW:] + bx_ref[:, ns],
                                   xc[:, ns], lam_ref[:, ns])
            h_parts.append(a * hlin_ref[:, ns] + bterm)
        h_new = jnp.concatenate(h_parts, axis=1)
        hl_ref[...] = h_new
        o_b = (h_new * _gelu_tanh(proj(5))).astype(BF16)
        ga_s[...] = _sigmoid(proj(6))
        mb_s[...] = _sigmoid(proj(7)) * _dot(o_b, wbup_ref[...])

    state_copy(j).wait()
    sin_ref = sring.at[lax.rem(j, STATE_RING)]
    b0 = pl.multiple_of(j * blk, blk)
    shifts = [lax.rem(nseq - b0 + p * blk, nseq) for p in range(3)]

    lane = lax.broadcasted_iota(jnp.int32, (DK, nseq), 1)
    sel_r = lax.broadcasted_iota(jnp.int32, (nseq, blk * DV), 0)
    sel_c = lax.broadcasted_iota(jnp.int32, (nseq, blk * DV), 1)
    selector = jnp.where((sel_r < 3 * blk) & (jnp.bitwise_and(sel_r, blk - 1) == sel_c // DV),
                         1.0, 0.0).astype(BF16)

    def spread(cols_t):
        p1 = cols_t.astype(BF16).astype(F32)
        r1 = cols_t - p1
        p2 = r1.astype(BF16).astype(F32)
        p3 = r1 - p2
        lhs = jnp.where(lane < blk, pltpu.roll(p1, shifts[0], 1),
                        jnp.where(lane < 2 * blk, pltpu.roll(p2, shifts[1], 1),
                                  jnp.where(lane < 3 * blk, pltpu.roll(p3, shifts[2], 1), 0.0)))
        return _dot(lhs.astype(BF16), selector)

    for h in range(HEADS):
        hrows = slice(h * DK, (h + 1) * DK)
        q_all = spread(qt_s[hrows, :])
        f_all = spread(ft_s[hrows, :])
        v_rows = v_s[pl.ds(b0, blk), hrows]
        o_rows = []
        for i in range(blk):
            f_col = f_all[:, i * DV:(i + 1) * DV]
            s_new = f_col * sin_ref[i, h] + (1.0 - f_col) * v_rows[i:i + 1, :]
            sout_ref[i, h] = s_new
            o_rows.append(jnp.sum(q_all[:, i * DV:(i + 1) * DV] * s_new, axis=0, keepdims=True))
        o = jnp.concatenate(o_rows, axis=0)
        oa_s[pl.ds(b0, blk), hrows] = o * lax.rsqrt(
            jnp.mean(o * o, axis=-1, keepdims=True) + EPS)

    @pl.when(j == pl.num_programs(0) - 1)
    def _():
        late_copy(0).wait()
        late_copy(1).wait()
        o_a = (oa_s[...] * gsil_s[...]).astype(BF16)
        m = ga_s[...] * _dot(o_a, wlate[0]) + mb_s[...]
        xo_ref[...] = x_ref[...] + _dot(m.astype(BF16), wlate[1])


def _mix_sample(x2d, s_in, h_in, c_in, p):
    nseq = x2d.shape[0]
    blk = SAMPLE_STATE_BLOCK
    assert nseq == LANES and nseq % blk == 0
    row_vec = _resident((1, D_MODEL))
    tok = _resident((nseq, D_MODEL))
    conv_state = _resident((nseq, CONV_W - 1, D_MODEL))
    out_shape = (
        jax.ShapeDtypeStruct((nseq, D_MODEL), F32),
        jax.ShapeDtypeStruct((nseq, HEADS, DK, DV), F32),
        jax.ShapeDtypeStruct((nseq, D_MODEL), F32),
        jax.ShapeDtypeStruct((nseq, CONV_W - 1, D_MODEL), F32),
    )
    state_spec = pl.BlockSpec((blk, HEADS, DK, DV), lambda j: (j, 0, 0, 0))
    return pl.pallas_call(
        _mix_sample_kernel,
        grid=(nseq // blk,),
        in_specs=[
            tok, row_vec,
            _resident((D_MODEL, N_SPLITS * D_MODEL)),
            _resident(p["lb_raw"].shape),
            row_vec,
            _resident((CONV_W, D_MODEL)),
            row_vec,
            _resident((LRU_BLOCKS, LRU_BW, 2 * LRU_BW)),
            row_vec, row_vec, row_vec,
            pl.BlockSpec(memory_space=pl.ANY),
            _resident((D_MODEL, D_MODEL)),
            pl.BlockSpec(memory_space=pl.ANY),
            pl.BlockSpec(memory_space=pl.ANY), tok, conv_state,
        ],
        out_specs=(
            pl.BlockSpec((nseq, D_MODEL), lambda j: (0, 0)),
            state_spec,
            pl.BlockSpec((nseq, D_MODEL), lambda j: (0, 0)),
            pl.BlockSpec((nseq, CONV_W - 1, D_MODEL), lambda j: (0, 0, 0)),
        ),
        out_shape=out_shape,
        scratch_shapes=[
            pltpu.VMEM((D_MODEL, nseq), F32),
            pltpu.VMEM((D_MODEL, nseq), F32),
            pltpu.VMEM((nseq, D_MODEL), F32),
            pltpu.VMEM((nseq, D_MODEL), F32),
            pltpu.VMEM((nseq, D_MODEL), F32),
            pltpu.VMEM((nseq, D_MODEL), F32),
            pltpu.VMEM((nseq, D_MODEL), F32),
            pltpu.VMEM((STATE_RING, blk, HEADS, DK, DV), F32),
            pltpu.SemaphoreType.DMA((STATE_RING,)),
            pltpu.VMEM((2, D_MODEL, D_MODEL), BF16),
            pltpu.SemaphoreType.DMA((2,)),
        ],
        name="mix_sample",
        compiler_params=pltpu.CompilerParams(
            dimension_semantics=("arbitrary",), vmem_limit_bytes=VMEM_LIMIT_BYTES),
    )(x2d, p["mix_norm"], p["w_in"], p["lb_raw"], p["o_norm"], p["conv_w"], p["conv_b"],
      p["w_ax"], p["b_a"], p["b_x"], p["lam"], p["w_a_up"], p["w_b_up"], p["w_out"],
      s_in, h_in, c_in)


def kernel(x_prompt, x_sample, state_hgrn, state_lru, state_conv, ffn1_norm, ffn1_w_gate, ffn1_w_up, ffn1_w_down, mix_norm, w_in, hgrn_lower_bounds, hgrn_out_norm, conv_w, conv_b, lru_w_a, lru_b_a, lru_w_x, lru_b_x, lru_lambda, w_a_up, w_b_up, w_out, ffn2_norm, ffn2_w_gate, ffn2_w_up, ffn2_w_down, final_norm):
    depth = w_in.shape[0]
    assert depth == 1, "single-layer trunk"
    l = 0
    row = lambda v: v.reshape(1, -1).astype(F32)
    bf = lambda w: w.astype(BF16)
    p = {
        "mix_norm": row(mix_norm[l]),
        "lb_raw": hgrn_lower_bounds.astype(F32),
        "o_norm": row(hgrn_out_norm[l]),
        "conv_w": conv_w[l].astype(F32),
        "conv_b": row(conv_b[l]),
        "b_a": row(lru_b_a[l]),
        "b_x": row(lru_b_x[l]),
        "lam": row(lru_lambda[l]),
    }
    f1 = (row(ffn1_norm[l]), bf(ffn1_w_gate[l]), bf(ffn1_w_up[l]), bf(ffn1_w_down[l]))
    gfin = row(final_norm)

    bp, tp, _ = x_prompt.shape
    bs, ts, _ = x_sample.shape
    assert ts == 1
    flat = lambda w: w.reshape(-1, w.shape[-1])
    w_ax = jnp.concatenate([lru_w_a[l], lru_w_x[l]], axis=-1)
    xp, xs, p["w_in"], p["w_a_up"], p["w_b_up"], p["w_out"], w_ax = _ffn(
        x_prompt.reshape(bp * tp, D_MODEL), x_sample.reshape(bs, D_MODEL), *f1, gfin,
        final_norm=False, cast=(flat(w_in), flat(w_a_up), flat(w_b_up), flat(w_out), flat(w_ax)))
    p["w_ax"] = w_ax.reshape(LRU_BLOCKS, LRU_BW, 2 * LRU_BW)
    xp, s_p, h_p, c_p, wg2, wu2, wd2 = _mix_prompt(
        xp.reshape(bp, tp, D_MODEL), p,
        cast=(flat(ffn2_w_gate), flat(ffn2_w_up), flat(ffn2_w_down)))
    f2 = (row(ffn2_norm[l]), wg2, wu2, wd2)
    xs, s_s, h_s, c_s = _mix_sample(xs, state_hgrn[l], state_lru[l], state_conv[l], p)
    y_p, y_s = _ffn(xp.reshape(bp * tp, D_MODEL), xs, *f2, gfin, final_norm=True)
    y_p = y_p.reshape(bp, tp, D_MODEL)
    y_s = y_s.reshape(bs, ts, D_MODEL)

    return (y_p, y_s, s_p[None], h_p.reshape(bp, D_MODEL)[None], c_p[None],
            s_s[None], h_s[None], c_s[None])
```
